```python
import functools
import jax, jax.numpy as jnp
from jax import lax
import numpy as np

D_MODEL = 1024
BATCH = 8
SEQ = 2048
DEPTH = 1
DEC_BATCH = 128
DEC_SEQ = 4
PAST_LEN = 8192
PAGE_SIZE = 128

RET_HEADS = 4
RET_DK = 128
RET_DV = 256
RET_CHUNK = 128
MLA_HEADS = 8
MLA_Q_LORA = 384
MLA_KV_LORA = 256
MLA_D_NOPE = 128
MLA_D_ROPE = 64
MLA_D_V = 128
MLA_Q_BLOCK = 128
N_MEM = 256
X_HEADS = 4
X_HD = 64
N_BRANCH = 3
D_FF = -(-8 * D_MODEL // (3 * 256)) * 256
ROPE_BASE = 10000.0
RMS_EPS = 1e-6
IN_SIZES = (RET_HEADS * RET_DK, RET_HEADS * RET_DK, RET_HEADS * RET_DV, RET_HEADS * RET_DV,
            MLA_Q_LORA, MLA_KV_LORA, MLA_D_ROPE, X_HEADS * X_HD, N_BRANCH * D_MODEL)
D_IN = (2 * RET_HEADS * RET_DK + 2 * RET_HEADS * RET_DV + MLA_Q_LORA + MLA_KV_LORA
        + MLA_D_ROPE + X_HEADS * X_HD + N_BRANCH * D_MODEL)

kernel_name = 'hybrid_retention_mla_memory_decoder_step'


def rms_norm(x, g):
    xf = x.astype(jnp.float32)
    y = xf * lax.rsqrt(jnp.mean(xf * xf, axis=-1, keepdims=True) + RMS_EPS)
    return (y * g.astype(jnp.float32)).astype(x.dtype)


def rope(x, pos):
    half = x.shape[-1] // 2
    inv = ROPE_BASE ** (-jnp.arange(half, dtype=jnp.float32) / half)
    ang = pos.astype(jnp.float32)[:, None] * inv[None, :]
    cos, sin = jnp.cos(ang)[:, None, :], jnp.sin(ang)[:, None, :]
    xf = x.astype(jnp.float32)
    x1, x2 = xf[..., :half], xf[..., half:]
    return jnp.concatenate([x1 * cos - x2 * sin, x1 * sin + x2 * cos], axis=-1).astype(x.dtype)


def split_cols(z):
    out, off = [], 0
    for n in IN_SIZES:
        out.append(z[..., off:off + n])
        off += n
    return out


def ret_log_gamma():
    return jnp.log1p(-jnp.exp2(-5.0 - jnp.arange(RET_HEADS, dtype=jnp.float32)))


def retention_chunk(q, k, v, s):
    L = q.shape[2]
    lg = ret_log_gamma()
    i = jnp.arange(L, dtype=jnp.float32)
    diff = i[:, None] - i[None, :]
    decay = jnp.where(diff[None] >= 0, jnp.exp(jnp.maximum(diff, 0.0)[None] * lg[:, None, None]), 0.0)
    q_dec = jnp.exp((i[None, :] + 1.0) * lg[:, None])
    k_dec = jnp.exp((L - 1.0 - i[None, :]) * lg[:, None])
    inner = jnp.einsum('bhid,bhjd->bhij', q, k) * decay
    o = (jnp.einsum('bhij,bhje->bhie', inner, v)
         + jnp.einsum('bhid,bhde->bhie', q * q_dec[None, :, :, None], s))
    s_new = (s * jnp.exp(L * lg)[None, :, None, None]
             + jnp.einsum('bhjd,bhje->bhde', k * k_dec[None, :, :, None], v))
    return o, s_new


def retention_prompt(q, k, v):
    b, S = q.shape[:2]
    nc = S // RET_CHUNK

    def to_chunks(t):
        return t.reshape(b, nc, RET_CHUNK, RET_HEADS, t.shape[-1]).transpose(1, 0, 3, 2, 4)

    def step(s, xs):
        qc, kc, vc = xs
        o, s = retention_chunk(qc, kc, vc, s)
        return s, o

    s0 = jnp.zeros((b, RET_HEADS, RET_DK, RET_DV), jnp.float32)
    s_fin, o = lax.scan(step, s0, (to_chunks(q), to_chunks(k), to_chunks(v)))
    o = o.transpose(1, 0, 3, 2, 4).reshape(b, S, RET_HEADS, RET_DV)
    return o, s_fin


def retention_sample(q, k, v, s0):
    tr = lambda t: t.transpose(0, 2, 1, 3)
    o, s = retention_chunk(tr(q), tr(k), tr(v), s0)
    return tr(o), s


def mla_prompt_attend(q_lat, q_pe, ckv, kpe):
    b, S = q_lat.shape[:2]
    nb = S // MLA_Q_BLOCK
    scale = (MLA_D_NOPE + MLA_D_ROPE) ** -0.5
    kpos = jnp.arange(S)

    def blk(xs):
        ql, qp, start = xs
        s = jnp.einsum('bqhl,bkl->bhqk', ql, ckv) + jnp.einsum('bqhr,bkr->bhqk', qp, kpe)
        qpos = start + jnp.arange(MLA_Q_BLOCK)
        mask = kpos[None, :] <= qpos[:, None]
        s = jnp.where(mask, s.astype(jnp.float32) * scale, -jnp.inf)
        p = jax.nn.softmax(s, axis=-1).astype(ckv.dtype)
        return jnp.einsum('bhqk,bkl->bqhl', p, ckv)

    def to_blocks(t):
        return t.reshape(b, nb, MLA_Q_BLOCK, *t.shape[2:]).swapaxes(0, 1)

    o = lax.map(blk, (to_blocks(q_lat), to_blocks(q_pe), jnp.arange(nb) * MLA_Q_BLOCK))
    return o.swapaxes(0, 1).reshape(b, S, MLA_HEADS, MLA_KV_LORA)


def mla_sample_attend(q_lat, q_pe, ckv_new, kpe_new, ckv_past, kpe_past):
    T = q_lat.shape[1]
    P = ckv_past.shape[1]
    scale = (MLA_D_NOPE + MLA_D_ROPE) ** -0.5
    s_past = (jnp.einsum('bqhl,bkl->bhqk', q_lat, ckv_past)
              + jnp.einsum('bqhr,bkr->bhqk', q_pe, kpe_past)).astype(jnp.float32) * scale
    s_new = (jnp.einsum('bqhl,bkl->bhqk', q_lat, ckv_new)
             + jnp.einsum('bqhr,bkr->bhqk', q_pe, kpe_new)).astype(jnp.float32) * scale
    causal = jnp.arange(T)[None, :] <= jnp.arange(T)[:, None]
    s_new = jnp.where(causal, s_new, -jnp.inf)
    p = jax.nn.softmax(jnp.concatenate([s_past, s_new], axis=-1), axis=-1)
    p_past = p[..., :P].astype(ckv_past.dtype)
    p_new = p[..., P:].astype(ckv_new.dtype)
    return (jnp.einsum('bhqk,bkl->bqhl', p_past, ckv_past)
            + jnp.einsum('bhqk,bkl->bqhl', p_new, ckv_new))


def mem_kv(mem, p):
    b, m, _ = mem.shape
    mn = rms_norm(mem, p['norm_mem'])
    k = (mn @ p['w_mem_k']).reshape(b, m, X_HEADS, X_HD)
    v = (mn @ p['w_mem_v']).reshape(b, m, X_HEADS, X_HD)
    return k, v


def mem_attend(q, mk, mv):
    s = jnp.einsum('bshd,bmhd->bhsm', q, mk).astype(jnp.float32) * (X_HD ** -0.5)
    p = jax.nn.softmax(s, axis=-1).astype(mv.dtype)
    return jnp.einsum('bhsm,bmhd->bshd', p, mv)


def decoder_layer(x, pos, mem_k, mem_v, p, ret_fn, mla_fn):
    b, s, _ = x.shape
    u = rms_norm(x, p['norm_mix_pre'])
    rq, rk, rv, rg, cq, ckv, kpe, xq, gates = split_cols(u @ p['w_in'])

    rq = rope(rq.reshape(b, s, RET_HEADS, RET_DK), pos).astype(jnp.float32)
    rk = rope(rk.reshape(b, s, RET_HEADS, RET_DK), pos).astype(jnp.float32) * (RET_DK ** -0.5)
    rv = rv.reshape(b, s, RET_HEADS, RET_DV).astype(jnp.float32)
    o_ret, ret_state = ret_fn(rq, rk, rv)
    o_ret = o_ret * lax.rsqrt(jnp.mean(o_ret * o_ret, axis=-1, keepdims=True) + RMS_EPS)
    o_ret = (jax.nn.silu(rg.astype(jnp.float32)) * o_ret.reshape(b, s, RET_HEADS * RET_DV)).astype(x.dtype)
    a_ret = o_ret @ p['w_ret_o']

    cq = rms_norm(cq, p['norm_q_lat'])
    q = (cq @ p['w_uq']).reshape(b, s, MLA_HEADS, MLA_D_NOPE + MLA_D_ROPE)
    q_nope, q_pe = q[..., :MLA_D_NOPE], rope(q[..., MLA_D_NOPE:], pos)
    q_lat = jnp.einsum('bshd,hld->bshl', q_nope, p['w_uk'])
    ckv = rms_norm(ckv, p['norm_kv_lat'])
    kpe = rope(kpe[:, :, None, :], pos)[:, :, 0, :]
    o_lat = mla_fn(q_lat, q_pe, ckv, kpe)
    o_mla = jnp.einsum('bshl,hld->bshd', o_lat, p['w_uv']).reshape(b, s, MLA_HEADS * MLA_D_V)
    a_mla = o_mla @ p['w_mla_o']

    o_x = mem_attend(xq.reshape(b, s, X_HEADS, X_HD), mem_k, mem_v).reshape(b, s, X_HEADS * X_HD)
    a_x = o_x @ p['w_x_o']

    g = jax.nn.sigmoid(gates.astype(jnp.float32)).reshape(b, s, N_BRANCH, D_MODEL)
    mixed = (g[:, :, 0] * a_ret + g[:, :, 1] * a_mla + g[:, :, 2] * a_x).astype(x.dtype)
    h = x + rms_norm(mixed @ p['w_out'], p['norm_mix_post'])
    f = rms_norm(h, p['norm_ffn_pre'])
    f = (jax.nn.silu(f @ p['w_ffn_gate']) * (f @ p['w_ffn_up'])) @ p['w_ffn_down']
    y = h + rms_norm(f, p['norm_ffn_post'])
    return y, ret_state, ckv, kpe


def setup_inputs(seed: int = 0) -> dict:
    key = jax.random.key(seed)
    ks = jax.random.split(key, 32)
    n_pages = PAST_LEN // PAGE_SIZE
    n_used = DEC_BATCH * n_pages
    n_pool = n_used + max(1, n_used // 4)

    def nrm(i, shape, scale=1.0):
        return jax.random.normal(ks[i], shape, jnp.float32) * scale

    def gain(i, n):
        return 1.0 + nrm(i, (DEPTH, n), 0.05)

    page_table = jax.random.permutation(ks[5], n_pool)[:n_used].reshape(DEC_BATCH, n_pages).astype(jnp.int32)
    return {
        'x_prompt': nrm(0, (BATCH, SEQ, D_MODEL)),
        'x_sample': nrm(1, (DEC_BATCH, DEC_SEQ, D_MODEL)),
        'mem_prompt': nrm(2, (BATCH, N_MEM, D_MODEL)),
        'cache_ckv': nrm(3, (DEPTH, n_pool, PAGE_SIZE, MLA_KV_LORA)),
        'cache_kpe': nrm(4, (DEPTH, n_pool, PAGE_SIZE, MLA_D_ROPE)),
        'page_table': page_table,
        'state_ret': nrm(6, (DEPTH, DEC_BATCH, RET_HEADS, RET_DK, RET_DV), 0.5),
        'cache_mem_k': nrm(7, (DEPTH, DEC_BATCH, N_MEM, X_HEADS, X_HD)),
        'cache_mem_v': nrm(8, (DEPTH, DEC_BATCH, N_MEM, X_HEADS, X_HD)),
        'norm_mix_pre': gain(9, D_MODEL),
        'norm_mix_post': gain(10, D_MODEL),
        'norm_ffn_pre': gain(11, D_MODEL),
        'norm_ffn_post': gain(12, D_MODEL),
        'norm_mem': gain(13, D_MODEL),
        'norm_q_lat': gain(14, MLA_Q_LORA),
        'norm_kv_lat': gain(15, MLA_KV_LORA),
        'w_in': nrm(16, (DEPTH, D_MODEL, D_IN), D_MODEL ** -0.5),
        'w_uq': nrm(17, (DEPTH, MLA_Q_LORA, MLA_HEADS * (MLA_D_NOPE + MLA_D_ROPE)), MLA_Q_LORA ** -0.5),
        'w_uk': nrm(18, (DEPTH, MLA_HEADS, MLA_KV_LORA, MLA_D_NOPE), MLA_KV_LORA ** -0.5),
        'w_uv': nrm(19, (DEPTH, MLA_HEADS, MLA_KV_LORA, MLA_D_V), MLA_KV_LORA ** -0.5),
        'w_mem_k': nrm(20, (DEPTH, D_MODEL, X_HEADS * X_HD), D_MODEL ** -0.5),
        'w_mem_v': nrm(21, (DEPTH, D_MODEL, X_HEADS * X_HD), D_MODEL ** -0.5),
        'w_ret_o': nrm(22, (DEPTH, RET_HEADS * RET_DV, D_MODEL), (RET_HEADS * RET_DV) ** -0.5),
        'w_mla_o': nrm(23, (DEPTH, MLA_HEADS * MLA_D_V, D_MODEL), (MLA_HEADS * MLA_D_V) ** -0.5),
        'w_x_o': nrm(24, (DEPTH, X_HEADS * X_HD, D_MODEL), (X_HEADS * X_HD) ** -0.5),
        'w_out': nrm(25, (DEPTH, D_MODEL, D_MODEL), D_MODEL ** -0.5),
        'w_ffn_gate': nrm(26, (DEPTH, D_MODEL, D_FF), D_MODEL ** -0.5),
        'w_ffn_up': nrm(27, (DEPTH, D_MODEL, D_FF), D_MODEL ** -0.5),
        'w_ffn_down': nrm(28, (DEPTH, D_FF, D_MODEL), D_FF ** -0.5),
    }


def reference(x_prompt, x_sample, mem_prompt, cache_ckv, cache_kpe, page_table, state_ret,
              cache_mem_k, cache_mem_v, norm_mix_pre, norm_mix_post, norm_ffn_pre, norm_ffn_post,
              norm_mem, norm_q_lat, norm_kv_lat, w_in, w_uq, w_uk, w_uv, w_mem_k, w_mem_v,
              w_ret_o, w_mla_o, w_x_o, w_out, w_ffn_gate, w_ffn_up, w_ffn_down):
    db = x_sample.shape[0]
    past_len = page_table.shape[1] * cache_ckv.shape[2]
    pos_p = jnp.arange(x_prompt.shape[1])
    pos_s = past_len + jnp.arange(x_sample.shape[1])

    y_prompt, y_sample = x_prompt, x_sample
    ckv_p_l, kpe_p_l, ckv_s_l, kpe_s_l = [], [], [], []
    ret_p_l, ret_s_l, mk_p_l, mv_p_l = [], [], [], []
    for l in range(DEPTH):
        p = dict(norm_mix_pre=norm_mix_pre[l], norm_mix_post=norm_mix_post[l],
                 norm_ffn_pre=norm_ffn_pre[l], norm_ffn_post=norm_ffn_post[l],
                 norm_mem=norm_mem[l], norm_q_lat=norm_q_lat[l], norm_kv_lat=norm_kv_lat[l],
                 w_in=w_in[l], w_uq=w_uq[l], w_uk=w_uk[l], w_uv=w_uv[l],
                 w_mem_k=w_mem_k[l], w_mem_v=w_mem_v[l], w_ret_o=w_ret_o[l],
                 w_mla_o=w_mla_o[l], w_x_o=w_x_o[l], w_out=w_out[l],
                 w_ffn_gate=w_ffn_gate[l], w_ffn_up=w_ffn_up[l], w_ffn_down=w_ffn_down[l])

        mk_p, mv_p = mem_kv(mem_prompt, p)
        y_prompt, ret_p, ckv_p, kpe_p = decoder_layer(
            y_prompt, pos_p, mk_p, mv_p, p, retention_prompt, mla_prompt_attend)

        ckv_past = cache_ckv[l][page_table].reshape(db, past_len, MLA_KV_LORA)
        kpe_past = cache_kpe[l][page_table].reshape(db, past_len, MLA_D_ROPE)
        ret_fn = functools.partial(retention_sample, s0=state_ret[l].astype(jnp.float32))
        mla_fn = functools.partial(mla_sample_attend, ckv_past=ckv_past, kpe_past=kpe_past)
        y_sample, ret_s, ckv_s, kpe_s = decoder_layer(
            y_sample, pos_s, cache_mem_k[l], cache_mem_v[l], p, ret_fn, mla_fn)

        ckv_p_l.append(ckv_p)
        kpe_p_l.append(kpe_p)
        ckv_s_l.append(ckv_s)
        kpe_s_l.append(kpe_s)
        ret_p_l.append(ret_p.astype(x_prompt.dtype))
        ret_s_l.append(ret_s.astype(state_ret.dtype))
        mk_p_l.append(mk_p)
        mv_p_l.append(mv_p)

    new_ckv_prompt = jnp.stack(ckv_p_l)
    new_kpe_prompt = jnp.stack(kpe_p_l)
    new_ckv_sample = jnp.stack(ckv_s_l)
    new_kpe_sample = jnp.stack(kpe_s_l)
    new_ret_prompt = jnp.stack(ret_p_l)
    new_ret_sample = jnp.stack(ret_s_l)
    new_mem_k_prompt = jnp.stack(mk_p_l)
    new_mem_v_prompt = jnp.stack(mv_p_l)
    return (y_prompt, y_sample, new_ckv_prompt, new_kpe_prompt, new_ckv_sample, new_kpe_sample,
            new_ret_prompt, new_ret_sample, new_mem_k_prompt, new_mem_v_prompt)
```

```python
import functools
import math

import numpy as np
import jax
import jax.numpy as jnp
from jax import lax
from jax.experimental import pallas as pl
from jax.experimental.pallas import tpu as pltpu

F32 = jnp.float32
BF16 = jnp.bfloat16

ROPE_BASE = 10000.0
RMS_EPS = 1e-6
LANES = 128
VMEM_LIMIT = 56 * 1024 * 1024
NEG_BIG = -1e30

ROW_TILE = 256
RET_CHUNK = 128
ATTN_TILE = 512
X_TILE = 512
SAMPLE_TOK_PAD = 16
RET_SAMPLE_GROUP = 8
X_SAMPLE_GROUP = 8
PAGES_PER_STEP = 8


def _resident(shape):
    nd = len(shape)
    return pl.BlockSpec(shape, lambda *_: (0,) * nd, pipeline_mode=pl.Buffered(1))


def _params(n_axes):
    return pltpu.CompilerParams(dimension_semantics=("arbitrary",) * n_axes, vmem_limit_bytes=VMEM_LIMIT)


def _rms(x, g=None):
    y = x * lax.rsqrt(jnp.mean(x * x, axis=-1, keepdims=True) + RMS_EPS)
    return y if g is None else y * g


def _sigmoid(x):
    return 1.0 / (1.0 + jnp.exp(-x))


def _dot(a, b):
    return jnp.dot(a, b, preferred_element_type=F32)


def _dot_nt(a, b):
    return lax.dot_general(a, b, (((1,), (1,)), ((), ())), preferred_element_type=F32)


def _dot_tn(a, b):
    return lax.dot_general(a, b, (((0,), (0,)), ((), ())), preferred_element_type=F32)


def _rope_half_vreg(z, cos_t, sin_lo, sin_hi, quarter):
    return (z * cos_t + pltpu.roll(z, LANES - quarter, 1) * sin_lo + pltpu.roll(z, quarter, 1) * sin_hi)


def _in_proj_kernel(x_ref, g_ref, wrq, wrk, wrv, wrg, wcq, wckv, wkpe, wxq, gq_ref, gkv_ref, wuqn, wuqp,
                    wa, wb, cosr, sinr, cosp, sinlo, sinhi,
                    rq_o, rk_o, rv_o, rg_o, qcat_o, a_o, b_o, xq_o, ckv_o, kpe_o,
                    *, ret_heads, ret_dk, heads, d_nope, d_rope, kv_lora, absorbed):
    u = _rms(x_ref[...], g_ref[...]).astype(BF16)

    cr, sr = cosr[...], sinr[...]
    zq = _dot(u, wrq[...])
    zk = _dot(u, wrk[...])
    k_scale = ret_dk ** -0.5
    for h in range(ret_heads):
        sl = slice(h * ret_dk, (h + 1) * ret_dk)
        q_h = zq[:, sl]
        k_h = zk[:, sl]
        rq_o[:, sl] = (q_h * cr + pltpu.roll(q_h, ret_dk // 2, 1) * sr).astype(BF16)
        rk_o[:, sl] = ((k_h * cr + pltpu.roll(k_h, ret_dk // 2, 1) * sr) * k_scale).astype(BF16)
    rv_o[...] = _dot(u, wrv[...]).astype(BF16)
    rg_o[...] = _dot(u, wrg[...]).astype(BF16)

    cp, slo, shi = cosp[...], sinlo[...], sinhi[...]
    cqn = _rms(_dot(u, wcq[...]), gq_ref[...]).astype(BF16)
    qn = _dot(cqn, wuqn[...])
    qp = _dot(cqn, wuqp[...])
    group = d_nope + LANES
    for h in range(heads):
        qn_h = qn[:, h * d_nope:(h + 1) * d_nope]
        qp_h = _rope_half_vreg(qp[:, h * LANES:(h + 1) * LANES], cp, slo, shi, d_rope // 2)
        qcat_o[:, h * group:h * group + d_nope] = qn_h.astype(BF16)
        qcat_o[:, h * group + d_nope:(h + 1) * group] = qp_h.astype(BF16)
        if absorbed:
            a_o[:, h * kv_lora:(h + 1) * kv_lora] = _dot(qn_h.astype(BF16), wa[h]).astype(BF16)

    ckvn = _rms(_dot(u, wckv[...]), gkv_ref[...])
    ckv_o[...] = ckvn
    kpe = _rope_half_vreg(_dot(u, wkpe[...]), cp, slo, shi, d_rope // 2)
    kpe_o[...] = kpe[:, :d_rope]
    if not absorbed:
        ckvb = ckvn.astype(BF16)
        kn = _dot(ckvb, wa[...])
        kpeb = kpe.astype(BF16)
        for h in range(heads):
            a_o[:, h * group:h * group + d_nope] = kn[:, h * d_nope:(h + 1) * d_nope].astype(BF16)
            a_o[:, h * group + d_nope:(h + 1) * group] = kpeb
        b_o[...] = _dot(ckvb, wb[...]).astype(BF16)
    else:
        b_o[...] = jnp.zeros(b_o.shape, b_o.dtype)

    xq_o[...] = _dot(u, wxq[...]).astype(BF16)


def _in_proj(x, w, tabs, dims, *, absorbed, tm):
    n, d_model = x.shape
    heads, d_nope, d_rope, kv_lora = dims["heads"], dims["d_nope"], dims["d_rope"], dims["kv_lora"]
    ret_heads, ret_dk, ret_dv = dims["ret_heads"], dims["ret_dk"], dims["ret_dv"]
    group = d_nope + LANES
    tab_rows = tabs["cosr"].shape[0]
    tab_tiles = tab_rows // tm
    row = lambda c: pl.BlockSpec((tm, c), lambda i: (i, 0))
    tab = lambda: pl.BlockSpec((tm, LANES), lambda i: (i % tab_tiles, 0))
    wa = w["w_uk_t"] if absorbed else w["w_kn"]
    wb = w["w_vn"]
    a_cols = heads * kv_lora if absorbed else heads * group
    b_cols = LANES if absorbed else wb.shape[1]
    weights = [w["g_pre"], w["w_rq"], w["w_rk"], w["w_rv"], w["w_rg"], w["w_cq"], w["w_ckv"], w["w_kpe"],
               w["w_xq"], w["g_q"], w["g_kv"], w["w_uq_n"], w["w_uq_p"], wa, wb]
    out_cols = [(ret_heads * ret_dk, BF16), (ret_heads * ret_dk, BF16), (ret_heads * ret_dv, BF16),
                (ret_heads * ret_dv, BF16), (heads * group, BF16), (a_cols, BF16), (b_cols, BF16),
                (w["w_xq"].shape[1], BF16), (kv_lora, F32), (d_rope, F32)]
    kern = functools.partial(_in_proj_kernel, ret_heads=ret_heads, ret_dk=ret_dk, heads=heads, d_nope=d_nope,
                             d_rope=d_rope, kv_lora=kv_lora, absorbed=absorbed)
    return pl.pallas_call(
        kern,
        grid=(n // tm,),
        in_specs=[row(d_model)] + [_resident(a.shape) for a in weights] + [tab() for _ in range(5)],
        out_specs=[row(c) for c, _ in out_cols],
        out_shape=[jax.ShapeDtypeStruct((n, c), dt) for c, dt in out_cols],
        compiler_params=_params(1),
        name="in_proj_sample" if absorbed else "in_proj_prompt",
    )(x, *weights, tabs["cosr"], tabs["sinr"], tabs["cosp"], tabs["sinlo"], tabs["sinhi"])


def _ret_head(q, k, v, g, s, dec, qd, kd, g_l):
    inner = _dot_nt(q, k) * dec
    o = _dot(inner.astype(BF16), v) + _dot((q.astype(F32) * qd).astype(BF16), s.astype(BF16))
    s_new = s * g_l + _dot_tn((k.astype(F32) * kd).astype(BF16), v)
    gf = g.astype(F32)
    o = (gf * _sigmoid(gf)) * _rms(o)
    return o.astype(BF16), s_new


def _ret_prompt_kernel(rq, rk, rv, rg, dec, qd, kd, o_ref, s_out, s_scr, *, heads, dk, dv, g_l):
    c = pl.program_id(1)

    @pl.when(c == 0)
    def _():
        s_scr[...] = jnp.zeros(s_scr.shape, F32)

    for h in range(heads):
        o, s_new = _ret_head(rq[:, h * dk:(h + 1) * dk], rk[:, h * dk:(h + 1) * dk],
                             rv[:, h * dv:(h + 1) * dv], rg[:, h * dv:(h + 1) * dv],
                             s_scr[h], dec[h], qd[h], kd[h], g_l[h])
        o_ref[:, h * dv:(h + 1) * dv] = o
        s_scr[h] = s_new

    @pl.when(c == pl.num_programs(1) - 1)
    def _():
        s_out[0] = s_scr[...]


def _ret_consts(heads, length, dk, chunk_rows=None):
    rows = length if chunk_rows is None else chunk_rows
    lg = np.log1p(-np.exp2(-5.0 - np.arange(heads, dtype=np.float64)))
    i = np.arange(rows, dtype=np.float64)
    diff = i[:, None] - i[None, :]
    valid = (diff >= 0) & (i[:, None] < length) & (i[None, :] < length)
    dec = np.where(valid[None], np.exp(np.maximum(diff, 0.0)[None] * lg[:, None, None]), 0.0)
    qd = np.exp((i[None, :] + 1.0) * lg[:, None])
    kd = np.where(i[None, :] < length, np.exp((length - 1.0 - i[None, :]) * lg[:, None]), 0.0)
    qd = np.broadcast_to(qd[:, :, None], (heads, rows, dk))
    kd = np.broadcast_to(kd[:, :, None], (heads, rows, dk))
    g_l = tuple(float(v) for v in np.exp(length * lg))
    return (jnp.asarray(dec, F32), jnp.asarray(qd, F32), jnp.asarray(kd, F32), g_l)


def _ret_prompt(rq, rk, rv, rg, batch, seq, dims):
    heads, dk, dv = dims["ret_heads"], dims["ret_dk"], dims["ret_dv"]
    L = RET_CHUNK
    nc = seq // L
    dec, qd, kd, g_l = _ret_consts(heads, L, dk)
    row = lambda c: pl.BlockSpec((L, c), lambda b, i: (b * nc + i, 0))
    kern = functools.partial(_ret_prompt_kernel, heads=heads, dk=dk, dv=dv, g_l=g_l)
    return pl.pallas_call(
        kern,
        grid=(batch, nc),
        in_specs=[row(heads * dk), row(heads * dk), row(heads * dv), row(heads * dv),
                  _resident(dec.shape), _resident(qd.shape), _resident(kd.shape)],
        out_specs=[row(heads * dv), pl.BlockSpec((1, heads, dk, dv), lambda b, i: (b, 0, 0, 0))],
        out_shape=[jax.ShapeDtypeStruct((batch * seq, heads * dv), BF16),
                   jax.ShapeDtypeStruct((batch, heads, dk, dv), F32)],
        scratch_shapes=[pltpu.VMEM((heads, dk, dv), F32)],
        compiler_params=_params(2),
        name="retention_prompt",
    )(rq, rk, rv, rg, dec, qd, kd)


def _ret_sample_kernel(rq, rk, rv, rg, s0, dec, qd, kd, o_ref, s_out, *, group, heads, dk, dv, g_l):
    def body(i, carry):
        for h in range(heads):
            o, s_new = _ret_head(rq[i, :, h * dk:(h + 1) * dk], rk[i, :, h * dk:(h + 1) * dk],
                                 rv[i, :, h * dv:(h + 1) * dv], rg[i, :, h * dv:(h + 1) * dv],
                                 s0[i, h], dec[h], qd[h], kd[h], g_l[h])
            o_ref[i, :, h * dv:(h + 1) * dv] = o
            s_out[i, h] = s_new
        return carry

    lax.fori_loop(0, group, body, 0)


def _ret_sample(rq, rk, rv, rg, state, tokens, dims):
    heads, dk, dv = dims["ret_heads"], dims["ret_dk"], dims["ret_dv"]
    b, tp, _ = rq.shape
    G = RET_SAMPLE_GROUP
    dec, qd, kd, g_l = _ret_consts(heads, tokens, dk, chunk_rows=tp)
    blk = lambda c: pl.BlockSpec((G, tp, c), lambda i: (i, 0, 0))
    st = pl.BlockSpec((G, heads, dk, dv), lambda i: (i, 0, 0, 0))
    kern = functools.partial(_ret_sample_kernel, group=G, heads=heads, dk=dk, dv=dv, g_l=g_l)
    return pl.pallas_call(
        kern,
        grid=(b // G,),
        in_specs=[blk(heads * dk), blk(heads * dk), blk(heads * dv), blk(heads * dv), st,
                  _resident(dec.shape), _resident(qd.shape), _resident(kd.shape)],
        out_specs=[blk(heads * dv), st],
        out_shape=[jax.ShapeDtypeStruct((b, tp, heads * dv), BF16),
                   jax.ShapeDtypeStruct((b, heads, dk, dv), F32)],
        compiler_params=_params(1),
        name="retention_sample",
    )(rq, rk, rv, rg, state, dec, qd, kd)


def _mla_prompt_kernel(q_ref, k_ref, v_ref, o_ref, *, tile, scale):
    qi = pl.program_id(2)
    q = q_ref[...]

    def step(j, carry, masked):
        m, l, acc = carry
        start = pl.multiple_of(j * tile, tile)
        s = _dot_nt(q, k_ref[pl.ds(start, tile), :]) * scale
        if masked:
            r = lax.broadcasted_iota(jnp.int32, s.shape, 0)
            c = lax.broadcasted_iota(jnp.int32, s.shape, 1)
            s = jnp.where(c <= r, s, NEG_BIG)
        m_new = jnp.maximum(m, jnp.max(s, axis=-1, keepdims=True))
        a = jnp.exp(m - m_new)
        p = jnp.exp(s - m_new)
        l = a * l + jnp.sum(p, axis=-1, keepdims=True)
        acc = a * acc + _dot(p.astype(BF16), v_ref[pl.ds(start, tile), :])
        return m_new, l, acc

    init = (jnp.full((tile, 1), NEG_BIG, F32), jnp.zeros((tile, 1), F32), jnp.zeros((tile, v_ref.shape[1]), F32))
    carry = lax.fori_loop(0, qi, functools.partial(step, masked=False), init)
    m, l, acc = step(qi, carry, True)
    o_ref[...] = (acc / l).astype(BF16)


def _mla_prompt(qcat, kcat, v, batch, seq, dims):
    heads, d_nope, d_rope, d_v = dims["heads"], dims["d_nope"], dims["d_rope"], dims["d_v"]
    group = d_nope + LANES
    t = ATTN_TILE
    nq = seq // t
    kern = functools.partial(_mla_prompt_kernel, tile=t, scale=(d_nope + d_rope) ** -0.5)
    return pl.pallas_call(
        kern,
        grid=(batch, heads, nq),
        in_specs=[pl.BlockSpec((t, group), lambda b, h, i: (b * nq + i, h)),
                  pl.BlockSpec((seq, group), lambda b, h, i: (b, h)),
                  pl.BlockSpec((seq, d_v), lambda b, h, i: (b, h))],
        out_specs=pl.BlockSpec((t, d_v), lambda b, h, i: (b * nq + i, h)),
        out_shape=jax.ShapeDtypeStruct((batch * seq, heads * d_v), BF16),
        compiler_params=_params(3),
        name="mla_prompt",
    )(qcat, kcat, v)


def _mla_sample_kernel(pt_ref, ql_ref, qp_ref, cn_ref, kn_ref, *refs, pages, tokens, heads, scale):
    del pt_ref
    ckv_refs = refs[:pages]
    kpe_refs = refs[pages:2 * pages]
    o_ref, m_scr, l_scr, acc_scr = refs[2 * pages:]
    j = pl.program_id(1)

    @pl.when(j == 0)
    def _():
        m_scr[...] = jnp.full(m_scr.shape, NEG_BIG, F32)
        l_scr[...] = jnp.zeros(l_scr.shape, F32)
        acc_scr[...] = jnp.zeros(acc_scr.shape, F32)

    ql = ql_ref[0]
    qp = qp_ref[0]

    def online(s, vals):
        m = m_scr[...]
        m_new = jnp.maximum(m, jnp.max(s, axis=-1, keepdims=True))
        a = jnp.exp(m - m_new)
        p = jnp.exp(s - m_new)
        l_scr[...] = a * l_scr[...] + jnp.sum(p, axis=-1, keepdims=True)
        acc_scr[...] = a * acc_scr[...] + _dot(p.astype(BF16), vals)
        m_scr[...] = m_new

    ckv = jnp.concatenate([r[0].astype(BF16) for r in ckv_refs], axis=0)
    kpe = jnp.concatenate([r[0].astype(BF16) for r in kpe_refs], axis=0)
    online((_dot_nt(ql, ckv) + _dot_nt(qp, kpe)) * scale, ckv)

    @pl.when(j == pl.num_programs(1) - 1)
    def _():
        cn = cn_ref[0].astype(BF16)
        kn = kn_ref[0].astype(BF16)
        s = (_dot_nt(ql, cn) + _dot_nt(qp, kn)) * scale
        t = lax.broadcasted_iota(jnp.int32, s.shape, 0) // heads
        c = lax.broadcasted_iota(jnp.int32, s.shape, 1)
        online(jnp.where((c <= t) & (c < tokens), s, NEG_BIG), cn)
        o_ref[0] = acc_scr[...] / l_scr[...]


def _mla_sample(ql, qp, ckv_new, kpe_new, cache_ckv, cache_kpe, layer, page_table, tokens, dims):
    heads, d_nope, d_rope, kv_lora = dims["heads"], dims["d_nope"], dims["d_rope"], dims["kv_lora"]
    b, rows, _ = ql.shape
    n_pages = page_table.shape[1]
    page = cache_ckv.shape[2]
    P = PAGES_PER_STEP
    tp = ckv_new.shape[1]
    per_b = lambda r, c: pl.BlockSpec((1, r, c), lambda i, j, pt: (i, 0, 0))

    def paged(c, p):
        return pl.BlockSpec((None, 1, page, c), lambda i, j, pt: (layer, pt[i, j * P + p], 0, 0))

    kern = functools.partial(_mla_sample_kernel, pages=P, tokens=tokens, heads=heads,
                             scale=(d_nope + d_rope) ** -0.5)
    grid_spec = pltpu.PrefetchScalarGridSpec(
        num_scalar_prefetch=1,
        grid=(b, n_pages // P),
        in_specs=[per_b(rows, kv_lora), per_b(rows, d_rope), per_b(tp, kv_lora), per_b(tp, d_rope)]
                 + [paged(kv_lora, p) for p in range(P)] + [paged(d_rope, p) for p in range(P)],
        out_specs=per_b(rows, kv_lora),
        scratch_shapes=[pltpu.VMEM((rows, 1), F32), pltpu.VMEM((rows, 1), F32), pltpu.VMEM((rows, kv_lora), F32)],
    )
    return pl.pallas_call(
        kern,
        grid_spec=grid_spec,
        out_shape=jax.ShapeDtypeStruct((b, rows, kv_lora), F32),
        compiler_params=_params(2),
        name="mla_sample",
    )(page_table, ql, qp, ckv_new, kpe_new, *([cache_ckv] * P), *([cache_kpe] * P))


def _mem_kv_kernel(m_ref, g_ref, wk, wv, k_o, v_o):
    mn = _rms(m_ref[0], g_ref[...]).astype(BF16)
    k_o[0] = _dot(mn, wk[...])
    v_o[0] = _dot(mn, wv[...])


def _mem_kv(mem, g, wk, wv):
    b, m, d = mem.shape
    c = wk.shape[1]
    return pl.pallas_call(
        _mem_kv_kernel,
        grid=(b,),
        in_specs=[pl.BlockSpec((1, m, d), lambda i: (i, 0, 0)), _resident(g.shape), _resident(wk.shape),
                  _resident(wv.shape)],
        out_specs=[pl.BlockSpec((1, m, c), lambda i: (i, 0, 0))] * 2,
        out_shape=[jax.ShapeDtypeStruct((b, m, c), F32)] * 2,
        compiler_params=_params(1),
        name="mem_kv",
    )(mem, g, wk, wv)


def _x_attend_heads(q, mk, mv, heads, hd, rows_per_head=None):
    lane = lax.broadcasted_iota(jnp.int32, q.shape, 1)
    out = jnp.zeros(q.shape, F32)
    for h in range(heads):
        sel = (lane >= h * hd) & (lane < (h + 1) * hd)
        s = _dot_nt(jnp.where(sel, q, jnp.zeros_like(q)), mk) * (hd ** -0.5)
        p = jnp.exp(s - jnp.max(s, axis=-1, keepdims=True))
        p = p / jnp.sum(p, axis=-1, keepdims=True)
        out = out + jnp.where(sel, _dot(p.astype(BF16), mv), 0.0)
    return out


def _x_prompt_kernel(q_ref, k_ref, v_ref, o_ref, *, heads, hd):
    o_ref[...] = _x_attend_heads(q_ref[...], k_ref[0].astype(BF16), v_ref[0].astype(BF16), heads, hd).astype(BF16)


def _x_prompt(xq, mk, mv, batch, seq, heads):
    m, c = mk.shape[1:]
    t = X_TILE
    nt = seq // t
    kern = functools.partial(_x_prompt_kernel, heads=heads, hd=c // heads)
    return pl.pallas_call(
        kern,
        grid=(batch, nt),
        in_specs=[pl.BlockSpec((t, c), lambda b, i: (b * nt + i, 0)),
                  pl.BlockSpec((1, m, c), lambda b, i: (b, 0, 0)),
                  pl.BlockSpec((1, m, c), lambda b, i: (b, 0, 0))],
        out_specs=pl.BlockSpec((t, c), lambda b, i: (b * nt + i, 0)),
        out_shape=jax.ShapeDtypeStruct((batch * seq, c), BF16),
        compiler_params=_params(2),
        name="x_attend_prompt",
    )(xq, mk, mv)


def _x_sample_kernel(q_ref, k_ref, v_ref, o_ref, *, group, heads, hd):
    def body(i, carry):
        o_ref[i] = _x_attend_heads(q_ref[i], k_ref[i].astype(BF16), v_ref[i].astype(BF16), heads, hd).astype(BF16)
        return carry

    lax.fori_loop(0, group, body, 0)


def _x_sample(xq, mk, mv, heads):
    b, tp, c = xq.shape
    m = mk.shape[1]
    G = X_SAMPLE_GROUP
    kern = functools.partial(_x_sample_kernel, group=G, heads=heads, hd=c // heads)
    return pl.pallas_call(
        kern,
        grid=(b // G,),
        in_specs=[pl.BlockSpec((G, tp, c), lambda i: (i, 0, 0)),
                  pl.BlockSpec((G, m, c), lambda i: (i, 0, 0)),
                  pl.BlockSpec((G, m, c), lambda i: (i, 0, 0))],
        out_specs=pl.BlockSpec((G, tp, c), lambda i: (i, 0, 0)),
        out_shape=jax.ShapeDtypeStruct((b, tp, c), BF16),
        compiler_params=_params(1),
        name="x_attend_sample",
    )(xq, mk, mv)


def _merge_kernel(x_ref, gpre, wgate, oret, wret, omla, wmla, wuv, ox, wx, wout, gpost, h_o,
                  *, heads, kv_lora, d_v, absorbed):
    x = x_ref[...]
    d = x.shape[1]
    u = _rms(x, gpre[...]).astype(BF16)
    a_ret = _dot(oret[...], wret[...])
    if absorbed:
        a_mla = jnp.zeros(x.shape, F32)
        for h in range(heads):
            o_h = _dot(omla[:, h * kv_lora:(h + 1) * kv_lora].astype(BF16), wuv[h]).astype(BF16)
            a_mla = a_mla + _dot(o_h, wmla[h * d_v:(h + 1) * d_v, :])
    else:
        a_mla = _dot(omla[...], wmla[...])
    a_x = _dot(ox[...], wx[...])
    mixed = jnp.zeros(x.shape, F32)
    for i, a in enumerate((a_ret, a_mla, a_x)):
        mixed = mixed + _sigmoid(_dot(u, wgate[:, i * d:(i + 1) * d])) * a
    h_o[...] = x + _rms(_dot(mixed.astype(BF16), wout[...]), gpost[...])


def _merge(x, oret, omla, ox, w, dims, *, absorbed, tm):
    n, d = x.shape
    row = lambda a: pl.BlockSpec((tm, a.shape[1]), lambda i: (i, 0))
    res = lambda a: _resident(a.shape)
    kern = functools.partial(_merge_kernel, heads=dims["heads"], kv_lora=dims["kv_lora"], d_v=dims["d_v"],
                             absorbed=absorbed)
    args = [x, w["g_pre"], w["w_gates"], oret, w["w_ret_o"], omla, w["w_mla_o"], w["w_uv_h"], ox, w["w_x_o"],
            w["w_out"], w["g_post"]]
    specs = [row(x), res(w["g_pre"]), res(w["w_gates"]), row(oret), res(w["w_ret_o"]), row(omla),
             res(w["w_mla_o"]), res(w["w_uv_h"]), row(ox), res(w["w_x_o"]), res(w["w_out"]), res(w["g_post"])]
    return pl.pallas_call(
        kern,
        grid=(n // tm,),
        in_specs=specs,
        out_specs=pl.BlockSpec((tm, d), lambda i: (i, 0)),
        out_shape=jax.ShapeDtypeStruct((n, d), F32),
        compiler_params=_params(1),
        name="merge_sample" if absorbed else "merge_prompt",
    )(*args)


def _ffn_kernel(h_ref, gpre, wg, wu, wd, gpost, y_o):
    h = h_ref[...]
    f = _rms(h, gpre[...]).astype(BF16)
    gate = _dot(f, wg[...])
    act = (gate * _sigmoid(gate) * _dot(f, wu[...])).astype(BF16)
    y_o[...] = h + _rms(_dot(act, wd[...]), gpost[...])


def _ffn(h, w, *, tm):
    n, d = h.shape
    res = lambda a: _resident(a.shape)
    return pl.pallas_call(
        _ffn_kernel,
        grid=(n // tm,),
        in_specs=[pl.BlockSpec((tm, d), lambda i: (i, 0)), res(w["g_ffn_pre"]), res(w["w_ffn_gate"]),
                  res(w["w_ffn_up"]), res(w["w_ffn_down"]), res(w["g_ffn_post"])],
        out_specs=pl.BlockSpec((tm, d), lambda i: (i, 0)),
        out_shape=jax.ShapeDtypeStruct((n, d), F32),
        compiler_params=_params(1),
        name="ffn",
    )(h, w["g_ffn_pre"], w["w_ffn_gate"], w["w_ffn_up"], w["w_ffn_down"], w["g_ffn_post"])


def _rope_tables(pos, ret_dk, d_rope):
    posf = pos.astype(F32)[:, None]

    def angles(half):
        inv = ROPE_BASE ** (-jnp.arange(half, dtype=F32) / half)
        ang = posf * inv[None, :]
        return jnp.cos(ang), jnp.sin(ang)

    cr, sr = angles(ret_dk // 2)
    cp, sp = angles(d_rope // 2)
    z = jnp.zeros_like(cp)
    pad = jnp.zeros((pos.shape[0], LANES - d_rope), F32)
    return {
        "cosr": jnp.concatenate([cr, cr], axis=1),
        "sinr": jnp.concatenate([-sr, sr], axis=1),
        "cosp": jnp.concatenate([cp, cp, pad], axis=1),
        "sinlo": jnp.concatenate([-sp, z, pad], axis=1),
        "sinhi": jnp.concatenate([z, sp, pad], axis=1),
    }


def _layer_weights(l, dims, sizes, norm_mix_pre, norm_mix_post, norm_ffn_pre, norm_ffn_post, norm_mem, norm_q_lat,
                   norm_kv_lat, w_in, w_uq, w_uk, w_uv, w_mem_k, w_mem_v, w_ret_o, w_mla_o, w_x_o, w_out,
                   w_ffn_gate, w_ffn_up, w_ffn_down):
    heads, d_nope, d_rope, kv_lora, d_v = (dims[k] for k in ("heads", "d_nope", "d_rope", "kv_lora", "d_v"))
    bf = lambda a: a.astype(BF16)
    gain = lambda a: a[l].astype(F32)[None, :]
    offs = np.concatenate([[0], np.cumsum(sizes)])
    seg = [w_in[l][:, offs[i]:offs[i + 1]] for i in range(len(sizes))]
    q_lora = w_uq.shape[1]
    uq = w_uq[l].reshape(q_lora, heads, d_nope + d_rope)
    uq_p = jnp.pad(uq[:, :, d_nope:], ((0, 0), (0, 0), (0, LANES - d_rope)))
    return {
        "g_pre": gain(norm_mix_pre), "g_post": gain(norm_mix_post), "g_ffn_pre": gain(norm_ffn_pre),
        "g_ffn_post": gain(norm_ffn_post), "g_mem": gain(norm_mem), "g_q": gain(norm_q_lat),
        "g_kv": gain(norm_kv_lat),
        "w_rq": bf(seg[0]), "w_rk": bf(seg[1]), "w_rv": bf(seg[2]), "w_rg": bf(seg[3]), "w_cq": bf(seg[4]),
        "w_ckv": bf(seg[5]), "w_kpe": bf(jnp.pad(seg[6], ((0, 0), (0, LANES - d_rope)))), "w_xq": bf(seg[7]),
        "w_gates": bf(seg[8]),
        "w_uq_n": bf(uq[:, :, :d_nope].reshape(q_lora, heads * d_nope)),
        "w_uq_p": bf(uq_p.reshape(q_lora, heads * LANES)),
        "w_uk_t": bf(jnp.swapaxes(w_uk[l], 1, 2)),
        "w_kn": bf(jnp.swapaxes(w_uk[l], 0, 1).reshape(kv_lora, heads * d_nope)),
        "w_vn": bf(jnp.swapaxes(w_uv[l], 0, 1).reshape(kv_lora, heads * d_v)),
        "w_uv_h": bf(w_uv[l]),
        "w_mem_k": bf(w_mem_k[l]), "w_mem_v": bf(w_mem_v[l]),
        "w_ret_o": bf(w_ret_o[l]), "w_mla_o": bf(w_mla_o[l]), "w_x_o": bf(w_x_o[l]), "w_out": bf(w_out[l]),
        "w_ffn_gate": bf(w_ffn_gate[l]), "w_ffn_up": bf(w_ffn_up[l]), "w_ffn_down": bf(w_ffn_down[l]),
    }


def _pad_tokens(a, b, tokens):
    a = a.reshape(b, tokens, a.shape[-1])
    return jnp.pad(a, ((0, 0), (0, SAMPLE_TOK_PAD - tokens), (0, 0)))


def kernel(x_prompt, x_sample, mem_prompt, cache_ckv, cache_kpe, page_table, state_ret, cache_mem_k, cache_mem_v,
           norm_mix_pre, norm_mix_post, norm_ffn_pre, norm_ffn_post, norm_mem, norm_q_lat, norm_kv_lat, w_in,
           w_uq, w_uk, w_uv, w_mem_k, w_mem_v, w_ret_o, w_mla_o, w_x_o, w_out, w_ffn_gate, w_ffn_up, w_ffn_down):
    depth = w_in.shape[0]
    batch, seq, d_model = x_prompt.shape
    db, tokens, _ = x_sample.shape
    ret_heads, ret_dk, ret_dv = state_ret.shape[2:]
    heads, kv_lora, d_nope = w_uk.shape[1:]
    d_rope = cache_kpe.shape[-1]
    d_v = w_uv.shape[-1]
    n_mem, x_heads, x_hd = cache_mem_k.shape[2:]
    q_lora = w_uq.shape[1]
    past_len = page_table.shape[1] * cache_ckv.shape[2]
    dims = dict(heads=heads, d_nope=d_nope, d_rope=d_rope, kv_lora=kv_lora, d_v=d_v,
                ret_heads=ret_heads, ret_dk=ret_dk, ret_dv=ret_dv)
    assert ret_dk == LANES and d_nope == LANES and d_rope <= LANES // 2 and tokens <= SAMPLE_TOK_PAD
    sizes = (ret_heads * ret_dk, ret_heads * ret_dk, ret_heads * ret_dv, ret_heads * ret_dv, q_lora, kv_lora,
             d_rope, x_heads * x_hd, w_in.shape[2] - (2 * ret_heads * ret_dk + 2 * ret_heads * ret_dv + q_lora
                                                       + kv_lora + d_rope + x_heads * x_hd))

    tabs_p = _rope_tables(jnp.arange(seq), ret_dk, d_rope)
    n_s = db * tokens
    tabs_s = _rope_tables(past_len + (jnp.arange(n_s) % tokens), ret_dk, d_rope)

    y_p = x_prompt.reshape(batch * seq, d_model)
    y_s = x_sample.reshape(n_s, d_model)
    outs = [[] for _ in range(8)]
    for l in range(depth):
        w = _layer_weights(l, dims, sizes, norm_mix_pre, norm_mix_post, norm_ffn_pre, norm_ffn_post, norm_mem,
                           norm_q_lat, norm_kv_lat, w_in, w_uq, w_uk, w_uv, w_mem_k, w_mem_v, w_ret_o, w_mla_o,
                           w_x_o, w_out, w_ffn_gate, w_ffn_up, w_ffn_down)

        mk_p, mv_p = _mem_kv(mem_prompt, w["g_mem"], w["w_mem_k"], w["w_mem_v"])
        rq, rk, rv, rg, qcat, kcat, vn, xq, ckv_p, kpe_p = _in_proj(y_p, w, tabs_p, dims, absorbed=False,
                                                                    tm=ROW_TILE)
        o_ret, ret_p = _ret_prompt(rq, rk, rv, rg, batch, seq, dims)
        o_mla = _mla_prompt(qcat, kcat, vn, batch, seq, dims)
        o_x = _x_prompt(xq, mk_p, mv_p, batch, seq, x_heads)
        h_p = _merge(y_p, o_ret, o_mla, o_x, w, dims, absorbed=False, tm=ROW_TILE)
        y_p = _ffn(h_p, w, tm=ROW_TILE)

        rq, rk, rv, rg, qcat, qlat, _, xq, ckv_s, kpe_s = _in_proj(y_s, w, tabs_s, dims, absorbed=True,
                                                                   tm=ROW_TILE)
        pad = lambda a: _pad_tokens(a, db, tokens)
        o_ret, ret_s = _ret_sample(pad(rq), pad(rk), pad(rv), pad(rg), state_ret[l].astype(F32), tokens, dims)
        o_ret = o_ret[:, :tokens].reshape(n_s, ret_heads * ret_dv)
        group = d_nope + LANES
        q_pe = qcat.reshape(db, tokens * heads, group)[:, :, d_nope:d_nope + d_rope]
        o_lat = _mla_sample(qlat.reshape(db, tokens * heads, kv_lora), q_pe, pad(ckv_s), pad(kpe_s),
                            cache_ckv, cache_kpe, l, page_table, tokens, dims)
        o_lat = o_lat.reshape(n_s, heads * kv_lora)
        o_x = _x_sample(pad(xq), cache_mem_k[l].reshape(db, n_mem, x_heads * x_hd),
                        cache_mem_v[l].reshape(db, n_mem, x_heads * x_hd), x_heads)
        o_x = o_x[:, :tokens].reshape(n_s, x_heads * x_hd)
        h_s = _merge(y_s, o_ret, o_lat, o_x, w, dims, absorbed=True, tm=ROW_TILE)
        y_s = _ffn(h_s, w, tm=ROW_TILE)

        for lst, val in zip(outs, (ckv_p.reshape(batch, seq, kv_lora), kpe_p.reshape(batch, seq, d_rope),
                                   ckv_s.reshape(db, tokens, kv_lora), kpe_s.reshape(db, tokens, d_rope),
                                   ret_p.astype(x_prompt.dtype), ret_s.astype(state_ret.dtype),
                                   mk_p.reshape(batch, n_mem, x_heads, x_hd),
                                   mv_p.reshape(batch, n_mem, x_heads, x_hd))):
            lst.append(val)

    return (y_p.reshape(batch, seq, d_model), y_s.reshape(db, tokens, d_model)) + tuple(jnp.stack(o) for o in outs)
```

```python
import functools
import math

import numpy as np
import jax
import jax.numpy as jnp
from jax import lax
from jax.experimental import pallas as pl
from jax.experimental.pallas import tpu as pltpu

F32 = jnp.float32
BF16 = jnp.bfloat16

ROPE_BASE = 10000.0
RMS_EPS = 1e-6
LANES = 128
VMEM_LIMIT = 56 * 1024 * 1024
NEG_BIG = -1e30

ROW_TILE = 256
RET_CHUNK = 128
ATTN_TILE = 512
X_TILE = 512
ATTN_HEADS_PER_STEP = 2
SAMPLE_TOK_PAD = 16
X_SAMPLE_TOK_PAD = 8
RET_SAMPLE_GROUP = 8
X_SAMPLE_GROUP = 8
DECODE_CHUNK = 1024


def _resident(shape):
    nd = len(shape)
    return pl.BlockSpec(shape, lambda *_: (0,) * nd, pipeline_mode=pl.Buffered(1))


def _params(n_axes):
    return pltpu.CompilerParams(dimension_semantics=("arbitrary",) * n_axes, vmem_limit_bytes=VMEM_LIMIT)


def _rms(x, g=None):
    y = x * lax.rsqrt(jnp.mean(x * x, axis=-1, keepdims=True) + RMS_EPS)
    return y if g is None else y * g


def _sigmoid(x):
    return 1.0 / (1.0 + jnp.exp(-x))


def _dot(a, b):
    return jnp.dot(a, b, preferred_element_type=F32)


def _dot_nt(a, b):
    return lax.dot_general(a, b, (((1,), (1,)), ((), ())), preferred_element_type=F32)


def _dot_tn(a, b):
    return lax.dot_general(a, b, (((0,), (0,)), ((), ())), preferred_element_type=F32)


def _rope_half_vreg(z, cos_t, sin_lo, sin_hi, quarter):
    return (z * cos_t + pltpu.roll(z, LANES - quarter, 1) * sin_lo + pltpu.roll(z, quarter, 1) * sin_hi)


def _in_proj_kernel(x_ref, g_ref, wrq, wrk, wrv, wrg, wcq, wckv, wkpe, wxq, gq_ref, gkv_ref, wuqn, wuqp,
                    wa, wb, cosr, sinr, cosp, sinlo, sinhi,
                    rq_o, rk_o, rv_o, rg_o, qcat_o, a_o, b_o, xq_o, ckv_o, kpe_o,
                    *, ret_heads, ret_dk, heads, d_nope, d_rope, kv_lora, absorbed, q_scale):
    u = _rms(x_ref[...], g_ref[...]).astype(BF16)

    cr, sr = cosr[...], sinr[...]
    zq = _dot(u, wrq[...])
    zk = _dot(u, wrk[...])
    k_scale = ret_dk ** -0.5
    for h in range(ret_heads):
        sl = slice(h * ret_dk, (h + 1) * ret_dk)
        q_h = zq[:, sl]
        k_h = zk[:, sl]
        rq_o[:, sl] = (q_h * cr + pltpu.roll(q_h, ret_dk // 2, 1) * sr).astype(BF16)
        rk_o[:, sl] = ((k_h * cr + pltpu.roll(k_h, ret_dk // 2, 1) * sr) * k_scale).astype(BF16)
    rv_o[...] = _dot(u, wrv[...]).astype(BF16)
    rg_o[...] = _dot(u, wrg[...]).astype(BF16)

    cp, slo, shi = cosp[...], sinlo[...], sinhi[...]
    cqn = _rms(_dot(u, wcq[...]), gq_ref[...]).astype(BF16)
    qn = _dot(cqn, wuqn[...])
    qp = _dot(cqn, wuqp[...])
    group = d_nope + LANES
    for h in range(heads):
        qn_h = qn[:, h * d_nope:(h + 1) * d_nope]
        qp_h = _rope_half_vreg(qp[:, h * LANES:(h + 1) * LANES], cp, slo, shi, d_rope // 2)
        if absorbed:
            a_o[:, h * kv_lora:(h + 1) * kv_lora] = _dot(qn_h.astype(BF16), wa[h]).astype(BF16)
        else:
            qn_h = qn_h * q_scale
            qp_h = qp_h * q_scale
        qcat_o[:, h * group:h * group + d_nope] = qn_h.astype(BF16)
        qcat_o[:, h * group + d_nope:(h + 1) * group] = qp_h.astype(BF16)

    ckvn = _rms(_dot(u, wckv[...]), gkv_ref[...])
    ckv_o[...] = ckvn
    kpe = _rope_half_vreg(_dot(u, wkpe[...]), cp, slo, shi, d_rope // 2)
    kpe_o[...] = kpe[:, :d_rope]
    if not absorbed:
        ckvb = ckvn.astype(BF16)
        kn = _dot(ckvb, wa[...])
        kpeb = kpe.astype(BF16)
        for h in range(heads):
            a_o[:, h * group:h * group + d_nope] = kn[:, h * d_nope:(h + 1) * d_nope].astype(BF16)
            a_o[:, h * group + d_nope:(h + 1) * group] = kpeb
        b_o[...] = _dot(ckvb, wb[...]).astype(BF16)
    else:
        b_o[...] = jnp.zeros(b_o.shape, b_o.dtype)

    xq_o[...] = _dot(u, wxq[...]).astype(BF16)


def _in_proj(x, w, tabs, dims, *, absorbed, tm):
    n, d_model = x.shape
    heads, d_nope, d_rope, kv_lora = dims["heads"], dims["d_nope"], dims["d_rope"], dims["kv_lora"]
    ret_heads, ret_dk, ret_dv = dims["ret_heads"], dims["ret_dk"], dims["ret_dv"]
    group = d_nope + LANES
    tab_rows = tabs["cosr"].shape[0]
    tab_tiles = tab_rows // tm
    row = lambda c: pl.BlockSpec((tm, c), lambda i: (i, 0))
    tab = lambda: pl.BlockSpec((tm, LANES), lambda i: (i % tab_tiles, 0))
    wa = w["w_uk_t"] if absorbed else w["w_kn"]
    wb = w["w_vn"]
    a_cols = heads * kv_lora if absorbed else heads * group
    b_cols = LANES if absorbed else wb.shape[1]
    weights = [w["g_pre"], w["w_rq"], w["w_rk"], w["w_rv"], w["w_rg"], w["w_cq"], w["w_ckv"], w["w_kpe"],
               w["w_xq"], w["g_q"], w["g_kv"], w["w_uq_n"], w["w_uq_p"], wa, wb]
    out_cols = [(ret_heads * ret_dk, BF16), (ret_heads * ret_dk, BF16), (ret_heads * ret_dv, BF16),
                (ret_heads * ret_dv, BF16), (heads * group, BF16), (a_cols, BF16), (b_cols, BF16),
                (w["w_xq"].shape[1], BF16), (kv_lora, F32), (d_rope, F32)]
    kern = functools.partial(_in_proj_kernel, ret_heads=ret_heads, ret_dk=ret_dk, heads=heads, d_nope=d_nope,
                             d_rope=d_rope, kv_lora=kv_lora, absorbed=absorbed,
                             q_scale=(d_nope + d_rope) ** -0.5 * math.log2(math.e))
    return pl.pallas_call(
        kern,
        grid=(n // tm,),
        in_specs=[row(d_model)] + [_resident(a.shape) for a in weights] + [tab() for _ in range(5)],
        out_specs=[row(c) for c, _ in out_cols],
        out_shape=[jax.ShapeDtypeStruct((n, c), dt) for c, dt in out_cols],
        compiler_params=_params(1),
        name="in_proj_sample" if absorbed else "in_proj_prompt",
    )(x, *weights, tabs["cosr"], tabs["sinr"], tabs["cosp"], tabs["sinlo"], tabs["sinhi"])


def _ret_head(q, k, v, g, s, dec, qd, kd, g_l):
    inner = _dot_nt(q, k) * dec
    o = _dot(inner.astype(BF16), v) + _dot((q.astype(F32) * qd).astype(BF16), s.astype(BF16))
    s_new = s * g_l + _dot_tn((k.astype(F32) * kd).astype(BF16), v)
    gf = g.astype(F32)
    o = (gf * _sigmoid(gf)) * _rms(o)
    return o.astype(BF16), s_new


def _ret_prompt_kernel(rq, rk, rv, rg, dec, qd, kd, o_ref, s_out, s_scr, *, heads, dk, dv, g_l):
    c = pl.program_id(1)

    @pl.when(c == 0)
    def _():
        s_scr[...] = jnp.zeros(s_scr.shape, F32)

    for h in range(heads):
        o, s_new = _ret_head(rq[:, h * dk:(h + 1) * dk], rk[:, h * dk:(h + 1) * dk],
                             rv[:, h * dv:(h + 1) * dv], rg[:, h * dv:(h + 1) * dv],
                             s_scr[h], dec[h], qd[h], kd[h], g_l[h])
        o_ref[:, h * dv:(h + 1) * dv] = o
        s_scr[h] = s_new

    @pl.when(c == pl.num_programs(1) - 1)
    def _():
        s_out[0] = s_scr[...]


def _ret_consts(heads, length, dk, chunk_rows=None):
    rows = length if chunk_rows is None else chunk_rows
    lg = np.log1p(-np.exp2(-5.0 - np.arange(heads, dtype=np.float64)))
    i = np.arange(rows, dtype=np.float64)
    diff = i[:, None] - i[None, :]
    valid = (diff >= 0) & (i[:, None] < length) & (i[None, :] < length)
    dec = np.where(valid[None], np.exp(np.maximum(diff, 0.0)[None] * lg[:, None, None]), 0.0)
    qd = np.exp((i[None, :] + 1.0) * lg[:, None])
    kd = np.where(i[None, :] < length, np.exp((length - 1.0 - i[None, :]) * lg[:, None]), 0.0)
    qd = np.broadcast_to(qd[:, :, None], (heads, rows, dk))
    kd = np.broadcast_to(kd[:, :, None], (heads, rows, dk))
    g_l = tuple(float(v) for v in np.exp(length * lg))
    return (jnp.asarray(dec, F32), jnp.asarray(qd, F32), jnp.asarray(kd, F32), g_l)


def _ret_prompt(rq, rk, rv, rg, batch, seq, dims):
    heads, dk, dv = dims["ret_heads"], dims["ret_dk"], dims["ret_dv"]
    L = RET_CHUNK
    nc = seq // L
    dec, qd, kd, g_l = _ret_consts(heads, L, dk)
    row = lambda c: pl.BlockSpec((L, c), lambda b, i: (b * nc + i, 0))
    kern = functools.partial(_ret_prompt_kernel, heads=heads, dk=dk, dv=dv, g_l=g_l)
    return pl.pallas_call(
        kern,
        grid=(batch, nc),
        in_specs=[row(heads * dk), row(heads * dk), row(heads * dv), row(heads * dv),
                  _resident(dec.shape), _resident(qd.shape), _resident(kd.shape)],
        out_specs=[row(heads * dv), pl.BlockSpec((1, heads, dk, dv), lambda b, i: (b, 0, 0, 0))],
        out_shape=[jax.ShapeDtypeStruct((batch * seq, heads * dv), BF16),
                   jax.ShapeDtypeStruct((batch, heads, dk, dv), F32)],
        scratch_shapes=[pltpu.VMEM((heads, dk, dv), F32)],
        compiler_params=_params(2),
        name="retention_prompt",
    )(rq, rk, rv, rg, dec, qd, kd)


def _ret_sample_kernel(rq, rk, rv, rg, s0, dec, qd, kd, o_ref, s_out, *, group, heads, dk, dv, g_l):
    def body(i, carry):
        for h in range(heads):
            o, s_new = _ret_head(rq[i, :, h * dk:(h + 1) * dk], rk[i, :, h * dk:(h + 1) * dk],
                                 rv[i, :, h * dv:(h + 1) * dv], rg[i, :, h * dv:(h + 1) * dv],
                                 s0[i, h], dec[h], qd[h], kd[h], g_l[h])
            o_ref[i, :, h * dv:(h + 1) * dv] = o
            s_out[i, h] = s_new
        return carry

    lax.fori_loop(0, group, body, 0)


def _ret_sample(rq, rk, rv, rg, state, tokens, dims):
    heads, dk, dv = dims["ret_heads"], dims["ret_dk"], dims["ret_dv"]
    b, tp, _ = rq.shape
    G = RET_SAMPLE_GROUP
    dec, qd, kd, g_l = _ret_consts(heads, tokens, dk, chunk_rows=tp)
    blk = lambda c: pl.BlockSpec((G, tp, c), lambda i: (i, 0, 0))
    st = pl.BlockSpec((G, heads, dk, dv), lambda i: (i, 0, 0, 0))
    kern = functools.partial(_ret_sample_kernel, group=G, heads=heads, dk=dk, dv=dv, g_l=g_l)
    return pl.pallas_call(
        kern,
        grid=(b // G,),
        in_specs=[blk(heads * dk), blk(heads * dk), blk(heads * dv), blk(heads * dv), st,
                  _resident(dec.shape), _resident(qd.shape), _resident(kd.shape)],
        out_specs=[blk(heads * dv), st],
        out_shape=[jax.ShapeDtypeStruct((b, tp, heads * dv), BF16),
                   jax.ShapeDtypeStruct((b, heads, dk, dv), F32)],
        compiler_params=_params(1),
        name="retention_sample",
    )(rq, rk, rv, rg, state, dec, qd, kd)


def _mla_prompt_kernel(q_ref, k_ref, v_ref, o_ref, *, tile, hp, group, d_v):
    qi = pl.program_id(2)
    qs = [q_ref[:, h * group:(h + 1) * group] for h in range(hp)]

    def step(j, carry, masked):
        start = pl.multiple_of(j * tile, tile)
        out = []
        for h in range(hp):
            m, l, acc = carry[h]
            s = _dot_nt(qs[h], k_ref[pl.ds(start, tile), h * group:(h + 1) * group])
            if masked:
                r = lax.broadcasted_iota(jnp.int32, s.shape, 0)
                c = lax.broadcasted_iota(jnp.int32, s.shape, 1)
                s = jnp.where(c <= r, s, NEG_BIG)
            m_new = jnp.maximum(m, jnp.max(s, axis=-1, keepdims=True))
            a = jnp.exp2(m - m_new)
            p = jnp.exp2(s - m_new)
            l = a * l + jnp.sum(p, axis=-1, keepdims=True)
            acc = a * acc + _dot(p.astype(BF16), v_ref[pl.ds(start, tile), h * d_v:(h + 1) * d_v])
            out.append((m_new, l, acc))
        return tuple(out)

    init = tuple((jnp.full((tile, 1), NEG_BIG, F32), jnp.zeros((tile, 1), F32), jnp.zeros((tile, d_v), F32))
                 for _ in range(hp))
    carry = lax.fori_loop(0, qi, functools.partial(step, masked=False), init)
    carry = step(qi, carry, True)
    for h in range(hp):
        _, l, acc = carry[h]
        o_ref[:, h * d_v:(h + 1) * d_v] = (acc / l).astype(BF16)


def _mla_prompt(qcat, kcat, v, batch, seq, dims):
    heads, d_nope, d_v = dims["heads"], dims["d_nope"], dims["d_v"]
    group = d_nope + LANES
    t = ATTN_TILE
    hp = ATTN_HEADS_PER_STEP
    nq = seq // t
    kern = functools.partial(_mla_prompt_kernel, tile=t, hp=hp, group=group, d_v=d_v)
    return pl.pallas_call(
        kern,
        grid=(batch, heads // hp, nq),
        in_specs=[pl.BlockSpec((t, hp * group), lambda b, h, i: (b * nq + i, h)),
                  pl.BlockSpec((seq, hp * group), lambda b, h, i: (b, h)),
                  pl.BlockSpec((seq, hp * d_v), lambda b, h, i: (b, h))],
        out_specs=pl.BlockSpec((t, hp * d_v), lambda b, h, i: (b * nq + i, h)),
        out_shape=jax.ShapeDtypeStruct((batch * seq, heads * d_v), BF16),
        compiler_params=_params(3),
        name="mla_prompt",
    )(qcat, kcat, v)


def _mla_sample_kernel(pt_ref, ql_ref, qp_ref, cn_ref, kn_ref, ckv_hbm, kpe_hbm, o_ref,
                       ckv_buf, kpe_buf, kbf, s_scr, sem,
                       *, layer, n_pages, page, chunk, tokens, heads, scale):
    b = pl.program_id(0)
    past = n_pages * page

    def page_copies(seq, slot, p):
        pg = pt_ref[seq, p]
        off = pl.multiple_of(p * page, page)
        return (pltpu.make_async_copy(ckv_hbm.at[layer, pg], ckv_buf.at[slot, pl.ds(off, page), :], sem.at[0, slot]),
                pltpu.make_async_copy(kpe_hbm.at[layer, pg], kpe_buf.at[slot, :, pl.ds(off, page)], sem.at[1, slot]))

    def for_each_page(seq, slot, fn):
        def body(p, carry):
            for cp in page_copies(seq, slot, p):
                fn(cp)
            return carry
        lax.fori_loop(0, n_pages, body, 0)

    @pl.when(b == 0)
    def _():
        for_each_page(0, 0, lambda cp: cp.start())

    @pl.when(b + 1 < pl.num_programs(0))
    def _():
        for_each_page(b + 1, (b + 1) % 2, lambda cp: cp.start())

    slot = b % 2
    for_each_page(b, slot, lambda cp: cp.wait())

    ql = ql_ref[0]
    qp = qp_ref[0]
    ckv_s = ckv_buf.at[slot]
    kpe_s = kpe_buf.at[slot]
    n_chunks = past // chunk

    for c in range(n_chunks):
        sl = slice(c * chunk, (c + 1) * chunk)
        kc = ckv_s[sl, :].astype(BF16)
        kbf[sl, :] = kc
        s_scr[:, sl] = (_dot_nt(ql, kc) + _dot(qp, kpe_s[:, sl].astype(BF16))) * scale

    cn = cn_ref[0].astype(BF16)
    kn = kn_ref[0].astype(BF16)
    s_new = (_dot_nt(ql, cn) + _dot_nt(qp, kn)) * scale
    t = lax.broadcasted_iota(jnp.int32, s_new.shape, 0) // heads
    col = lax.broadcasted_iota(jnp.int32, s_new.shape, 1)
    s_new = jnp.where((col <= t) & (col < tokens), s_new, NEG_BIG)

    mx = s_scr[:, 0:chunk]
    for c in range(1, n_chunks):
        mx = jnp.maximum(mx, s_scr[:, c * chunk:(c + 1) * chunk])
    m = jnp.maximum(jnp.max(mx, axis=-1, keepdims=True), jnp.max(s_new, axis=-1, keepdims=True))

    p_new = jnp.exp(s_new - m)
    acc = _dot(p_new.astype(BF16), cn)
    lsum = jnp.zeros((s_scr.shape[0], chunk), F32)
    for c in range(n_chunks):
        sl = slice(c * chunk, (c + 1) * chunk)
        p = jnp.exp(s_scr[:, sl] - m)
        lsum = lsum + p
        acc = acc + _dot(p.astype(BF16), kbf[sl, :])
    l = jnp.sum(lsum, axis=-1, keepdims=True) + jnp.sum(p_new, axis=-1, keepdims=True)
    o_ref[0] = acc / l


def _mla_sample(ql, qp, ckv_new, kpe_new, cache_ckv, cache_kpe_t, layer, page_table, tokens, dims):
    heads, d_nope, d_rope, kv_lora = dims["heads"], dims["d_nope"], dims["d_rope"], dims["kv_lora"]
    b, rows, _ = ql.shape
    n_pages = page_table.shape[1]
    page = cache_ckv.shape[2]
    past = n_pages * page
    tp = ckv_new.shape[1]
    per_b = lambda r, c: pl.BlockSpec((1, r, c), lambda i, pt: (i, 0, 0))
    hbm = pl.BlockSpec(memory_space=pl.ANY)
    kern = functools.partial(_mla_sample_kernel, layer=layer, n_pages=n_pages, page=page, chunk=DECODE_CHUNK,
                             tokens=tokens, heads=heads, scale=(d_nope + d_rope) ** -0.5)
    grid_spec = pltpu.PrefetchScalarGridSpec(
        num_scalar_prefetch=1,
        grid=(b,),
        in_specs=[per_b(rows, kv_lora), per_b(rows, d_rope), per_b(tp, kv_lora), per_b(tp, d_rope), hbm, hbm],
        out_specs=per_b(rows, kv_lora),
        scratch_shapes=[pltpu.VMEM((2, past, kv_lora), F32), pltpu.VMEM((2, d_rope, past), F32),
                        pltpu.VMEM((past, kv_lora), BF16), pltpu.VMEM((rows, past), F32),
                        pltpu.SemaphoreType.DMA((2, 2))],
    )
    return pl.pallas_call(
        kern,
        grid_spec=grid_spec,
        out_shape=jax.ShapeDtypeStruct((b, rows, kv_lora), F32),
        compiler_params=_params(1),
        name="mla_sample",
    )(page_table, ql, qp, ckv_new, kpe_new, cache_ckv, cache_kpe_t)


def _mem_kv_kernel(m_ref, g_ref, wk, wv, k_o, v_o):
    mn = _rms(m_ref[0], g_ref[...]).astype(BF16)
    k_o[0] = _dot(mn, wk[...])
    v_o[0] = _dot(mn, wv[...])


def _mem_kv(mem, g, wk, wv):
    b, m, d = mem.shape
    c = wk.shape[1]
    return pl.pallas_call(
        _mem_kv_kernel,
        grid=(b,),
        in_specs=[pl.BlockSpec((1, m, d), lambda i: (i, 0, 0)), _resident(g.shape), _resident(wk.shape),
                  _resident(wv.shape)],
        out_specs=[pl.BlockSpec((1, m, c), lambda i: (i, 0, 0))] * 2,
        out_shape=[jax.ShapeDtypeStruct((b, m, c), F32)] * 2,
        compiler_params=_params(1),
        name="mem_kv",
    )(mem, g, wk, wv)


def _x_attend_heads(q, mk, mv, heads, hd, rows_per_head=None):
    lane = lax.broadcasted_iota(jnp.int32, q.shape, 1)
    out = jnp.zeros(q.shape, F32)
    for h in range(heads):
        sel = (lane >= h * hd) & (lane < (h + 1) * hd)
        s = _dot_nt(jnp.where(sel, q, jnp.zeros_like(q)), mk) * (hd ** -0.5)
        p = jnp.exp(s - jnp.max(s, axis=-1, keepdims=True))
        p = p / jnp.sum(p, axis=-1, keepdims=True)
        out = out + jnp.where(sel, _dot(p.astype(BF16), mv), 0.0)
    return out


def _x_prompt_kernel(q_ref, k_ref, v_ref, o_ref, *, heads, hd):
    o_ref[...] = _x_attend_heads(q_ref[...], k_ref[0].astype(BF16), v_ref[0].astype(BF16), heads, hd).astype(BF16)


def _x_prompt(xq, mk, mv, batch, seq, heads):
    m, c = mk.shape[1:]
    t = X_TILE
    nt = seq // t
    kern = functools.partial(_x_prompt_kernel, heads=heads, hd=c // heads)
    return pl.pallas_call(
        kern,
        grid=(batch, nt),
        in_specs=[pl.BlockSpec((t, c), lambda b, i: (b * nt + i, 0)),
                  pl.BlockSpec((1, m, c), lambda b, i: (b, 0, 0)),
                  pl.BlockSpec((1, m, c), lambda b, i: (b, 0, 0))],
        out_specs=pl.BlockSpec((t, c), lambda b, i: (b * nt + i, 0)),
        out_shape=jax.ShapeDtypeStruct((batch * seq, c), BF16),
        compiler_params=_params(2),
        name="x_attend_prompt",
    )(xq, mk, mv)


def _x_sample_kernel(q_ref, kt_ref, vt_ref, o_ref, *, group, heads, hd):
    tp = q_ref.shape[1]
    lane = lax.broadcasted_iota(jnp.int32, q_ref.shape[1:], 1)
    sels = [(lane >= h * hd) & (lane < (h + 1) * hd) for h in range(heads)]
    for g in range(group):
        q = q_ref[g]
        q_heads = jnp.concatenate([jnp.where(sel, q, 0.0) for sel in sels], axis=0).astype(BF16)
        s = _dot(q_heads, kt_ref[g].astype(BF16)) * (hd ** -0.5)
        p = jnp.exp(s - jnp.max(s, axis=-1, keepdims=True))
        p = p / jnp.sum(p, axis=-1, keepdims=True)
        r = _dot_nt(p.astype(BF16), vt_ref[g].astype(BF16))
        out = jnp.zeros(q.shape, F32)
        for h, sel in enumerate(sels):
            out = out + jnp.where(sel, r[h * tp:(h + 1) * tp], 0.0)
        o_ref[g] = out


def _x_sample(xq, mkt, mvt, heads):
    b, tp, c = xq.shape
    m = mkt.shape[2]
    G = X_SAMPLE_GROUP
    kern = functools.partial(_x_sample_kernel, group=G, heads=heads, hd=c // heads)
    return pl.pallas_call(
        kern,
        grid=(b // G,),
        in_specs=[pl.BlockSpec((G, tp, c), lambda i: (i, 0, 0)),
                  pl.BlockSpec((G, c, m), lambda i: (i, 0, 0)),
                  pl.BlockSpec((G, c, m), lambda i: (i, 0, 0))],
        out_specs=pl.BlockSpec((G, tp, c), lambda i: (i, 0, 0)),
        out_shape=jax.ShapeDtypeStruct((b, tp, c), F32),
        compiler_params=_params(1),
        name="x_attend_sample",
    )(xq, mkt, mvt)


def _merge_kernel(x_ref, gpre, wgate, oret, wret, omla, wmla, wuv, ox, wx, wout, gpost, h_o,
                  *, heads, kv_lora, d_v, absorbed):
    x = x_ref[...]
    d = x.shape[1]
    u = _rms(x, gpre[...]).astype(BF16)
    a_ret = _dot(oret[...], wret[...])
    if absorbed:
        a_mla = jnp.zeros(x.shape, F32)
        for h in range(heads):
            o_h = _dot(omla[:, h * kv_lora:(h + 1) * kv_lora].astype(BF16), wuv[h]).astype(BF16)
            a_mla = a_mla + _dot(o_h, wmla[h * d_v:(h + 1) * d_v, :])
    else:
        a_mla = _dot(omla[...], wmla[...])
    a_x = _dot(ox[...].astype(BF16), wx[...])
    mixed = jnp.zeros(x.shape, F32)
    for i, a in enumerate((a_ret, a_mla, a_x)):
        mixed = mixed + _sigmoid(_dot(u, wgate[:, i * d:(i + 1) * d])) * a
    h_o[...] = x + _rms(_dot(mixed.astype(BF16), wout[...]), gpost[...])


def _merge(x, oret, omla, ox, w, dims, *, absorbed, tm):
    n, d = x.shape
    row = lambda a: pl.BlockSpec((tm, a.shape[1]), lambda i: (i, 0))
    res = lambda a: _resident(a.shape)
    kern = functools.partial(_merge_kernel, heads=dims["heads"], kv_lora=dims["kv_lora"], d_v=dims["d_v"],
                             absorbed=absorbed)
    args = [x, w["g_pre"], w["w_gates"], oret, w["w_ret_o"], omla, w["w_mla_o"], w["w_uv_h"], ox, w["w_x_o"],
            w["w_out"], w["g_post"]]
    specs = [row(x), res(w["g_pre"]), res(w["w_gates"]), row(oret), res(w["w_ret_o"]), row(omla),
             res(w["w_mla_o"]), res(w["w_uv_h"]), row(ox), res(w["w_x_o"]), res(w["w_out"]), res(w["g_post"])]
    return pl.pallas_call(
        kern,
        grid=(n // tm,),
        in_specs=specs,
        out_specs=pl.BlockSpec((tm, d), lambda i: (i, 0)),
        out_shape=jax.ShapeDtypeStruct((n, d), F32),
        compiler_params=_params(1),
        name="merge_sample" if absorbed else "merge_prompt",
    )(*args)


def _ffn_kernel(h_ref, gpre, wg, wu, wd, gpost, y_o):
    h = h_ref[...]
    f = _rms(h, gpre[...]).astype(BF16)
    gate = _dot(f, wg[...])
    act = (gate * _sigmoid(gate) * _dot(f, wu[...])).astype(BF16)
    y_o[...] = h + _rms(_dot(act, wd[...]), gpost[...])


def _ffn(h, w, *, tm):
    n, d = h.shape
    res = lambda a: _resident(a.shape)
    return pl.pallas_call(
        _ffn_kernel,
        grid=(n // tm,),
        in_specs=[pl.BlockSpec((tm, d), lambda i: (i, 0)), res(w["g_ffn_pre"]), res(w["w_ffn_gate"]),
                  res(w["w_ffn_up"]), res(w["w_ffn_down"]), res(w["g_ffn_post"])],
        out_specs=pl.BlockSpec((tm, d), lambda i: (i, 0)),
        out_shape=jax.ShapeDtypeStruct((n, d), F32),
        compiler_params=_params(1),
        name="ffn",
    )(h, w["g_ffn_pre"], w["w_ffn_gate"], w["w_ffn_up"], w["w_ffn_down"], w["g_ffn_post"])


def _rope_tables(pos, ret_dk, d_rope):
    posf = pos.astype(F32)[:, None]

    def angles(half):
        inv = ROPE_BASE ** (-jnp.arange(half, dtype=F32) / half)
        ang = posf * inv[None, :]
        return jnp.cos(ang), jnp.sin(ang)

    cr, sr = angles(ret_dk // 2)
    cp, sp = angles(d_rope // 2)
    z = jnp.zeros_like(cp)
    pad = jnp.zeros((pos.shape[0], LANES - d_rope), F32)
    return {
        "cosr": jnp.concatenate([cr, cr], axis=1),
        "sinr": jnp.concatenate([-sr, sr], axis=1),
        "cosp": jnp.concatenate([cp, cp, pad], axis=1),
        "sinlo": jnp.concatenate([-sp, z, pad], axis=1),
        "sinhi": jnp.concatenate([z, sp, pad], axis=1),
    }


def _layer_weights(l, dims, sizes, norm_mix_pre, norm_mix_post, norm_ffn_pre, norm_ffn_post, norm_mem, norm_q_lat,
                   norm_kv_lat, w_in, w_uq, w_uk, w_uv, w_mem_k, w_mem_v, w_ret_o, w_mla_o, w_x_o, w_out,
                   w_ffn_gate, w_ffn_up, w_ffn_down):
    heads, d_nope, d_rope, kv_lora, d_v = (dims[k] for k in ("heads", "d_nope", "d_rope", "kv_lora", "d_v"))
    bf = lambda a: a.astype(BF16)
    gain = lambda a: a[l].astype(F32)[None, :]
    offs = np.concatenate([[0], np.cumsum(sizes)])
    seg = [w_in[l][:, offs[i]:offs[i + 1]] for i in range(len(sizes))]
    q_lora = w_uq.shape[1]
    uq = w_uq[l].reshape(q_lora, heads, d_nope + d_rope)
    uq_p = jnp.pad(uq[:, :, d_nope:], ((0, 0), (0, 0), (0, LANES - d_rope)))
    return {
        "g_pre": gain(norm_mix_pre), "g_post": gain(norm_mix_post), "g_ffn_pre": gain(norm_ffn_pre),
        "g_ffn_post": gain(norm_ffn_post), "g_mem": gain(norm_mem), "g_q": gain(norm_q_lat),
        "g_kv": gain(norm_kv_lat),
        "w_rq": bf(seg[0]), "w_rk": bf(seg[1]), "w_rv": bf(seg[2]), "w_rg": bf(seg[3]), "w_cq": bf(seg[4]),
        "w_ckv": bf(seg[5]), "w_kpe": bf(jnp.pad(seg[6], ((0, 0), (0, LANES - d_rope)))), "w_xq": bf(seg[7]),
        "w_gates": bf(seg[8]),
        "w_uq_n": bf(uq[:, :, :d_nope].reshape(q_lora, heads * d_nope)),
        "w_uq_p": bf(uq_p.reshape(q_lora, heads * LANES)),
        "w_uk_t": bf(jnp.swapaxes(w_uk[l], 1, 2)),
        "w_kn": bf(jnp.swapaxes(w_uk[l], 0, 1).reshape(kv_lora, heads * d_nope)),
        "w_vn": bf(jnp.swapaxes(w_uv[l], 0, 1).reshape(kv_lora, heads * d_v)),
        "w_uv_h": bf(w_uv[l]),
        "w_mem_k": bf(w_mem_k[l]), "w_mem_v": bf(w_mem_v[l]),
        "w_ret_o": bf(w_ret_o[l]), "w_mla_o": bf(w_mla_o[l]), "w_x_o": bf(w_x_o[l]), "w_out": bf(w_out[l]),
        "w_ffn_gate": bf(w_ffn_gate[l]), "w_ffn_up": bf(w_ffn_up[l]), "w_ffn_down": bf(w_ffn_down[l]),
    }


def _pad_tokens(a, b, tokens, rows=SAMPLE_TOK_PAD):
    a = a.reshape(b, tokens, a.shape[-1])
    return jnp.pad(a, ((0, 0), (0, rows - tokens), (0, 0)))


def kernel(x_prompt, x_sample, mem_prompt, cache_ckv, cache_kpe, page_table, state_ret, cache_mem_k, cache_mem_v,
           norm_mix_pre, norm_mix_post, norm_ffn_pre, norm_ffn_post, norm_mem, norm_q_lat, norm_kv_lat, w_in,
           w_uq, w_uk, w_uv, w_mem_k, w_mem_v, w_ret_o, w_mla_o, w_x_o, w_out, w_ffn_gate, w_ffn_up, w_ffn_down):
    depth = w_in.shape[0]
    batch, seq, d_model = x_prompt.shape
    db, tokens, _ = x_sample.shape
    ret_heads, ret_dk, ret_dv = state_ret.shape[2:]
    heads, kv_lora, d_nope = w_uk.shape[1:]
    d_rope = cache_kpe.shape[-1]
    d_v = w_uv.shape[-1]
    n_mem, x_heads, x_hd = cache_mem_k.shape[2:]
    q_lora = w_uq.shape[1]
    past_len = page_table.shape[1] * cache_ckv.shape[2]
    dims = dict(heads=heads, d_nope=d_nope, d_rope=d_rope, kv_lora=kv_lora, d_v=d_v,
                ret_heads=ret_heads, ret_dk=ret_dk, ret_dv=ret_dv)
    assert ret_dk == LANES and d_nope == LANES and d_rope <= LANES // 2 and tokens <= SAMPLE_TOK_PAD
    sizes = (ret_heads * ret_dk, ret_heads * ret_dk, ret_heads * ret_dv, ret_heads * ret_dv, q_lora, kv_lora,
             d_rope, x_heads * x_hd, w_in.shape[2] - (2 * ret_heads * ret_dk + 2 * ret_heads * ret_dv + q_lora
                                                       + kv_lora + d_rope + x_heads * x_hd))

    cache_kpe_t = jnp.swapaxes(cache_kpe, 2, 3)

    tabs_p = _rope_tables(jnp.arange(seq), ret_dk, d_rope)
    n_s = db * tokens
    tabs_s = _rope_tables(past_len + (jnp.arange(n_s) % tokens), ret_dk, d_rope)

    y_p = x_prompt.reshape(batch * seq, d_model)
    y_s = x_sample.reshape(n_s, d_model)
    outs = [[] for _ in range(8)]
    for l in range(depth):
        w = _layer_weights(l, dims, sizes, norm_mix_pre, norm_mix_post, norm_ffn_pre, norm_ffn_post, norm_mem,
                           norm_q_lat, norm_kv_lat, w_in, w_uq, w_uk, w_uv, w_mem_k, w_mem_v, w_ret_o, w_mla_o,
                           w_x_o, w_out, w_ffn_gate, w_ffn_up, w_ffn_down)

        mk_p, mv_p = _mem_kv(mem_prompt, w["g_mem"], w["w_mem_k"], w["w_mem_v"])
        rq, rk, rv, rg, qcat, kcat, vn, xq, ckv_p, kpe_p = _in_proj(y_p, w, tabs_p, dims, absorbed=False,
                                                                    tm=ROW_TILE)
        o_ret, ret_p = _ret_prompt(rq, rk, rv, rg, batch, seq, dims)
        o_mla = _mla_prompt(qcat, kcat, vn, batch, seq, dims)
        o_x = _x_prompt(xq, mk_p, mv_p, batch, seq, x_heads)
        h_p = _merge(y_p, o_ret, o_mla, o_x, w, dims, absorbed=False, tm=ROW_TILE)
        y_p = _ffn(h_p, w, tm=ROW_TILE)

        rq, rk, rv, rg, qcat, qlat, _, xq, ckv_s, kpe_s = _in_proj(y_s, w, tabs_s, dims, absorbed=True,
                                                                   tm=ROW_TILE)
        pad = lambda a: _pad_tokens(a, db, tokens)
        o_ret, ret_s = _ret_sample(pad(rq), pad(rk), pad(rv), pad(rg), state_ret[l].astype(F32), tokens, dims)
        o_ret = o_ret[:, :tokens].reshape(n_s, ret_heads * ret_dv)
        group = d_nope + LANES
        q_pe = qcat.reshape(db, tokens * heads, group)[:, :, d_nope:d_nope + d_rope]
        o_lat = _mla_sample(qlat.reshape(db, tokens * heads, kv_lora), q_pe, pad(ckv_s), pad(kpe_s),
                            cache_ckv, cache_kpe_t, l, page_table, tokens, dims)
        o_lat = o_lat.reshape(n_s, heads * kv_lora)
        mem_t = lambda c: jnp.transpose(c[l], (0, 2, 3, 1)).reshape(db, x_heads * x_hd, n_mem)
        o_x = _x_sample(_pad_tokens(xq.astype(F32), db, tokens, rows=X_SAMPLE_TOK_PAD), mem_t(cache_mem_k),
                        mem_t(cache_mem_v), x_heads)
        o_x = o_x[:, :tokens].reshape(n_s, x_heads * x_hd)
        h_s = _merge(y_s, o_ret, o_lat, o_x, w, dims, absorbed=True, tm=ROW_TILE)
        y_s = _ffn(h_s, w, tm=ROW_TILE)

        for lst, val in zip(outs, (ckv_p.reshape(batch, seq, kv_lora), kpe_p.reshape(batch, seq, d_rope),
                                   ckv_s.reshape(db, tokens, kv_lora), kpe_s.reshape(db, tokens, d_rope),
                                   ret_p.astype(x_prompt.dtype), ret_s.astype(state_ret.dtype),
                                   mk_p.reshape(batch, n_mem, x_heads, x_hd),
                                   mv_p.reshape(batch, n_mem, x_heads, x_hd))):
            lst.append(val)

    return (y_p.reshape(batch, seq, d_model), y_s.reshape(db, tokens, d_model)) + tuple(jnp.stack(o) for o in outs)
```

```python
import functools
import math

import numpy as np
import jax
import jax.numpy as jnp
from jax import lax
from jax.experimental import pallas as pl
from jax.experimental.pallas import tpu as pltpu

F32 = jnp.float32
BF16 = jnp.bfloat16

ROPE_BASE = 10000.0
RMS_EPS = 1e-6
LANES = 128
VMEM_LIMIT = 56 * 1024 * 1024
NEG_BIG = -1e30

ROW_TILE = 256
RET_CHUNK = 256
ATTN_Q_TILE = 512
ATTN_KV_TILE = 512
ATTN_HEADS_PER_STEP = 4
SAMPLE_TOK_PAD = 16
X_SAMPLE_TOK_PAD = 8
RET_SAMPLE_GROUP = 8
X_SAMPLE_GROUP = 8
DECODE_CHUNK = 2048
DMA_LOOP_UNROLL = 4


def _resident(shape):
    nd = len(shape)
    return pl.BlockSpec(shape, lambda *_: (0,) * nd, pipeline_mode=pl.Buffered(1))


def _params(n_axes):
    return pltpu.CompilerParams(dimension_semantics=("arbitrary",) * n_axes, vmem_limit_bytes=VMEM_LIMIT)


def _rms(x, g=None):
    y = x * lax.rsqrt(jnp.mean(x * x, axis=-1, keepdims=True) + RMS_EPS)
    return y if g is None else y * g


def _sigmoid(x):
    return 1.0 / (1.0 + jnp.exp(-x))


def _dot(a, b):
    return jnp.dot(a, b, preferred_element_type=F32)


def _dot_nt(a, b):
    return lax.dot_general(a, b, (((1,), (1,)), ((), ())), preferred_element_type=F32)


def _dot_tn(a, b):
    return lax.dot_general(a, b, (((0,), (0,)), ((), ())), preferred_element_type=F32)


def _rope_half_vreg(z, cos_t, sin_lo, sin_hi, quarter):
    return (z * cos_t + pltpu.roll(z, LANES - quarter, 1) * sin_lo + pltpu.roll(z, quarter, 1) * sin_hi)


def _in_proj_kernel(x_ref, g_ref, wrq, wrk, wrv, wrg, wcq, wckv, wkpe, wxq, gq_ref, gkv_ref, wuqn, wuqp,
                    cosr, sinr, cosp, sinlo, sinhi, *rest,
                    ret_heads, ret_dk, heads, d_nope, d_rope, kv_lora, absorbed, q_scale):
    if absorbed:
        wa, (rq_o, rk_o, rv_o, rg_o, qcat_o, xq_o, ckv_o, kpe_o, a_o) = rest[0], rest[1:]
    else:
        wa, wb, (rq_o, rk_o, rv_o, rg_o, qcat_o, xq_o, ckv_o, kpe_o, a_o, b_o) = rest[0], rest[1], rest[2:]
    u = _rms(x_ref[...], g_ref[...]).astype(BF16)

    cr, sr = cosr[...], sinr[...]
    zq = _dot(u, wrq[...])
    zk = _dot(u, wrk[...])
    k_scale = ret_dk ** -0.5
    for h in range(ret_heads):
        sl = slice(h * ret_dk, (h + 1) * ret_dk)
        q_h = zq[:, sl]
        k_h = zk[:, sl]
        rq_o[:, sl] = (q_h * cr + pltpu.roll(q_h, ret_dk // 2, 1) * sr).astype(BF16)
        rk_o[:, sl] = ((k_h * cr + pltpu.roll(k_h, ret_dk // 2, 1) * sr) * k_scale).astype(BF16)
    rv_o[...] = _dot(u, wrv[...]).astype(BF16)
    rg_o[...] = _dot(u, wrg[...]).astype(BF16)

    cp, slo, shi = cosp[...], sinlo[...], sinhi[...]
    cqn = _rms(_dot(u, wcq[...]), gq_ref[...]).astype(BF16)
    qn = _dot(cqn, wuqn[...])
    qp = _dot(cqn, wuqp[...])
    group = d_nope + LANES
    for h in range(heads):
        qn_h = qn[:, h * d_nope:(h + 1) * d_nope]
        qp_h = _rope_half_vreg(qp[:, h * LANES:(h + 1) * LANES], cp, slo, shi, d_rope // 2)
        if absorbed:
            a_o[:, h * kv_lora:(h + 1) * kv_lora] = _dot(qn_h.astype(BF16), wa[h]).astype(BF16)
        else:
            qn_h = qn_h * q_scale
            qp_h = qp_h * q_scale
        qcat_o[:, h * group:h * group + d_nope] = qn_h.astype(BF16)
        qcat_o[:, h * group + d_nope:(h + 1) * group] = qp_h.astype(BF16)

    ckvn = _rms(_dot(u, wckv[...]), gkv_ref[...])
    ckv_o[...] = ckvn
    kpe = _rope_half_vreg(_dot(u, wkpe[...]), cp, slo, shi, d_rope // 2)
    kpe_o[...] = kpe[:, :d_rope]
    if not absorbed:
        ckvb = ckvn.astype(BF16)
        kn = _dot(ckvb, wa[...])
        kpeb = kpe.astype(BF16)
        for h in range(heads):
            a_o[:, h * group:h * group + d_nope] = kn[:, h * d_nope:(h + 1) * d_nope].astype(BF16)
            a_o[:, h * group + d_nope:(h + 1) * group] = kpeb
        b_o[...] = _dot(ckvb, wb[...]).astype(BF16)

    xq_o[...] = _dot(u, wxq[...]).astype(BF16)


def _in_proj(x, w, tabs, dims, *, absorbed, tm):
    n, d_model = x.shape
    heads, d_nope, d_rope, kv_lora = dims["heads"], dims["d_nope"], dims["d_rope"], dims["kv_lora"]
    ret_heads, ret_dk, ret_dv = dims["ret_heads"], dims["ret_dk"], dims["ret_dv"]
    group = d_nope + LANES
    tab_rows = tabs["cosr"].shape[0]
    tab_tiles = tab_rows // tm
    row = lambda c: pl.BlockSpec((tm, c), lambda i: (i, 0))
    tab = lambda: pl.BlockSpec((tm, LANES), lambda i: (i % tab_tiles, 0))
    weights = [w["g_pre"], w["w_rq"], w["w_rk"], w["w_rv"], w["w_rg"], w["w_cq"], w["w_ckv"], w["w_kpe"],
               w["w_xq"], w["g_q"], w["g_kv"], w["w_uq_n"], w["w_uq_p"]]
    out_cols = [(ret_heads * ret_dk, BF16), (ret_heads * ret_dk, BF16), (ret_heads * ret_dv, BF16),
                (ret_heads * ret_dv, BF16), (heads * group, BF16), (w["w_xq"].shape[1], BF16), (kv_lora, F32),
                (d_rope, F32)]
    if absorbed:
        mode_weights = [w["w_uk_t"]]
        out_cols += [(heads * kv_lora, BF16)]
    else:
        mode_weights = [w["w_kn"], w["w_vn"]]
        out_cols += [(heads * group, BF16), (w["w_vn"].shape[1], BF16)]
    kern = functools.partial(_in_proj_kernel, ret_heads=ret_heads, ret_dk=ret_dk, heads=heads, d_nope=d_nope,
                             d_rope=d_rope, kv_lora=kv_lora, absorbed=absorbed,
                             q_scale=(d_nope + d_rope) ** -0.5 * math.log2(math.e))
    return pl.pallas_call(
        kern,
        grid=(n // tm,),
        in_specs=([row(d_model)] + [_resident(a.shape) for a in weights] + [tab() for _ in range(5)]
                  + [_resident(a.shape) for a in mode_weights]),
        out_specs=[row(c) for c, _ in out_cols],
        out_shape=[jax.ShapeDtypeStruct((n, c), dt) for c, dt in out_cols],
        compiler_params=_params(1),
        name="in_proj_sample" if absorbed else "in_proj_prompt",
    )(x, *weights, tabs["cosr"], tabs["sinr"], tabs["cosp"], tabs["sinlo"], tabs["sinhi"], *mode_weights)


def _ret_head(q, k, v, g, s, dec, qd, kd, g_l):
    inner = _dot_nt(q, k) * dec
    o = _dot(inner.astype(BF16), v) + _dot((q.astype(F32) * qd).astype(BF16), s.astype(BF16))
    s_new = s * g_l + _dot_tn((k.astype(F32) * kd).astype(BF16), v)
    gf = g.astype(F32)
    o = (gf * _sigmoid(gf)) * _rms(o)
    return o.astype(BF16), s_new


def _ret_prompt_kernel(rq, rk, rv, rg, dec, qd, kd, o_ref, s_out, s_scr, *, heads, dk, dv, g_l):
    c = pl.program_id(1)

    @pl.when(c == 0)
    def _():
        s_scr[...] = jnp.zeros(s_scr.shape, F32)

    for h in range(heads):
        o, s_new = _ret_head(rq[:, h * dk:(h + 1) * dk], rk[:, h * dk:(h + 1) * dk],
                             rv[:, h * dv:(h + 1) * dv], rg[:, h * dv:(h + 1) * dv],
                             s_scr[h], dec[h], qd[h], kd[h], g_l[h])
        o_ref[:, h * dv:(h + 1) * dv] = o
        s_scr[h] = s_new

    @pl.when(c == pl.num_programs(1) - 1)
    def _():
        s_out[0] = s_scr[...]


def _ret_consts(heads, length, dk, chunk_rows=None):
    rows = length if chunk_rows is None else chunk_rows
    lg = np.log1p(-np.exp2(-5.0 - np.arange(heads, dtype=np.float64)))
    i = np.arange(rows, dtype=np.float64)
    diff = i[:, None] - i[None, :]
    valid = (diff >= 0) & (i[:, None] < length) & (i[None, :] < length)
    dec = np.where(valid[None], np.exp(np.maximum(diff, 0.0)[None] * lg[:, None, None]), 0.0)
    qd = np.exp((i[None, :] + 1.0) * lg[:, None])
    kd = np.where(i[None, :] < length, np.exp((length - 1.0 - i[None, :]) * lg[:, None]), 0.0)
    qd = np.broadcast_to(qd[:, :, None], (heads, rows, dk))
    kd = np.broadcast_to(kd[:, :, None], (heads, rows, dk))
    g_l = tuple(float(v) for v in np.exp(length * lg))
    return (jnp.asarray(dec, F32), jnp.asarray(qd, F32), jnp.asarray(kd, F32), g_l)


def _ret_prompt(rq, rk, rv, rg, batch, seq, dims):
    heads, dk, dv = dims["ret_heads"], dims["ret_dk"], dims["ret_dv"]
    L = RET_CHUNK
    nc = seq // L
    dec, qd, kd, g_l = _ret_consts(heads, L, dk)
    row = lambda c: pl.BlockSpec((L, c), lambda b, i: (b * nc + i, 0))
    kern = functools.partial(_ret_prompt_kernel, heads=heads, dk=dk, dv=dv, g_l=g_l)
    return pl.pallas_call(
        kern,
        grid=(batch, nc),
        in_specs=[row(heads * dk), row(heads * dk), row(heads * dv), row(heads * dv),
                  _resident(dec.shape), _resident(qd.shape), _resident(kd.shape)],
        out_specs=[row(heads * dv), pl.BlockSpec((1, heads, dk, dv), lambda b, i: (b, 0, 0, 0))],
        out_shape=[jax.ShapeDtypeStruct((batch * seq, heads * dv), BF16),
                   jax.ShapeDtypeStruct((batch, heads, dk, dv), F32)],
        scratch_shapes=[pltpu.VMEM((heads, dk, dv), F32)],
        compiler_params=_params(2),
        name="retention_prompt",
    )(rq, rk, rv, rg, dec, qd, kd)


def _ret_sample_kernel(rq, rk, rv, rg, s0, dec, qd, kd, o_ref, s_out, *, group, heads, dk, dv, g_l):
    def body(i, carry):
        for h in range(heads):
            o, s_new = _ret_head(rq[i, :, h * dk:(h + 1) * dk], rk[i, :, h * dk:(h + 1) * dk],
                                 rv[i, :, h * dv:(h + 1) * dv], rg[i, :, h * dv:(h + 1) * dv],
                                 s0[i, h], dec[h], qd[h], kd[h], g_l[h])
            o_ref[i, :, h * dv:(h + 1) * dv] = o
            s_out[i, h] = s_new
        return carry

    lax.fori_loop(0, group, body, 0)


def _ret_sample(rq, rk, rv, rg, state, tokens, dims):
    heads, dk, dv = dims["ret_heads"], dims["ret_dk"], dims["ret_dv"]
    b, tp, _ = rq.shape
    G = RET_SAMPLE_GROUP
    dec, qd, kd, g_l = _ret_consts(heads, tokens, dk, chunk_rows=tp)
    blk = lambda c: pl.BlockSpec((G, tp, c), lambda i: (i, 0, 0))
    st = pl.BlockSpec((G, heads, dk, dv), lambda i: (i, 0, 0, 0))
    kern = functools.partial(_ret_sample_kernel, group=G, heads=heads, dk=dk, dv=dv, g_l=g_l)
    return pl.pallas_call(
        kern,
        grid=(b // G,),
        in_specs=[blk(heads * dk), blk(heads * dk), blk(heads * dv), blk(heads * dv), st,
                  _resident(dec.shape), _resident(qd.shape), _resident(kd.shape)],
        out_specs=[blk(heads * dv), st],
        out_shape=[jax.ShapeDtypeStruct((b, tp, heads * dv), BF16),
                   jax.ShapeDtypeStruct((b, heads, dk, dv), F32)],
        compiler_params=_params(1),
        name="retention_sample",
    )(rq, rk, rv, rg, state, dec, qd, kd)


def _mla_prompt_kernel(q_ref, k_ref, v_ref, o_ref, m_scr, l_scr, acc_scr, *, tq, tk, hp, group, d_v):
    qi = pl.program_id(2)
    row0 = qi * tq
    m_scr[...] = jnp.full(m_scr.shape, NEG_BIG, F32)
    l_scr[...] = jnp.zeros(l_scr.shape, F32)
    acc_scr[...] = jnp.zeros(acc_scr.shape, F32)

    def step(j, carry, masked):
        start = pl.multiple_of(j * tk, tk)
        for h in range(hp):
            s = _dot_nt(q_ref[:, h * group:(h + 1) * group], k_ref[pl.ds(start, tk), h * group:(h + 1) * group])
            if masked:
                r = row0 + lax.broadcasted_iota(jnp.int32, s.shape, 0)
                c = start + lax.broadcasted_iota(jnp.int32, s.shape, 1)
                s = jnp.where(c <= r, s, NEG_BIG)
            m_prev = m_scr[h]
            m_new = jnp.maximum(m_prev, jnp.max(s, axis=-1, keepdims=True))
            a = jnp.exp2(m_prev - m_new)
            p = jnp.exp2(s - jnp.tile(m_new, (1, tk // LANES)))
            l_scr[h] = a * l_scr[h] + jnp.sum(p, axis=-1, keepdims=True)
            acc_scr[h] = a * acc_scr[h] + _dot(p.astype(BF16), v_ref[pl.ds(start, tk), h * d_v:(h + 1) * d_v])
            m_scr[h] = m_new
        return carry

    n_full = row0 // tk
    n_all = (row0 + tq + tk - 1) // tk
    lax.fori_loop(0, n_full, functools.partial(step, masked=False), 0)
    lax.fori_loop(n_full, n_all, functools.partial(step, masked=True), 0)
    for h in range(hp):
        o_ref[:, h * d_v:(h + 1) * d_v] = (acc_scr[h] / l_scr[h]).astype(BF16)


def _mla_prompt(qcat, kcat, v, batch, seq, dims):
    heads, d_nope, d_v = dims["heads"], dims["d_nope"], dims["d_v"]
    group = d_nope + LANES
    t = ATTN_Q_TILE
    hp = ATTN_HEADS_PER_STEP
    nq = seq // t
    kern = functools.partial(_mla_prompt_kernel, tq=t, tk=ATTN_KV_TILE, hp=hp, group=group, d_v=d_v)
    return pl.pallas_call(
        kern,
        grid=(batch, heads // hp, nq),
        in_specs=[pl.BlockSpec((t, hp * group), lambda b, h, i: (b * nq + i, h)),
                  pl.BlockSpec((seq, hp * group), lambda b, h, i: (b, h)),
                  pl.BlockSpec((seq, hp * d_v), lambda b, h, i: (b, h))],
        out_specs=pl.BlockSpec((t, hp * d_v), lambda b, h, i: (b * nq + i, h)),
        out_shape=jax.ShapeDtypeStruct((batch * seq, heads * d_v), BF16),
        scratch_shapes=[pltpu.VMEM((hp, t, LANES), F32), pltpu.VMEM((hp, t, LANES), F32),
                        pltpu.VMEM((hp, t, d_v), F32)],
        compiler_params=_params(3),
        name="mla_prompt",
    )(qcat, kcat, v)


def _mla_sample_kernel(pt_ref, ql_ref, qp_ref, cn_ref, kn_ref, ckv_hbm, kpe_hbm, o_ref,
                       ckv_buf, kpe_buf, kbf, s_scr, sem,
                       *, layer, n_pages, page, chunk, tokens, heads, scale):
    b = pl.program_id(0)
    past = n_pages * page

    def page_copies(seq, slot, p):
        pg = pt_ref[seq, p]
        off = pl.multiple_of(p * page, page)
        return (pltpu.make_async_copy(ckv_hbm.at[layer, pg], ckv_buf.at[slot, pl.ds(off, page), :], sem.at[0, slot]),
                pltpu.make_async_copy(kpe_hbm.at[layer, pg], kpe_buf.at[slot, :, pl.ds(off, page)], sem.at[1, slot]))

    def for_each_page(seq, slot, fn):
        def body(p, carry):
            for cp in page_copies(seq, slot, p):
                fn(cp)
            return carry
        lax.fori_loop(0, n_pages, body, 0, unroll=DMA_LOOP_UNROLL)

    @pl.when(b == 0)
    def _():
        for_each_page(0, 0, lambda cp: cp.start())

    @pl.when(b + 1 < pl.num_programs(0))
    def _():
        for_each_page(b + 1, (b + 1) % 2, lambda cp: cp.start())

    slot = b % 2
    for_each_page(b, slot, lambda cp: cp.wait())

    ql = ql_ref[0]
    qp = qp_ref[0]
    ckv_s = ckv_buf.at[slot]
    kpe_s = kpe_buf.at[slot]
    n_chunks = past // chunk

    for c in range(n_chunks):
        sl = slice(c * chunk, (c + 1) * chunk)
        kc = ckv_s[sl, :].astype(BF16)
        kbf[sl, :] = kc
        s_scr[:, sl] = (_dot_nt(ql, kc) + _dot(qp, kpe_s[:, sl].astype(BF16))) * scale

    cn = cn_ref[0].astype(BF16)
    kn = kn_ref[0].astype(BF16)
    s_new = (_dot_nt(ql, cn) + _dot_nt(qp, kn)) * scale
    t = lax.broadcasted_iota(jnp.int32, s_new.shape, 0) // heads
    col = lax.broadcasted_iota(jnp.int32, s_new.shape, 1)
    s_new = jnp.where((col <= t) & (col < tokens), s_new, NEG_BIG)

    mx = s_scr[:, 0:chunk]
    for c in range(1, n_chunks):
        mx = jnp.maximum(mx, s_scr[:, c * chunk:(c + 1) * chunk])
    m = jnp.maximum(jnp.max(mx, axis=-1, keepdims=True), jnp.max(s_new, axis=-1, keepdims=True))

    p_new = jnp.exp(s_new - m)
    acc = _dot(p_new.astype(BF16), cn)
    lsum = jnp.zeros((s_scr.shape[0], chunk), F32)
    for c in range(n_chunks):
        sl = slice(c * chunk, (c + 1) * chunk)
        p = jnp.exp(s_scr[:, sl] - m)
        lsum = lsum + p
        acc = acc + _dot(p.astype(BF16), kbf[sl, :])
    l = jnp.sum(lsum, axis=-1, keepdims=True) + jnp.sum(p_new, axis=-1, keepdims=True)
    o_ref[0] = acc / l


def _mla_sample(ql, qp, ckv_new, kpe_new, cache_ckv, cache_kpe_t, layer, page_table, tokens, dims):
    heads, d_nope, d_rope, kv_lora = dims["heads"], dims["d_nope"], dims["d_rope"], dims["kv_lora"]
    b, rows, _ = ql.shape
    n_pages = page_table.shape[1]
    page = cache_ckv.shape[2]
    past = n_pages * page
    tp = ckv_new.shape[1]
    per_b = lambda r, c: pl.BlockSpec((1, r, c), lambda i, pt: (i, 0, 0))
    hbm = pl.BlockSpec(memory_space=pl.ANY)
    kern = functools.partial(_mla_sample_kernel, layer=layer, n_pages=n_pages, page=page, chunk=DECODE_CHUNK,
                             tokens=tokens, heads=heads, scale=(d_nope + d_rope) ** -0.5)
    grid_spec = pltpu.PrefetchScalarGridSpec(
        num_scalar_prefetch=1,
        grid=(b,),
        in_specs=[per_b(rows, kv_lora), per_b(rows, d_rope), per_b(tp, kv_lora), per_b(tp, d_rope), hbm, hbm],
        out_specs=per_b(rows, kv_lora),
        scratch_shapes=[pltpu.VMEM((2, past, kv_lora), F32), pltpu.VMEM((2, d_rope, past), F32),
                        pltpu.VMEM((past, kv_lora), BF16), pltpu.VMEM((rows, past), F32),
                        pltpu.SemaphoreType.DMA((2, 2))],
    )
    return pl.pallas_call(
        kern,
        grid_spec=grid_spec,
        out_shape=jax.ShapeDtypeStruct((b, rows, kv_lora), F32),
        compiler_params=_params(1),
        name="mla_sample",
    )(page_table, ql, qp, ckv_new, kpe_new, cache_ckv, cache_kpe_t)


def _mem_kv_kernel(m_ref, g_ref, wk, wv, k_o, v_o):
    mn = _rms(m_ref[0], g_ref[...]).astype(BF16)
    k_o[0] = _dot(mn, wk[...])
    v_o[0] = _dot(mn, wv[...])


def _mem_kv(mem, g, wk, wv):
    b, m, d = mem.shape
    c = wk.shape[1]
    return pl.pallas_call(
        _mem_kv_kernel,
        grid=(b,),
        in_specs=[pl.BlockSpec((1, m, d), lambda i: (i, 0, 0)), _resident(g.shape), _resident(wk.shape),
                  _resident(wv.shape)],
        out_specs=[pl.BlockSpec((1, m, c), lambda i: (i, 0, 0))] * 2,
        out_shape=[jax.ShapeDtypeStruct((b, m, c), F32)] * 2,
        compiler_params=_params(1),
        name="mem_kv",
    )(mem, g, wk, wv)


def _x_attend_heads(q, mk, mv, heads, hd):
    lane = lax.broadcasted_iota(jnp.int32, q.shape, 1)
    out = jnp.zeros(q.shape, F32)
    for h in range(heads):
        sel = (lane >= h * hd) & (lane < (h + 1) * hd)
        s = _dot_nt(jnp.where(sel, q, jnp.zeros_like(q)), mk) * (hd ** -0.5)
        p = jnp.exp(s - jnp.max(s, axis=-1, keepdims=True))
        p = p / jnp.sum(p, axis=-1, keepdims=True)
        out = out + jnp.where(sel, _dot(p.astype(BF16), mv), 0.0)
    return out


def _x_sample_kernel(q_ref, kt_ref, vt_ref, o_ref, *, group, heads, hd):
    tp = q_ref.shape[1]
    lane = lax.broadcasted_iota(jnp.int32, q_ref.shape[1:], 1)
    sels = [(lane >= h * hd) & (lane < (h + 1) * hd) for h in range(heads)]
    for g in range(group):
        q = q_ref[g]
        q_heads = jnp.concatenate([jnp.where(sel, q, 0.0) for sel in sels], axis=0).astype(BF16)
        s = _dot(q_heads, kt_ref[g].astype(BF16)) * (hd ** -0.5)
        p = jnp.exp(s - jnp.max(s, axis=-1, keepdims=True))
        p = p / jnp.sum(p, axis=-1, keepdims=True)
        r = _dot_nt(p.astype(BF16), vt_ref[g].astype(BF16))
        out = jnp.zeros(q.shape, F32)
        for h, sel in enumerate(sels):
            out = out + jnp.where(sel, r[h * tp:(h + 1) * tp], 0.0)
        o_ref[g] = out


def _x_sample(xq, mkt, mvt, heads):
    b, tp, c = xq.shape
    m = mkt.shape[2]
    G = X_SAMPLE_GROUP
    kern = functools.partial(_x_sample_kernel, group=G, heads=heads, hd=c // heads)
    return pl.pallas_call(
        kern,
        grid=(b // G,),
        in_specs=[pl.BlockSpec((G, tp, c), lambda i: (i, 0, 0)),
                  pl.BlockSpec((G, c, m), lambda i: (i, 0, 0)),
                  pl.BlockSpec((G, c, m), lambda i: (i, 0, 0))],
        out_specs=pl.BlockSpec((G, tp, c), lambda i: (i, 0, 0)),
        out_shape=jax.ShapeDtypeStruct((b, tp, c), F32),
        compiler_params=_params(1),
        name="x_attend_sample",
    )(xq, mkt, mvt)


def _merge_kernel(x_ref, gpre, wgate, oret, wret, omla, wmla, wuv, ox, wx, wout, gpost, *rest,
                  heads, kv_lora, d_v, absorbed, x_heads):
    h_o = rest[-1]
    if x_heads:
        mk, mv = rest[:2]
        o_x = _x_attend_heads(ox[...], mk[0].astype(BF16), mv[0].astype(BF16), x_heads, ox.shape[1] // x_heads)
    else:
        o_x = ox[...]
    x = x_ref[...]
    d = x.shape[1]
    u = _rms(x, gpre[...]).astype(BF16)
    a_ret = _dot(oret[...], wret[...])
    if absorbed:
        a_mla = jnp.zeros(x.shape, F32)
        for h in range(heads):
            o_h = _dot(omla[:, h * kv_lora:(h + 1) * kv_lora].astype(BF16), wuv[h]).astype(BF16)
            a_mla = a_mla + _dot(o_h, wmla[h * d_v:(h + 1) * d_v, :])
    else:
        a_mla = _dot(omla[...], wmla[...])
    a_x = _dot(o_x.astype(BF16), wx[...])
    mixed = jnp.zeros(x.shape, F32)
    for i, a in enumerate((a_ret, a_mla, a_x)):
        mixed = mixed + _sigmoid(_dot(u, wgate[:, i * d:(i + 1) * d])) * a
    h_o[...] = x + _rms(_dot(mixed.astype(BF16), wout[...]), gpost[...])


def _merge(x, oret, omla, ox, w, dims, *, absorbed, tm, mem=None):
    n, d = x.shape
    row = lambda a: pl.BlockSpec((tm, a.shape[1]), lambda i: (i, 0))
    res = lambda a: _resident(a.shape)
    args = [x, w["g_pre"], w["w_gates"], oret, w["w_ret_o"], omla, w["w_mla_o"], w["w_uv_h"], ox, w["w_x_o"],
            w["w_out"], w["g_post"]]
    specs = [row(x), res(w["g_pre"]), res(w["w_gates"]), row(oret), res(w["w_ret_o"]), row(omla),
             res(w["w_mla_o"]), res(w["w_uv_h"]), row(ox), res(w["w_x_o"]), res(w["w_out"]), res(w["g_post"])]
    x_heads = 0
    if mem is not None:
        mk, mv, x_heads, seq_rows = mem
        tiles_per_seq = seq_rows // tm
        per_seq = pl.BlockSpec((1,) + mk.shape[1:], lambda i: (i // tiles_per_seq, 0, 0))
        args += [mk, mv]
        specs += [per_seq, per_seq]
    kern = functools.partial(_merge_kernel, heads=dims["heads"], kv_lora=dims["kv_lora"], d_v=dims["d_v"],
                             absorbed=absorbed, x_heads=x_heads)
    return pl.pallas_call(
        kern,
        grid=(n // tm,),
        in_specs=specs,
        out_specs=pl.BlockSpec((tm, d), lambda i: (i, 0)),
        out_shape=jax.ShapeDtypeStruct((n, d), F32),
        compiler_params=_params(1),
        name="merge_sample" if absorbed else "merge_prompt",
    )(*args)


def _ffn_kernel(h_ref, gpre, wg, wu, wd, gpost, y_o):
    h = h_ref[...]
    f = _rms(h, gpre[...]).astype(BF16)
    gate = _dot(f, wg[...])
    act = (gate * _sigmoid(gate) * _dot(f, wu[...])).astype(BF16)
    y_o[...] = h + _rms(_dot(act, wd[...]), gpost[...])


def _ffn(h, w, *, tm):
    n, d = h.shape
    res = lambda a: _resident(a.shape)
    return pl.pallas_call(
        _ffn_kernel,
        grid=(n // tm,),
        in_specs=[pl.BlockSpec((tm, d), lambda i: (i, 0)), res(w["g_ffn_pre"]), res(w["w_ffn_gate"]),
                  res(w["w_ffn_up"]), res(w["w_ffn_down"]), res(w["g_ffn_post"])],
        out_specs=pl.BlockSpec((tm, d), lambda i: (i, 0)),
        out_shape=jax.ShapeDtypeStruct((n, d), F32),
        compiler_params=_params(1),
        name="ffn",
    )(h, w["g_ffn_pre"], w["w_ffn_gate"], w["w_ffn_up"], w["w_ffn_down"], w["g_ffn_post"])


def _rope_tables(pos, ret_dk, d_rope):
    posf = pos.astype(F32)[:, None]

    def angles(half):
        inv = ROPE_BASE ** (-jnp.arange(half, dtype=F32) / half)
        ang = posf * inv[None, :]
        return jnp.cos(ang), jnp.sin(ang)

    cr, sr = angles(ret_dk // 2)
    cp, sp = angles(d_rope // 2)
    z = jnp.zeros_like(cp)
    pad = jnp.zeros((pos.shape[0], LANES - d_rope), F32)
    return {
        "cosr": jnp.concatenate([cr, cr], axis=1),
        "sinr": jnp.concatenate([-sr, sr], axis=1),
        "cosp": jnp.concatenate([cp, cp, pad], axis=1),
        "sinlo": jnp.concatenate([-sp, z, pad], axis=1),
        "sinhi": jnp.concatenate([z, sp, pad], axis=1),
    }


def _layer_weights(l, dims, sizes, norm_mix_pre, norm_mix_post, norm_ffn_pre, norm_ffn_post, norm_mem, norm_q_lat,
                   norm_kv_lat, w_in, w_uq, w_uk, w_uv, w_mem_k, w_mem_v, w_ret_o, w_mla_o, w_x_o, w_out,
                   w_ffn_gate, w_ffn_up, w_ffn_down):
    heads, d_nope, d_rope, kv_lora, d_v = (dims[k] for k in ("heads", "d_nope", "d_rope", "kv_lora", "d_v"))
    bf = lambda a: a.astype(BF16)
    gain = lambda a: a[l].astype(F32)[None, :]
    offs = np.concatenate([[0], np.cumsum(sizes)])
    seg = [w_in[l][:, offs[i]:offs[i + 1]] for i in range(len(sizes))]
    q_lora = w_uq.shape[1]
    uq = w_uq[l].reshape(q_lora, heads, d_nope + d_rope)
    uq_p = jnp.pad(uq[:, :, d_nope:], ((0, 0), (0, 0), (0, LANES - d_rope)))
    return {
        "g_pre": gain(norm_mix_pre), "g_post": gain(norm_mix_post), "g_ffn_pre": gain(norm_ffn_pre),
        "g_ffn_post": gain(norm_ffn_post), "g_mem": gain(norm_mem), "g_q": gain(norm_q_lat),
        "g_kv": gain(norm_kv_lat),
        "w_rq": bf(seg[0]), "w_rk": bf(seg[1]), "w_rv": bf(seg[2]), "w_rg": bf(seg[3]), "w_cq": bf(seg[4]),
        "w_ckv": bf(seg[5]), "w_kpe": bf(jnp.pad(seg[6], ((0, 0), (0, LANES - d_rope)))), "w_xq": bf(seg[7]),
        "w_gates": bf(seg[8]),
        "w_uq_n": bf(uq[:, :, :d_nope].reshape(q_lora, heads * d_nope)),
        "w_uq_p": bf(uq_p.reshape(q_lora, heads * LANES)),
        "w_uk_t": bf(jnp.swapaxes(w_uk[l], 1, 2)),
        "w_kn": bf(jnp.swapaxes(w_uk[l], 0, 1).reshape(kv_lora, heads * d_nope)),
        "w_vn": bf(jnp.swapaxes(w_uv[l], 0, 1).reshape(kv_lora, heads * d_v)),
        "w_uv_h": bf(w_uv[l]),
        "w_mem_k": bf(w_mem_k[l]), "w_mem_v": bf(w_mem_v[l]),
        "w_ret_o": bf(w_ret_o[l]), "w_mla_o": bf(w_mla_o[l]), "w_x_o": bf(w_x_o[l]), "w_out": bf(w_out[l]),
        "w_ffn_gate": bf(w_ffn_gate[l]), "w_ffn_up": bf(w_ffn_up[l]), "w_ffn_down": bf(w_ffn_down[l]),
    }


def _pad_tokens(a, b, tokens, rows=SAMPLE_TOK_PAD):
    a = a.reshape(b, tokens, a.shape[-1])
    return jnp.pad(a, ((0, 0), (0, rows - tokens), (0, 0)))


def kernel(x_prompt, x_sample, mem_prompt, cache_ckv, cache_kpe, page_table, state_ret, cache_mem_k, cache_mem_v,
           norm_mix_pre, norm_mix_post, norm_ffn_pre, norm_ffn_post, norm_mem, norm_q_lat, norm_kv_lat, w_in,
           w_uq, w_uk, w_uv, w_mem_k, w_mem_v, w_ret_o, w_mla_o, w_x_o, w_out, w_ffn_gate, w_ffn_up, w_ffn_down):
    depth = w_in.shape[0]
    batch, seq, d_model = x_prompt.shape
    db, tokens, _ = x_sample.shape
    ret_heads, ret_dk, ret_dv = state_ret.shape[2:]
    heads, kv_lora, d_nope = w_uk.shape[1:]
    d_rope = cache_kpe.shape[-1]
    d_v = w_uv.shape[-1]
    n_mem, x_heads, x_hd = cache_mem_k.shape[2:]
    q_lora = w_uq.shape[1]
    past_len = page_table.shape[1] * cache_ckv.shape[2]
    dims = dict(heads=heads, d_nope=d_nope, d_rope=d_rope, kv_lora=kv_lora, d_v=d_v,
                ret_heads=ret_heads, ret_dk=ret_dk, ret_dv=ret_dv)
    assert ret_dk == LANES and d_nope == LANES and d_rope <= LANES // 2 and tokens <= SAMPLE_TOK_PAD
    sizes = (ret_heads * ret_dk, ret_heads * ret_dk, ret_heads * ret_dv, ret_heads * ret_dv, q_lora, kv_lora,
             d_rope, x_heads * x_hd, w_in.shape[2] - (2 * ret_heads * ret_dk + 2 * ret_heads * ret_dv + q_lora
                                                       + kv_lora + d_rope + x_heads * x_hd))

    cache_kpe_t = jnp.swapaxes(cache_kpe, 2, 3)

    tabs_p = _rope_tables(jnp.arange(seq), ret_dk, d_rope)
    n_s = db * tokens
    tabs_s = _rope_tables(past_len + (jnp.arange(n_s) % tokens), ret_dk, d_rope)

    y_p = x_prompt.reshape(batch * seq, d_model)
    y_s = x_sample.reshape(n_s, d_model)
    outs = [[] for _ in range(8)]
    for l in range(depth):
        w = _layer_weights(l, dims, sizes, norm_mix_pre, norm_mix_post, norm_ffn_pre, norm_ffn_post, norm_mem,
                           norm_q_lat, norm_kv_lat, w_in, w_uq, w_uk, w_uv, w_mem_k, w_mem_v, w_ret_o, w_mla_o,
                           w_x_o, w_out, w_ffn_gate, w_ffn_up, w_ffn_down)

        mk_p, mv_p = _mem_kv(mem_prompt, w["g_mem"], w["w_mem_k"], w["w_mem_v"])
        rq, rk, rv, rg, qcat, xq, ckv_p, kpe_p, kcat, vn = _in_proj(y_p, w, tabs_p, dims, absorbed=False,
                                                                    tm=ROW_TILE)
        o_ret, ret_p = _ret_prompt(rq, rk, rv, rg, batch, seq, dims)
        o_mla = _mla_prompt(qcat, kcat, vn, batch, seq, dims)
        h_p = _merge(y_p, o_ret, o_mla, xq, w, dims, absorbed=False, tm=ROW_TILE, mem=(mk_p, mv_p, x_heads, seq))
        y_p = _ffn(h_p, w, tm=ROW_TILE)

        rq, rk, rv, rg, qcat, xq, ckv_s, kpe_s, qlat = _in_proj(y_s, w, tabs_s, dims, absorbed=True, tm=ROW_TILE)
        pad = lambda a: _pad_tokens(a, db, tokens)
        o_ret, ret_s = _ret_sample(pad(rq), pad(rk), pad(rv), pad(rg), state_ret[l].astype(F32), tokens, dims)
        o_ret = o_ret[:, :tokens].reshape(n_s, ret_heads * ret_dv)
        group = d_nope + LANES
        q_pe = qcat.reshape(db, tokens * heads, group)[:, :, d_nope:d_nope + d_rope]
        o_lat = _mla_sample(qlat.reshape(db, tokens * heads, kv_lora), q_pe, pad(ckv_s), pad(kpe_s),
                            cache_ckv, cache_kpe_t, l, page_table, tokens, dims)
        o_lat = o_lat.reshape(n_s, heads * kv_lora)
        mem_t = lambda c: jnp.transpose(c[l], (0, 2, 3, 1)).reshape(db, x_heads * x_hd, n_mem)
        o_x = _x_sample(_pad_tokens(xq.astype(F32), db, tokens, rows=X_SAMPLE_TOK_PAD), mem_t(cache_mem_k),
                        mem_t(cache_mem_v), x_heads)
        o_x = o_x[:, :tokens].reshape(n_s, x_heads * x_hd)
        h_s = _merge(y_s, o_ret, o_lat, o_x, w, dims, absorbed=True, tm=ROW_TILE)
        y_s = _ffn(h_s, w, tm=ROW_TILE)

        for lst, val in zip(outs, (ckv_p.reshape(batch, seq, kv_lora), kpe_p.reshape(batch, seq, d_rope),
                                   ckv_s.reshape(db, tokens, kv_lora), kpe_s.reshape(db, tokens, d_rope),
                                   ret_p.astype(x_prompt.dtype), ret_s.astype(state_ret.dtype),
                                   mk_p.reshape(batch, n_mem, x_heads, x_hd),
                                   mv_p.reshape(batch, n_mem, x_heads, x_hd))):
            lst.append(val)

    return (y_p.reshape(batch, seq, d_model), y_s.reshape(db, tokens, d_model)) + tuple(jnp.stack(o) for o in outs)
```

```python
import functools
import math

import numpy as np
import jax
import jax.numpy as jnp
from jax import lax
from jax.experimental import pallas as pl
from jax.experimental.pallas import tpu as pltpu

F32 = jnp.float32
BF16 = jnp.bfloat16

ROPE_BASE = 10000.0
RMS_EPS = 1e-6
LANES = 128
VMEM_LIMIT = 56 * 1024 * 1024
NEG_BIG = -1e30

ROW_TILE = 512
RET_CHUNK = 256
ATTN_Q_TILE = 512
ATTN_KV_TILE = 512
ATTN_HEADS_PER_STEP = 4
SAMPLE_TOK_PAD = 16
X_SAMPLE_TOK_PAD = 8
RET_SAMPLE_GROUP = 8
X_SAMPLE_GROUP = 8
DECODE_CHUNK = 4096
DMA_LOOP_UNROLL = 4


def _resident(shape):
    nd = len(shape)
    return pl.BlockSpec(shape, lambda *_: (0,) * nd, pipeline_mode=pl.Buffered(1))


def _params(n_axes):
    return pltpu.CompilerParams(dimension_semantics=("arbitrary",) * n_axes, vmem_limit_bytes=VMEM_LIMIT)


def _rms(x, g=None):
    y = x * lax.rsqrt(jnp.mean(x * x, axis=-1, keepdims=True) + RMS_EPS)
    return y if g is None else y * g


def _sigmoid(x):
    return 1.0 / (1.0 + jnp.exp(-x))


def _dot(a, b):
    return jnp.dot(a, b, preferred_element_type=F32)


def _dot_nt(a, b):
    return lax.dot_general(a, b, (((1,), (1,)), ((), ())), preferred_element_type=F32)


def _dot_tn(a, b):
    return lax.dot_general(a, b, (((0,), (0,)), ((), ())), preferred_element_type=F32)


def _rope_half_vreg(z, cos_t, sin_lo, sin_hi, quarter):
    return (z * cos_t + pltpu.roll(z, LANES - quarter, 1) * sin_lo + pltpu.roll(z, quarter, 1) * sin_hi)


def _in_proj_kernel(x_ref, g_ref, wrq, wrk, wrv, wrg, wcq, wckv, wkpe, wxq, gq_ref, gkv_ref, wuqn, wuqp,
                    cosr, sinr, cosp, sinlo, sinhi, *rest,
                    ret_heads, ret_dk, heads, d_nope, d_rope, kv_lora, absorbed, q_scale):
    if absorbed:
        wa, (rq_o, rk_o, rv_o, rg_o, qcat_o, xq_o, ckv_o, kpe_o, a_o) = rest[0], rest[1:]
    else:
        wa, wb, (rq_o, rk_o, rv_o, rg_o, qcat_o, xq_o, ckv_o, kpe_o, a_o, b_o) = rest[0], rest[1], rest[2:]
    u = _rms(x_ref[...], g_ref[...]).astype(BF16)

    cr, sr = cosr[...], sinr[...]
    zq = _dot(u, wrq[...])
    zk = _dot(u, wrk[...])
    k_scale = ret_dk ** -0.5
    for h in range(ret_heads):
        sl = slice(h * ret_dk, (h + 1) * ret_dk)
        q_h = zq[:, sl]
        k_h = zk[:, sl]
        rq_o[:, sl] = (q_h * cr + pltpu.roll(q_h, ret_dk // 2, 1) * sr).astype(BF16)
        rk_o[:, sl] = ((k_h * cr + pltpu.roll(k_h, ret_dk // 2, 1) * sr) * k_scale).astype(BF16)
    rv_o[...] = _dot(u, wrv[...]).astype(BF16)
    rg_o[...] = _dot(u, wrg[...]).astype(BF16)

    cp, slo, shi = cosp[...], sinlo[...], sinhi[...]
    cqn = _rms(_dot(u, wcq[...]), gq_ref[...]).astype(BF16)
    qn = _dot(cqn, wuqn[...])
    qp = _dot(cqn, wuqp[...])
    group = d_nope + LANES
    for h in range(heads):
        qn_h = qn[:, h * d_nope:(h + 1) * d_nope]
        qp_h = _rope_half_vreg(qp[:, h * LANES:(h + 1) * LANES], cp, slo, shi, d_rope // 2)
        if absorbed:
            a_o[:, h * kv_lora:(h + 1) * kv_lora] = _dot(qn_h.astype(BF16), wa[h]).astype(BF16)
        else:
            qn_h = qn_h * q_scale
            qp_h = qp_h * q_scale
        qcat_o[:, h * group:h * group + d_nope] = qn_h.astype(BF16)
        qcat_o[:, h * group + d_nope:(h + 1) * group] = qp_h.astype(BF16)

    ckvn = _rms(_dot(u, wckv[...]), gkv_ref[...])
    ckv_o[...] = ckvn
    kpe = _rope_half_vreg(_dot(u, wkpe[...]), cp, slo, shi, d_rope // 2)
    kpe_o[...] = kpe[:, :d_rope]
    if not absorbed:
        ckvb = ckvn.astype(BF16)
        kn = _dot(ckvb, wa[...])
        kpeb = kpe.astype(BF16)
        for h in range(heads):
            a_o[:, h * group:h * group + d_nope] = kn[:, h * d_nope:(h + 1) * d_nope].astype(BF16)
            a_o[:, h * group + d_nope:(h + 1) * group] = kpeb
        b_o[...] = _dot(ckvb, wb[...]).astype(BF16)

    xq_o[...] = _dot(u, wxq[...]).astype(BF16)


def _in_proj(x, w, tabs, dims, *, absorbed, tm):
    n, d_model = x.shape
    heads, d_nope, d_rope, kv_lora = dims["heads"], dims["d_nope"], dims["d_rope"], dims["kv_lora"]
    ret_heads, ret_dk, ret_dv = dims["ret_heads"], dims["ret_dk"], dims["ret_dv"]
    group = d_nope + LANES
    tab_rows = tabs["cosr"].shape[0]
    tab_tiles = tab_rows // tm
    row = lambda c: pl.BlockSpec((tm, c), lambda i: (i, 0))
    tab = lambda: pl.BlockSpec((tm, LANES), lambda i: (i % tab_tiles, 0))
    weights = [w["g_pre"], w["w_rq"], w["w_rk"], w["w_rv"], w["w_rg"], w["w_cq"], w["w_ckv"], w["w_kpe"],
               w["w_xq"], w["g_q"], w["g_kv"], w["w_uq_n"], w["w_uq_p"]]
    out_cols = [(ret_heads * ret_dk, BF16), (ret_heads * ret_dk, BF16), (ret_heads * ret_dv, BF16),
                (ret_heads * ret_dv, BF16), (heads * group, BF16), (w["w_xq"].shape[1], BF16), (kv_lora, F32),
                (d_rope, F32)]
    if absorbed:
        mode_weights = [w["w_uk_t"]]
        out_cols += [(heads * kv_lora, BF16)]
    else:
        mode_weights = [w["w_kn"], w["w_vn"]]
        out_cols += [(heads * group, BF16), (w["w_vn"].shape[1], BF16)]
    kern = functools.partial(_in_proj_kernel, ret_heads=ret_heads, ret_dk=ret_dk, heads=heads, d_nope=d_nope,
                             d_rope=d_rope, kv_lora=kv_lora, absorbed=absorbed,
                             q_scale=(d_nope + d_rope) ** -0.5 * math.log2(math.e))
    return pl.pallas_call(
        kern,
        grid=(n // tm,),
        in_specs=([row(d_model)] + [_resident(a.shape) for a in weights] + [tab() for _ in range(5)]
                  + [_resident(a.shape) for a in mode_weights]),
        out_specs=[row(c) for c, _ in out_cols],
        out_shape=[jax.ShapeDtypeStruct((n, c), dt) for c, dt in out_cols],
        compiler_params=_params(1),
        name="in_proj_sample" if absorbed else "in_proj_prompt",
    )(x, *weights, tabs["cosr"], tabs["sinr"], tabs["cosp"], tabs["sinlo"], tabs["sinhi"], *mode_weights)


def _ret_head(q, k, v, g, s, dec, qd, kd, g_l):
    inner = _dot_nt(q, k) * dec
    o = _dot(inner.astype(BF16), v) + _dot((q.astype(F32) * qd).astype(BF16), s.astype(BF16))
    s_new = s * g_l + _dot_tn((k.astype(F32) * kd).astype(BF16), v)
    gf = g.astype(F32)
    o = (gf * _sigmoid(gf)) * _rms(o)
    return o.astype(BF16), s_new


def _ret_prompt_kernel(rq, rk, rv, rg, dec, qd, kd, o_ref, s_out, s_scr, *, heads, dk, dv, g_l):
    c = pl.program_id(1)

    @pl.when(c == 0)
    def _():
        s_scr[...] = jnp.zeros(s_scr.shape, F32)

    for h in range(heads):
        o, s_new = _ret_head(rq[:, h * dk:(h + 1) * dk], rk[:, h * dk:(h + 1) * dk],
                             rv[:, h * dv:(h + 1) * dv], rg[:, h * dv:(h + 1) * dv],
                             s_scr[h], dec[h], qd[h], kd[h], g_l[h])
        o_ref[:, h * dv:(h + 1) * dv] = o
        s_scr[h] = s_new

    @pl.when(c == pl.num_programs(1) - 1)
    def _():
        s_out[0] = s_scr[...]


def _ret_consts(heads, length, dk, chunk_rows=None):
    rows = length if chunk_rows is None else chunk_rows
    lg = np.log1p(-np.exp2(-5.0 - np.arange(heads, dtype=np.float64)))
    i = np.arange(rows, dtype=np.float64)
    diff = i[:, None] - i[None, :]
    valid = (diff >= 0) & (i[:, None] < length) & (i[None, :] < length)
    dec = np.where(valid[None], np.exp(np.maximum(diff, 0.0)[None] * lg[:, None, None]), 0.0)
    qd = np.exp((i[None, :] + 1.0) * lg[:, None])
    kd = np.where(i[None, :] < length, np.exp((length - 1.0 - i[None, :]) * lg[:, None]), 0.0)
    qd = np.broadcast_to(qd[:, :, None], (heads, rows, dk))
    kd = np.broadcast_to(kd[:, :, None], (heads, rows, dk))
    g_l = tuple(float(v) for v in np.exp(length * lg))
    return (jnp.asarray(dec, F32), jnp.asarray(qd, F32), jnp.asarray(kd, F32), g_l)


def _ret_prompt(rq, rk, rv, rg, batch, seq, dims):
    heads, dk, dv = dims["ret_heads"], dims["ret_dk"], dims["ret_dv"]
    L = RET_CHUNK
    nc = seq // L
    dec, qd, kd, g_l = _ret_consts(heads, L, dk)
    row = lambda c: pl.BlockSpec((L, c), lambda b, i: (b * nc + i, 0))
    kern = functools.partial(_ret_prompt_kernel, heads=heads, dk=dk, dv=dv, g_l=g_l)
    return pl.pallas_call(
        kern,
        grid=(batch, nc),
        in_specs=[row(heads * dk), row(heads * dk), row(heads * dv), row(heads * dv),
                  _resident(dec.shape), _resident(qd.shape), _resident(kd.shape)],
        out_specs=[row(heads * dv), pl.BlockSpec((1, heads, dk, dv), lambda b, i: (b, 0, 0, 0))],
        out_shape=[jax.ShapeDtypeStruct((batch * seq, heads * dv), BF16),
                   jax.ShapeDtypeStruct((batch, heads, dk, dv), F32)],
        scratch_shapes=[pltpu.VMEM((heads, dk, dv), F32)],
        compiler_params=_params(2),
        name="retention_prompt",
    )(rq, rk, rv, rg, dec, qd, kd)


def _ret_sample_kernel(rq, rk, rv, rg, s0, dec, qd, kd, o_ref, s_out, *, group, heads, dk, dv, g_l):
    for i in range(group):
        for h in range(heads):
            o, s_new = _ret_head(rq[i, :, h * dk:(h + 1) * dk], rk[i, :, h * dk:(h + 1) * dk],
                                 rv[i, :, h * dv:(h + 1) * dv], rg[i, :, h * dv:(h + 1) * dv],
                                 s0[i, h], dec[h], qd[h], kd[h], g_l[h])
            o_ref[i, :, h * dv:(h + 1) * dv] = o
            s_out[i, h] = s_new


def _ret_sample(rq, rk, rv, rg, state, tokens, dims):
    heads, dk, dv = dims["ret_heads"], dims["ret_dk"], dims["ret_dv"]
    b, tp, _ = rq.shape
    G = RET_SAMPLE_GROUP
    dec, qd, kd, g_l = _ret_consts(heads, tokens, dk, chunk_rows=tp)
    blk = lambda c: pl.BlockSpec((G, tp, c), lambda i: (i, 0, 0))
    st = pl.BlockSpec((G, heads, dk, dv), lambda i: (i, 0, 0, 0))
    kern = functools.partial(_ret_sample_kernel, group=G, heads=heads, dk=dk, dv=dv, g_l=g_l)
    return pl.pallas_call(
        kern,
        grid=(b // G,),
        in_specs=[blk(heads * dk), blk(heads * dk), blk(heads * dv), blk(heads * dv), st,
                  _resident(dec.shape), _resident(qd.shape), _resident(kd.shape)],
        out_specs=[blk(heads * dv), st],
        out_shape=[jax.ShapeDtypeStruct((b, tp, heads * dv), BF16),
                   jax.ShapeDtypeStruct((b, heads, dk, dv), F32)],
        compiler_params=_params(1),
        name="retention_sample",
    )(rq, rk, rv, rg, state, dec, qd, kd)


def _mla_prompt_kernel(q_ref, k_ref, v_ref, o_ref, m_scr, l_scr, acc_scr, *, tq, tk, hp, group, d_v):
    qi = pl.program_id(2)
    row0 = qi * tq
    m_scr[...] = jnp.full(m_scr.shape, NEG_BIG, F32)
    l_scr[...] = jnp.zeros(l_scr.shape, F32)
    acc_scr[...] = jnp.zeros(acc_scr.shape, F32)

    def step(j, carry, masked):
        start = pl.multiple_of(j * tk, tk)
        for h in range(hp):
            s = _dot_nt(q_ref[:, h * group:(h + 1) * group], k_ref[pl.ds(start, tk), h * group:(h + 1) * group])
            if masked:
                r = row0 + lax.broadcasted_iota(jnp.int32, s.shape, 0)
                c = start + lax.broadcasted_iota(jnp.int32, s.shape, 1)
                s = jnp.where(c <= r, s, NEG_BIG)
            m_prev = m_scr[h]
            m_new = jnp.maximum(m_prev, jnp.max(s, axis=-1, keepdims=True))
            a = jnp.exp2(m_prev - m_new)
            p = jnp.exp2(s - jnp.tile(m_new, (1, tk // LANES)))
            l_scr[h] = a * l_scr[h] + jnp.sum(p, axis=-1, keepdims=True)
            acc_scr[h] = a * acc_scr[h] + _dot(p.astype(BF16), v_ref[pl.ds(start, tk), h * d_v:(h + 1) * d_v])
            m_scr[h] = m_new
        return carry

    n_full = row0 // tk
    n_all = (row0 + tq + tk - 1) // tk
    lax.fori_loop(0, n_full, functools.partial(step, masked=False), 0)
    lax.fori_loop(n_full, n_all, functools.partial(step, masked=True), 0)
    for h in range(hp):
        o_ref[:, h * d_v:(h + 1) * d_v] = (acc_scr[h] / l_scr[h]).astype(BF16)


def _mla_prompt(qcat, kcat, v, batch, seq, dims):
    heads, d_nope, d_v = dims["heads"], dims["d_nope"], dims["d_v"]
    group = d_nope + LANES
    t = ATTN_Q_TILE
    hp = ATTN_HEADS_PER_STEP
    nq = seq // t
    kern = functools.partial(_mla_prompt_kernel, tq=t, tk=ATTN_KV_TILE, hp=hp, group=group, d_v=d_v)
    return pl.pallas_call(
        kern,
        grid=(batch, heads // hp, nq),
        in_specs=[pl.BlockSpec((t, hp * group), lambda b, h, i: (b * nq + i, h)),
                  pl.BlockSpec((seq, hp * group), lambda b, h, i: (b, h)),
                  pl.BlockSpec((seq, hp * d_v), lambda b, h, i: (b, h))],
        out_specs=pl.BlockSpec((t, hp * d_v), lambda b, h, i: (b * nq + i, h)),
        out_shape=jax.ShapeDtypeStruct((batch * seq, heads * d_v), BF16),
        scratch_shapes=[pltpu.VMEM((hp, t, LANES), F32), pltpu.VMEM((hp, t, LANES), F32),
                        pltpu.VMEM((hp, t, d_v), F32)],
        compiler_params=_params(3),
        name="mla_prompt",
    )(qcat, kcat, v)


def _mla_sample_kernel(pt_ref, ql_ref, qp_ref, cn_ref, kn_ref, ckv_hbm, kpe_hbm, o_ref,
                       ckv_buf, kpe_buf, kbf, s_scr, sem,
                       *, layer, n_pages, page, chunk, tokens, heads, scale):
    b = pl.program_id(0)
    past = n_pages * page

    def page_copies(seq, slot, p):
        pg = pt_ref[seq, p]
        off = pl.multiple_of(p * page, page)
        return (pltpu.make_async_copy(ckv_hbm.at[layer, pg], ckv_buf.at[slot, pl.ds(off, page), :], sem.at[0, slot]),
                pltpu.make_async_copy(kpe_hbm.at[layer, pg], kpe_buf.at[slot, :, pl.ds(off, page)], sem.at[1, slot]))

    def for_each_page(seq, slot, fn):
        def body(p, carry):
            for cp in page_copies(seq, slot, p):
                fn(cp)
            return carry
        lax.fori_loop(0, n_pages, body, 0, unroll=DMA_LOOP_UNROLL)

    @pl.when(b == 0)
    def _():
        for_each_page(0, 0, lambda cp: cp.start())

    @pl.when(b + 1 < pl.num_programs(0))
    def _():
        for_each_page(b + 1, (b + 1) % 2, lambda cp: cp.start())

    slot = b % 2
    for_each_page(b, slot, lambda cp: cp.wait())

    ql = ql_ref[0]
    qp = qp_ref[0]
    ckv_s = ckv_buf.at[slot]
    kpe_s = kpe_buf.at[slot]
    n_chunks = past // chunk

    cn = cn_ref[0].astype(BF16)
    kn = kn_ref[0].astype(BF16)
    s_new = (_dot_nt(ql, cn) + _dot_nt(qp, kn)) * scale
    t = lax.broadcasted_iota(jnp.int32, s_new.shape, 0) // heads
    col = lax.broadcasted_iota(jnp.int32, s_new.shape, 1)
    s_new = jnp.where((col <= t) & (col < tokens), s_new, NEG_BIG)
    m = jnp.max(s_new, axis=-1, keepdims=True)
    p_new = jnp.exp(s_new - m)
    l = jnp.sum(p_new, axis=-1, keepdims=True)
    acc = _dot(p_new.astype(BF16), cn)

    def scores(c):
        sl = slice(c * chunk, (c + 1) * chunk)
        kc = ckv_s[sl, :].astype(BF16)
        kbf[sl, :] = kc
        s_scr[:, sl] = (_dot_nt(ql, kc) + _dot(qp, kpe_s[:, sl].astype(BF16))) * scale

    def values(c, m, l, acc):
        sl = slice(c * chunk, (c + 1) * chunk)
        s = s_scr[:, sl]
        m_new = jnp.maximum(m, jnp.max(s, axis=-1, keepdims=True))
        a = jnp.exp(m - m_new)
        p = jnp.exp(s - m_new)
        l = a * l + jnp.sum(p, axis=-1, keepdims=True)
        return m_new, l, a * acc + _dot(p.astype(BF16), kbf[sl, :])

    scores(0)
    for c in range(n_chunks):
        if c + 1 < n_chunks:
            scores(c + 1)
        m, l, acc = values(c, m, l, acc)
    o_ref[0] = acc / l


def _mla_sample(ql, qp, ckv_new, kpe_new, cache_ckv, cache_kpe_t, layer, page_table, tokens, dims):
    heads, d_nope, d_rope, kv_lora = dims["heads"], dims["d_nope"], dims["d_rope"], dims["kv_lora"]
    b, rows, _ = ql.shape
    n_pages = page_table.shape[1]
    page = cache_ckv.shape[2]
    past = n_pages * page
    tp = ckv_new.shape[1]
    per_b = lambda r, c: pl.BlockSpec((1, r, c), lambda i, pt: (i, 0, 0))
    hbm = pl.BlockSpec(memory_space=pl.ANY)
    kern = functools.partial(_mla_sample_kernel, layer=layer, n_pages=n_pages, page=page, chunk=DECODE_CHUNK,
                             tokens=tokens, heads=heads, scale=(d_nope + d_rope) ** -0.5)
    grid_spec = pltpu.PrefetchScalarGridSpec(
        num_scalar_prefetch=1,
        grid=(b,),
        in_specs=[per_b(rows, kv_lora), per_b(rows, d_rope), per_b(tp, kv_lora), per_b(tp, d_rope), hbm, hbm],
        out_specs=per_b(rows, kv_lora),
        scratch_shapes=[pltpu.VMEM((2, past, kv_lora), F32), pltpu.VMEM((2, d_rope, past), F32),
                        pltpu.VMEM((past, kv_lora), BF16), pltpu.VMEM((rows, past), F32),
                        pltpu.SemaphoreType.DMA((2, 2))],
    )
    return pl.pallas_call(
        kern,
        grid_spec=grid_spec,
        out_shape=jax.ShapeDtypeStruct((b, rows, kv_lora), F32),
        compiler_params=_params(1),
        name="mla_sample",
    )(page_table, ql, qp, ckv_new, kpe_new, cache_ckv, cache_kpe_t)


def _mem_kv_kernel(m_ref, g_ref, wk, wv, k_o, v_o):
    mn = _rms(m_ref[0], g_ref[...]).astype(BF16)
    k_o[0] = _dot(mn, wk[...])
    v_o[0] = _dot(mn, wv[...])


def _mem_kv(mem, g, wk, wv):
    b, m, d = mem.shape
    c = wk.shape[1]
    return pl.pallas_call(
        _mem_kv_kernel,
        grid=(b,),
        in_specs=[pl.BlockSpec((1, m, d), lambda i: (i, 0, 0)), _resident(g.shape), _resident(wk.shape),
                  _resident(wv.shape)],
        out_specs=[pl.BlockSpec((1, m, c), lambda i: (i, 0, 0))] * 2,
        out_shape=[jax.ShapeDtypeStruct((b, m, c), F32)] * 2,
        compiler_params=_params(1),
        name="mem_kv",
    )(mem, g, wk, wv)


def _x_attend_heads(q, mk, mv, heads, hd):
    r = q.shape[0]
    lane = lax.broadcasted_iota(jnp.int32, q.shape, 1)
    sels = [(lane >= h * hd) & (lane < (h + 1) * hd) for h in range(heads)]
    q_heads = jnp.concatenate([jnp.where(sel, q, jnp.zeros_like(q)) for sel in sels], axis=0)
    s = _dot_nt(q_heads, mk) * (hd ** -0.5)
    p = jnp.exp(s - jnp.max(s, axis=-1, keepdims=True))
    p = p / jnp.sum(p, axis=-1, keepdims=True)
    pv = _dot(p.astype(BF16), mv)
    out = jnp.zeros(q.shape, F32)
    for h, sel in enumerate(sels):
        out = out + jnp.where(sel, pv[h * r:(h + 1) * r], 0.0)
    return out


def _x_sample_kernel(q_ref, kt_ref, vt_ref, o_ref, *, group, heads, hd):
    tp = q_ref.shape[1]
    lane = lax.broadcasted_iota(jnp.int32, q_ref.shape[1:], 1)
    sels = [(lane >= h * hd) & (lane < (h + 1) * hd) for h in range(heads)]
    for g in range(group):
        q = q_ref[g]
        q_heads = jnp.concatenate([jnp.where(sel, q, 0.0) for sel in sels], axis=0).astype(BF16)
        s = _dot(q_heads, kt_ref[g].astype(BF16)) * (hd ** -0.5)
        p = jnp.exp(s - jnp.max(s, axis=-1, keepdims=True))
        p = p / jnp.sum(p, axis=-1, keepdims=True)
        r = _dot_nt(p.astype(BF16), vt_ref[g].astype(BF16))
        out = jnp.zeros(q.shape, F32)
        for h, sel in enumerate(sels):
            out = out + jnp.where(sel, r[h * tp:(h + 1) * tp], 0.0)
        o_ref[g] = out


def _x_sample(xq, mkt, mvt, heads):
    b, tp, c = xq.shape
    m = mkt.shape[2]
    G = X_SAMPLE_GROUP
    kern = functools.partial(_x_sample_kernel, group=G, heads=heads, hd=c // heads)
    return pl.pallas_call(
        kern,
        grid=(b // G,),
        in_specs=[pl.BlockSpec((G, tp, c), lambda i: (i, 0, 0)),
                  pl.BlockSpec((G, c, m), lambda i: (i, 0, 0)),
                  pl.BlockSpec((G, c, m), lambda i: (i, 0, 0))],
        out_specs=pl.BlockSpec((G, tp, c), lambda i: (i, 0, 0)),
        out_shape=jax.ShapeDtypeStruct((b, tp, c), F32),
        compiler_params=_params(1),
        name="x_attend_sample",
    )(xq, mkt, mvt)


def _merge_kernel(x_ref, gpre, wgate, oret, wret, omla, wmla, wuv, ox, wx, wout, gpost, *rest,
                  heads, kv_lora, d_v, absorbed, x_heads):
    h_o = rest[-1]
    if x_heads:
        mk, mv = rest[:2]
        o_x = _x_attend_heads(ox[...], mk[0].astype(BF16), mv[0].astype(BF16), x_heads, ox.shape[1] // x_heads)
    else:
        o_x = ox[...]
    x = x_ref[...]
    d = x.shape[1]
    u = _rms(x, gpre[...]).astype(BF16)
    a_ret = _dot(oret[...], wret[...])
    if absorbed:
        a_mla = jnp.zeros(x.shape, F32)
        for h in range(heads):
            o_h = _dot(omla[:, h * kv_lora:(h + 1) * kv_lora].astype(BF16), wuv[h]).astype(BF16)
            a_mla = a_mla + _dot(o_h, wmla[h * d_v:(h + 1) * d_v, :])
    else:
        a_mla = _dot(omla[...], wmla[...])
    a_x = _dot(o_x.astype(BF16), wx[...])
    mixed = jnp.zeros(x.shape, F32)
    for i, a in enumerate((a_ret, a_mla, a_x)):
        mixed = mixed + _sigmoid(_dot(u, wgate[:, i * d:(i + 1) * d])) * a
    h_o[...] = x + _rms(_dot(mixed.astype(BF16), wout[...]), gpost[...])


def _merge(x, oret, omla, ox, w, dims, *, absorbed, tm, mem=None):
    n, d = x.shape
    row = lambda a: pl.BlockSpec((tm, a.shape[1]), lambda i: (i, 0))
    res = lambda a: _resident(a.shape)
    args = [x, w["g_pre"], w["w_gates"], oret, w["w_ret_o"], omla, w["w_mla_o"], w["w_uv_h"], ox, w["w_x_o"],
            w["w_out"], w["g_post"]]
    specs = [row(x), res(w["g_pre"]), res(w["w_gates"]), row(oret), res(w["w_ret_o"]), row(omla),
             res(w["w_mla_o"]), res(w["w_uv_h"]), row(ox), res(w["w_x_o"]), res(w["w_out"]), res(w["g_post"])]
    x_heads = 0
    if mem is not None:
        mk, mv, x_heads, seq_rows = mem
        tiles_per_seq = seq_rows // tm
        per_seq = pl.BlockSpec((1,) + mk.shape[1:], lambda i: (i // tiles_per_seq, 0, 0))
        args += [mk, mv]
        specs += [per_seq, per_seq]
    kern = functools.partial(_merge_kernel, heads=dims["heads"], kv_lora=dims["kv_lora"], d_v=dims["d_v"],
                             absorbed=absorbed, x_heads=x_heads)
    return pl.pallas_call(
        kern,
        grid=(n // tm,),
        in_specs=specs,
        out_specs=pl.BlockSpec((tm, d), lambda i: (i, 0)),
        out_shape=jax.ShapeDtypeStruct((n, d), F32),
        compiler_params=_params(1),
        name="merge_sample" if absorbed else "merge_prompt",
    )(*args)


def _ffn_kernel(h_ref, gpre, wg, wu, wd, gpost, y_o):
    h = h_ref[...]
    f = _rms(h, gpre[...]).astype(BF16)
    gate = _dot(f, wg[...])
    act = (gate * _sigmoid(gate) * _dot(f, wu[...])).astype(BF16)
    y_o[...] = h + _rms(_dot(act, wd[...]), gpost[...])


def _ffn(h, w, *, tm):
    n, d = h.shape
    res = lambda a: _resident(a.shape)
    return pl.pallas_call(
        _ffn_kernel,
        grid=(n // tm,),
        in_specs=[pl.BlockSpec((tm, d), lambda i: (i, 0)), res(w["g_ffn_pre"]), res(w["w_ffn_gate"]),
                  res(w["w_ffn_up"]), res(w["w_ffn_down"]), res(w["g_ffn_post"])],
        out_specs=pl.BlockSpec((tm, d), lambda i: (i, 0)),
        out_shape=jax.ShapeDtypeStruct((n, d), F32),
        compiler_params=_params(1),
        name="ffn",
    )(h, w["g_ffn_pre"], w["w_ffn_gate"], w["w_ffn_up"], w["w_ffn_down"], w["g_ffn_post"])


def _rope_tables(pos, ret_dk, d_rope):
    posf = pos.astype(F32)[:, None]

    def angles(half):
        inv = ROPE_BASE ** (-jnp.arange(half, dtype=F32) / half)
        ang = posf * inv[None, :]
        return jnp.cos(ang), jnp.sin(ang)

    cr, sr = angles(ret_dk // 2)
    cp, sp = angles(d_rope // 2)
    z = jnp.zeros_like(cp)
    pad = jnp.zeros((pos.shape[0], LANES - d_rope), F32)
    return {
        "cosr": jnp.concatenate([cr, cr], axis=1),
        "sinr": jnp.concatenate([-sr, sr], axis=1),
        "cosp": jnp.concatenate([cp, cp, pad], axis=1),
        "sinlo": jnp.concatenate([-sp, z, pad], axis=1),
        "sinhi": jnp.concatenate([z, sp, pad], axis=1),
    }


def _layer_weights(l, dims, sizes, norm_mix_pre, norm_mix_post, norm_ffn_pre, norm_ffn_post, norm_mem, norm_q_lat,
                   norm_kv_lat, w_in, w_uq, w_uk, w_uv, w_mem_k, w_mem_v, w_ret_o, w_mla_o, w_x_o, w_out,
                   w_ffn_gate, w_ffn_up, w_ffn_down):
    heads, d_nope, d_rope, kv_lora, d_v = (dims[k] for k in ("heads", "d_nope", "d_rope", "kv_lora", "d_v"))
    bf = lambda a: a.astype(BF16)
    gain = lambda a: a[l].astype(F32)[None, :]
    offs = np.concatenate([[0], np.cumsum(sizes)])
    seg = [w_in[l][:, offs[i]:offs[i + 1]] for i in range(len(sizes))]
    q_lora = w_uq.shape[1]
    uq = w_uq[l].reshape(q_lora, heads, d_nope + d_rope)
    uq_p = jnp.pad(uq[:, :, d_nope:], ((0, 0), (0, 0), (0, LANES - d_rope)))
    return {
        "g_pre": gain(norm_mix_pre), "g_post": gain(norm_mix_post), "g_ffn_pre": gain(norm_ffn_pre),
        "g_ffn_post": gain(norm_ffn_post), "g_mem": gain(norm_mem), "g_q": gain(norm_q_lat),
        "g_kv": gain(norm_kv_lat),
        "w_rq": bf(seg[0]), "w_rk": bf(seg[1]), "w_rv": bf(seg[2]), "w_rg": bf(seg[3]), "w_cq": bf(seg[4]),
        "w_ckv": bf(seg[5]), "w_kpe": bf(jnp.pad(seg[6], ((0, 0), (0, LANES - d_rope)))), "w_xq": bf(seg[7]),
        "w_gates": bf(seg[8]),
        "w_uq_n": bf(uq[:, :, :d_nope].reshape(q_lora, heads * d_nope)),
        "w_uq_p": bf(uq_p.reshape(q_lora, heads * LANES)),
        "w_uk_t": bf(jnp.swapaxes(w_uk[l], 1, 2)),
        "w_kn": bf(jnp.swapaxes(w_uk[l], 0, 1).reshape(kv_lora, heads * d_nope)),
        "w_vn": bf(jnp.swapaxes(w_uv[l], 0, 1).reshape(kv_lora, heads * d_v)),
        "w_uv_h": bf(w_uv[l]),
        "w_mem_k": bf(w_mem_k[l]), "w_mem_v": bf(w_mem_v[l]),
        "w_ret_o": bf(w_ret_o[l]), "w_mla_o": bf(w_mla_o[l]), "w_x_o": bf(w_x_o[l]), "w_out": bf(w_out[l]),
        "w_ffn_gate": bf(w_ffn_gate[l]), "w_ffn_up": bf(w_ffn_up[l]), "w_ffn_down": bf(w_ffn_down[l]),
    }


def _pad_tokens(a, b, tokens, rows=SAMPLE_TOK_PAD):
    a = a.reshape(b, tokens, a.shape[-1])
    return jnp.pad(a, ((0, 0), (0, rows - tokens), (0, 0)))


def kernel(x_prompt, x_sample, mem_prompt, cache_ckv, cache_kpe, page_table, state_ret, cache_mem_k, cache_mem_v,
           norm_mix_pre, norm_mix_post, norm_ffn_pre, norm_ffn_post, norm_mem, norm_q_lat, norm_kv_lat, w_in,
           w_uq, w_uk, w_uv, w_mem_k, w_mem_v, w_ret_o, w_mla_o, w_x_o, w_out, w_ffn_gate, w_ffn_up, w_ffn_down):
    depth = w_in.shape[0]
    batch, seq, d_model = x_prompt.shape
    db, tokens, _ = x_sample.shape
    ret_heads, ret_dk, ret_dv = state_ret.shape[2:]
    heads, kv_lora, d_nope = w_uk.shape[1:]
    d_rope = cache_kpe.shape[-1]
    d_v = w_uv.shape[-1]
    n_mem, x_heads, x_hd = cache_mem_k.shape[2:]
    q_lora = w_uq.shape[1]
    past_len = page_table.shape[1] * cache_ckv.shape[2]
    dims = dict(heads=heads, d_nope=d_nope, d_rope=d_rope, kv_lora=kv_lora, d_v=d_v,
                ret_heads=ret_heads, ret_dk=ret_dk, ret_dv=ret_dv)
    assert ret_dk == LANES and d_nope == LANES and d_rope <= LANES // 2 and tokens <= SAMPLE_TOK_PAD
    sizes = (ret_heads * ret_dk, ret_heads * ret_dk, ret_heads * ret_dv, ret_heads * ret_dv, q_lora, kv_lora,
             d_rope, x_heads * x_hd, w_in.shape[2] - (2 * ret_heads * ret_dk + 2 * ret_heads * ret_dv + q_lora
                                                       + kv_lora + d_rope + x_heads * x_hd))

    cache_kpe_t = jnp.swapaxes(cache_kpe, 2, 3)

    tabs_p = _rope_tables(jnp.arange(seq), ret_dk, d_rope)
    n_s = db * tokens
    tabs_s = _rope_tables(past_len + (jnp.arange(n_s) % tokens), ret_dk, d_rope)

    y_p = x_prompt.reshape(batch * seq, d_model)
    y_s = x_sample.reshape(n_s, d_model)
    outs = [[] for _ in range(8)]
    for l in range(depth):
        w = _layer_weights(l, dims, sizes, norm_mix_pre, norm_mix_post, norm_ffn_pre, norm_ffn_post, norm_mem,
                           norm_q_lat, norm_kv_lat, w_in, w_uq, w_uk, w_uv, w_mem_k, w_mem_v, w_ret_o, w_mla_o,
                           w_x_o, w_out, w_ffn_gate, w_ffn_up, w_ffn_down)

        mk_p, mv_p = _mem_kv(mem_prompt, w["g_mem"], w["w_mem_k"], w["w_mem_v"])
        rq, rk, rv, rg, qcat, xq, ckv_p, kpe_p, kcat, vn = _in_proj(y_p, w, tabs_p, dims, absorbed=False,
                                                                    tm=ROW_TILE)
        o_ret, ret_p = _ret_prompt(rq, rk, rv, rg, batch, seq, dims)
        o_mla = _mla_prompt(qcat, kcat, vn, batch, seq, dims)
        h_p = _merge(y_p, o_ret, o_mla, xq, w, dims, absorbed=False, tm=ROW_TILE, mem=(mk_p, mv_p, x_heads, seq))
        y_p = _ffn(h_p, w, tm=ROW_TILE)

        rq, rk, rv, rg, qcat, xq, ckv_s, kpe_s, qlat = _in_proj(y_s, w, tabs_s, dims, absorbed=True, tm=ROW_TILE)
        pad = lambda a: _pad_tokens(a, db, tokens)
        o_ret, ret_s = _ret_sample(pad(rq), pad(rk), pad(rv), pad(rg), state_ret[l].astype(F32), tokens, dims)
        o_ret = o_ret[:, :tokens].reshape(n_s, ret_heads * ret_dv)
        group = d_nope + LANES
        q_pe = qcat.reshape(db, tokens * heads, group)[:, :, d_nope:d_nope + d_rope]
        o_lat = _mla_sample(qlat.reshape(db, tokens * heads, kv_lora), q_pe, pad(ckv_s), pad(kpe_s),
                            cache_ckv, cache_kpe_t, l, page_table, tokens, dims)
        o_lat = o_lat.reshape(n_s, heads * kv_lora)
        mem_t = lambda c: jnp.transpose(c[l], (0, 2, 3, 1)).reshape(db, x_heads * x_hd, n_mem)
        o_x = _x_sample(_pad_tokens(xq.astype(F32), db, tokens, rows=X_SAMPLE_TOK_PAD), mem_t(cache_mem_k),
                        mem_t(cache_mem_v), x_heads)
        o_x = o_x[:, :tokens].reshape(n_s, x_heads * x_hd)
        h_s = _merge(y_s, o_ret, o_lat, o_x, w, dims, absorbed=True, tm=ROW_TILE)
        y_s = _ffn(h_s, w, tm=ROW_TILE)

        for lst, val in zip(outs, (ckv_p.reshape(batch, seq, kv_lora), kpe_p.reshape(batch, seq, d_rope),
                                   ckv_s.reshape(db, tokens, kv_lora), kpe_s.reshape(db, tokens, d_rope),
                                   ret_p.astype(x_prompt.dtype), ret_s.astype(state_ret.dtype),
                                   mk_p.reshape(batch, n_mem, x_heads, x_hd),
                                   mv_p.reshape(batch, n_mem, x_heads, x_hd))):
            lst.append(val)

    return (y_p.reshape(batch, seq, d_model), y_s.reshape(db, tokens, d_model)) + tuple(jnp.stack(o) for o in outs)
```

```python
import functools
import math

import numpy as np
import jax
import jax.numpy as jnp
from jax import lax
from jax.experimental import pallas as pl
from jax.experimental.pallas import tpu as pltpu

F32 = jnp.float32
BF16 = jnp.bfloat16

ROPE_BASE = 10000.0
RMS_EPS = 1e-6
LANES = 128
VMEM_LIMIT = 56 * 1024 * 1024
NEG_BIG = -1e30

ROW_TILE = 512
RET_CHUNK = 256
ATTN_Q_TILE = 512
ATTN_KV_TILE = 512
ATTN_HEADS_PER_STEP = 8
SAMPLE_TOK_PAD = 16
X_SAMPLE_TOK_PAD = 8
RET_SAMPLE_GROUP = 8
X_SAMPLE_GROUP = 8
DECODE_CHUNK = 4096
DMA_LOOP_UNROLL = 4


def _resident(shape):
    nd = len(shape)
    return pl.BlockSpec(shape, lambda *_: (0,) * nd, pipeline_mode=pl.Buffered(1))


def _params(n_axes, flags=None):
    return pltpu.CompilerParams(dimension_semantics=("arbitrary",) * n_axes, vmem_limit_bytes=VMEM_LIMIT,
                                flags=flags)


def _rms(x, g=None):
    y = x * lax.rsqrt(jnp.mean(x * x, axis=-1, keepdims=True) + RMS_EPS)
    return y if g is None else y * g


def _sigmoid(x):
    return 1.0 / (1.0 + jnp.exp(-x))


def _dot(a, b):
    return jnp.dot(a, b, preferred_element_type=F32)


def _dot_nt(a, b):
    return lax.dot_general(a, b, (((1,), (1,)), ((), ())), preferred_element_type=F32)


def _dot_tn(a, b):
    return lax.dot_general(a, b, (((0,), (0,)), ((), ())), preferred_element_type=F32)


def _rope_half_vreg(z, cos_t, sin_lo, sin_hi, quarter):
    return (z * cos_t + pltpu.roll(z, LANES - quarter, 1) * sin_lo + pltpu.roll(z, quarter, 1) * sin_hi)


def _in_proj_kernel(x_ref, g_ref, wrq, wrk, wrv, wrg, wcq, wckv, wkpe, wxq, gq_ref, gkv_ref, wuqn, wuqp,
                    cosr, sinr, cosp, sinlo, sinhi, *rest,
                    ret_heads, ret_dk, heads, d_nope, d_rope, kv_lora, absorbed, q_scale):
    if absorbed:
        wa, (rq_o, rk_o, rv_o, rg_o, qcat_o, xq_o, ckv_o, kpe_o, a_o) = rest[0], rest[1:]
    else:
        wa, wb, (rq_o, rk_o, rv_o, rg_o, qcat_o, xq_o, ckv_o, kpe_o, a_o, b_o) = rest[0], rest[1], rest[2:]
    u = _rms(x_ref[...], g_ref[...]).astype(BF16)
    cr, sr = cosr[...], sinr[...]
    cp, slo, shi = cosp[...], sinlo[...], sinhi[...]
    group = d_nope + LANES
    k_scale = ret_dk ** -0.5

    cq = _dot(u, wcq[...])
    ckv = _dot(u, wckv[...])
    kpe_raw = _dot(u, wkpe[...])
    zq = _dot(u, wrq[...])
    zk = _dot(u, wrk[...])

    cqn = _rms(cq, gq_ref[...]).astype(BF16)
    ckvn = _rms(ckv, gkv_ref[...])
    ckv_o[...] = ckvn
    kpe = _rope_half_vreg(kpe_raw, cp, slo, shi, d_rope // 2)
    kpe_o[...] = kpe[:, :d_rope]

    qn = _dot(cqn, wuqn[...])
    qp = _dot(cqn, wuqp[...])
    if not absorbed:
        ckvb = ckvn.astype(BF16)
        kn = _dot(ckvb, wa[...])
        vn = _dot(ckvb, wb[...])

    for h in range(ret_heads):
        sl = slice(h * ret_dk, (h + 1) * ret_dk)
        q_h = zq[:, sl]
        k_h = zk[:, sl]
        rq_o[:, sl] = (q_h * cr + pltpu.roll(q_h, ret_dk // 2, 1) * sr).astype(BF16)
        rk_o[:, sl] = ((k_h * cr + pltpu.roll(k_h, ret_dk // 2, 1) * sr) * k_scale).astype(BF16)

    rv = _dot(u, wrv[...])
    rg = _dot(u, wrg[...])
    xq = _dot(u, wxq[...])

    for h in range(heads):
        qn_h = qn[:, h * d_nope:(h + 1) * d_nope]
        qp_h = _rope_half_vreg(qp[:, h * LANES:(h + 1) * LANES], cp, slo, shi, d_rope // 2)
        if absorbed:
            a_o[:, h * kv_lora:(h + 1) * kv_lora] = _dot(qn_h.astype(BF16), wa[h]).astype(BF16)
        else:
            qn_h = qn_h * q_scale
            qp_h = qp_h * q_scale
        qcat_o[:, h * group:h * group + d_nope] = qn_h.astype(BF16)
        qcat_o[:, h * group + d_nope:(h + 1) * group] = qp_h.astype(BF16)
    if not absorbed:
        kpeb = kpe.astype(BF16)
        for h in range(heads):
            a_o[:, h * group:h * group + d_nope] = kn[:, h * d_nope:(h + 1) * d_nope].astype(BF16)
            a_o[:, h * group + d_nope:(h + 1) * group] = kpeb
        b_o[...] = vn.astype(BF16)

    rv_o[...] = rv.astype(BF16)
    rg_o[...] = rg.astype(BF16)
    xq_o[...] = xq.astype(BF16)


def _in_proj(x, w, tabs, dims, *, absorbed, tm):
    n, d_model = x.shape
    heads, d_nope, d_rope, kv_lora = dims["heads"], dims["d_nope"], dims["d_rope"], dims["kv_lora"]
    ret_heads, ret_dk, ret_dv = dims["ret_heads"], dims["ret_dk"], dims["ret_dv"]
    group = d_nope + LANES
    tab_rows = tabs["cosr"].shape[0]
    tab_tiles = tab_rows // tm
    row = lambda c: pl.BlockSpec((tm, c), lambda i: (i, 0))
    tab = lambda: pl.BlockSpec((tm, LANES), lambda i: (i % tab_tiles, 0))
    weights = [w["g_pre"], w["w_rq"], w["w_rk"], w["w_rv"], w["w_rg"], w["w_cq"], w["w_ckv"], w["w_kpe"],
               w["w_xq"], w["g_q"], w["g_kv"], w["w_uq_n"], w["w_uq_p"]]
    out_cols = [(ret_heads * ret_dk, BF16), (ret_heads * ret_dk, BF16), (ret_heads * ret_dv, BF16),
                (ret_heads * ret_dv, BF16), (heads * group, BF16), (w["w_xq"].shape[1], BF16), (kv_lora, F32),
                (d_rope, F32)]
    if absorbed:
        mode_weights = [w["w_uk_t"]]
        out_cols += [(heads * kv_lora, BF16)]
    else:
        mode_weights = [w["w_kn"], w["w_vn"]]
        out_cols += [(heads * group, BF16), (w["w_vn"].shape[1], BF16)]
    kern = functools.partial(_in_proj_kernel, ret_heads=ret_heads, ret_dk=ret_dk, heads=heads, d_nope=d_nope,
                             d_rope=d_rope, kv_lora=kv_lora, absorbed=absorbed,
                             q_scale=(d_nope + d_rope) ** -0.5 * math.log2(math.e))
    return pl.pallas_call(
        kern,
        grid=(n // tm,),
        in_specs=([row(d_model)] + [_resident(a.shape) for a in weights] + [tab() for _ in range(5)]
                  + [_resident(a.shape) for a in mode_weights]),
        out_specs=[row(c) for c, _ in out_cols],
        out_shape=[jax.ShapeDtypeStruct((n, c), dt) for c, dt in out_cols],
        compiler_params=_params(1),
        name="in_proj_sample" if absorbed else "in_proj_prompt",
    )(x, *weights, tabs["cosr"], tabs["sinr"], tabs["cosp"], tabs["sinlo"], tabs["sinhi"], *mode_weights)


def _ret_head(q, k, v, g, s, dec, qd, kd, g_l):
    inner = _dot_nt(q, k) * dec
    o = _dot(inner.astype(BF16), v) + _dot((q.astype(F32) * qd).astype(BF16), s.astype(BF16))
    s_new = s * g_l + _dot_tn((k.astype(F32) * kd).astype(BF16), v)
    gf = g.astype(F32)
    o = (gf * _sigmoid(gf)) * _rms(o)
    return o.astype(BF16), s_new


def _ret_prompt_kernel(rq, rk, rv, rg, dec, qd, kd, o_ref, s_out, s_scr, *, heads, dk, dv, g_l):
    c = pl.program_id(1)

    @pl.when(c == 0)
    def _():
        s_scr[...] = jnp.zeros(s_scr.shape, F32)

    for h in range(heads):
        o, s_new = _ret_head(rq[:, h * dk:(h + 1) * dk], rk[:, h * dk:(h + 1) * dk],
                             rv[:, h * dv:(h + 1) * dv], rg[:, h * dv:(h + 1) * dv],
                             s_scr[h], dec[h], qd[h], kd[h], g_l[h])
        o_ref[:, h * dv:(h + 1) * dv] = o
        s_scr[h] = s_new

    @pl.when(c == pl.num_programs(1) - 1)
    def _():
        s_out[0] = s_scr[...]


def _ret_consts(heads, length, dk, chunk_rows=None):
    rows = length if chunk_rows is None else chunk_rows
    lg = np.log1p(-np.exp2(-5.0 - np.arange(heads, dtype=np.float64)))
    i = np.arange(rows, dtype=np.float64)
    diff = i[:, None] - i[None, :]
    valid = (diff >= 0) & (i[:, None] < length) & (i[None, :] < length)
    dec = np.where(valid[None], np.exp(np.maximum(diff, 0.0)[None] * lg[:, None, None]), 0.0)
    qd = np.exp((i[None, :] + 1.0) * lg[:, None])
    kd = np.where(i[None, :] < length, np.exp((length - 1.0 - i[None, :]) * lg[:, None]), 0.0)
    qd = np.broadcast_to(qd[:, :, None], (heads, rows, dk))
    kd = np.broadcast_to(kd[:, :, None], (heads, rows, dk))
    g_l = tuple(float(v) for v in np.exp(length * lg))
    return (jnp.asarray(dec, F32), jnp.asarray(qd, F32), jnp.asarray(kd, F32), g_l)


def _ret_prompt(rq, rk, rv, rg, batch, seq, dims):
    heads, dk, dv = dims["ret_heads"], dims["ret_dk"], dims["ret_dv"]
    L = RET_CHUNK
    nc = seq // L
    dec, qd, kd, g_l = _ret_consts(heads, L, dk)
    row = lambda c: pl.BlockSpec((L, c), lambda b, i: (b * nc + i, 0))
    kern = functools.partial(_ret_prompt_kernel, heads=heads, dk=dk, dv=dv, g_l=g_l)
    return pl.pallas_call(
        kern,
        grid=(batch, nc),
        in_specs=[row(heads * dk), row(heads * dk), row(heads * dv), row(heads * dv),
                  _resident(dec.shape), _resident(qd.shape), _resident(kd.shape)],
        out_specs=[row(heads * dv), pl.BlockSpec((1, heads, dk, dv), lambda b, i: (b, 0, 0, 0))],
        out_shape=[jax.ShapeDtypeStruct((batch * seq, heads * dv), BF16),
                   jax.ShapeDtypeStruct((batch, heads, dk, dv), F32)],
        scratch_shapes=[pltpu.VMEM((heads, dk, dv), F32)],
        compiler_params=_params(2),
        name="retention_prompt",
    )(rq, rk, rv, rg, dec, qd, kd)


def _ret_sample_kernel(rq, rk, rv, rg, s0, dec, qd, kd, o_ref, s_out, *, group, heads, dk, dv, g_l):
    for i in range(group):
        for h in range(heads):
            o, s_new = _ret_head(rq[i, :, h * dk:(h + 1) * dk], rk[i, :, h * dk:(h + 1) * dk],
                                 rv[i, :, h * dv:(h + 1) * dv], rg[i, :, h * dv:(h + 1) * dv],
                                 s0[i, h], dec[h], qd[h], kd[h], g_l[h])
            o_ref[i, :, h * dv:(h + 1) * dv] = o
            s_out[i, h] = s_new


def _ret_sample(rq, rk, rv, rg, state, tokens, dims):
    heads, dk, dv = dims["ret_heads"], dims["ret_dk"], dims["ret_dv"]
    b, tp, _ = rq.shape
    G = RET_SAMPLE_GROUP
    dec, qd, kd, g_l = _ret_consts(heads, tokens, dk, chunk_rows=tp)
    blk = lambda c: pl.BlockSpec((G, tp, c), lambda i: (i, 0, 0))
    st = pl.BlockSpec((G, heads, dk, dv), lambda i: (i, 0, 0, 0))
    kern = functools.partial(_ret_sample_kernel, group=G, heads=heads, dk=dk, dv=dv, g_l=g_l)
    return pl.pallas_call(
        kern,
        grid=(b // G,),
        in_specs=[blk(heads * dk), blk(heads * dk), blk(heads * dv), blk(heads * dv), st,
                  _resident(dec.shape), _resident(qd.shape), _resident(kd.shape)],
        out_specs=[blk(heads * dv), st],
        out_shape=[jax.ShapeDtypeStruct((b, tp, heads * dv), BF16),
                   jax.ShapeDtypeStruct((b, heads, dk, dv), F32)],
        compiler_params=_params(1),
        name="retention_sample",
    )(rq, rk, rv, rg, state, dec, qd, kd)


def _mla_prompt_kernel(q_ref, k_ref, v_ref, o_ref, m_scr, l_scr, acc_scr, *, tq, tk, hp, group, d_v):
    qi = pl.program_id(2)
    row0 = qi * tq
    m_scr[...] = jnp.full(m_scr.shape, NEG_BIG, F32)
    l_scr[...] = jnp.zeros(l_scr.shape, F32)
    acc_scr[...] = jnp.zeros(acc_scr.shape, F32)

    def step(j, carry, masked):
        start = pl.multiple_of(j * tk, tk)
        for h in range(hp):
            s = _dot_nt(q_ref[:, h * group:(h + 1) * group], k_ref[pl.ds(start, tk), h * group:(h + 1) * group])
            if masked:
                r = row0 + lax.broadcasted_iota(jnp.int32, s.shape, 0)
                c = start + lax.broadcasted_iota(jnp.int32, s.shape, 1)
                s = jnp.where(c <= r, s, NEG_BIG)
            m_prev = m_scr[h]
            m_new = jnp.maximum(m_prev, jnp.max(s, axis=-1, keepdims=True))
            a = jnp.exp2(m_prev - m_new)
            p = jnp.exp2(s - jnp.tile(m_new, (1, tk // LANES)))
            l_scr[h] = a * l_scr[h] + jnp.sum(p, axis=-1, keepdims=True)
            acc_scr[h] = a * acc_scr[h] + _dot(p.astype(BF16), v_ref[pl.ds(start, tk), h * d_v:(h + 1) * d_v])
            m_scr[h] = m_new
        return carry

    n_full = row0 // tk
    n_all = (row0 + tq + tk - 1) // tk
    lax.fori_loop(0, n_full, functools.partial(step, masked=False), 0)
    lax.fori_loop(n_full, n_all, functools.partial(step, masked=True), 0)
    for h in range(hp):
        o_ref[:, h * d_v:(h + 1) * d_v] = (acc_scr[h] / l_scr[h]).astype(BF16)


def _mla_prompt(qcat, kcat, v, batch, seq, dims):
    heads, d_nope, d_v = dims["heads"], dims["d_nope"], dims["d_v"]
    group = d_nope + LANES
    t = ATTN_Q_TILE
    hp = ATTN_HEADS_PER_STEP
    nq = seq // t
    kern = functools.partial(_mla_prompt_kernel, tq=t, tk=ATTN_KV_TILE, hp=hp, group=group, d_v=d_v)
    return pl.pallas_call(
        kern,
        grid=(batch, heads // hp, nq),
        in_specs=[pl.BlockSpec((t, hp * group), lambda b, h, i: (b * nq + i, h)),
                  pl.BlockSpec((seq, hp * group), lambda b, h, i: (b, h)),
                  pl.BlockSpec((seq, hp * d_v), lambda b, h, i: (b, h))],
        out_specs=pl.BlockSpec((t, hp * d_v), lambda b, h, i: (b * nq + i, h)),
        out_shape=jax.ShapeDtypeStruct((batch * seq, heads * d_v), BF16),
        scratch_shapes=[pltpu.VMEM((hp, t, LANES), F32), pltpu.VMEM((hp, t, LANES), F32),
                        pltpu.VMEM((hp, t, d_v), F32)],
        compiler_params=_params(3),
        name="mla_prompt",
    )(qcat, kcat, v)


def _mla_sample_kernel(pt_ref, ql_ref, qp_ref, cn_ref, kn_ref, ckv_hbm, kpe_hbm, o_ref,
                       ckv_buf, kpe_buf, kbf, s_scr, sem,
                       *, layer, n_pages, page, chunk, tokens, heads, scale):
    b = pl.program_id(0)
    past = n_pages * page

    def page_copies(seq, slot, p):
        pg = pt_ref[seq, p]
        off = pl.multiple_of(p * page, page)
        return (pltpu.make_async_copy(ckv_hbm.at[layer, pg], ckv_buf.at[slot, pl.ds(off, page), :], sem.at[0, slot]),
                pltpu.make_async_copy(kpe_hbm.at[layer, pg], kpe_buf.at[slot, :, pl.ds(off, page)], sem.at[1, slot]))

    def for_each_page(seq, slot, fn):
        def body(p, carry):
            for cp in page_copies(seq, slot, p):
                fn(cp)
            return carry
        lax.fori_loop(0, n_pages, body, 0, unroll=DMA_LOOP_UNROLL)

    @pl.when(b == 0)
    def _():
        for_each_page(0, 0, lambda cp: cp.start())

    @pl.when(b + 1 < pl.num_programs(0))
    def _():
        for_each_page(b + 1, (b + 1) % 2, lambda cp: cp.start())

    slot = b % 2
    for_each_page(b, slot, lambda cp: cp.wait())

    ql = ql_ref[0]
    qp = qp_ref[0]
    ckv_s = ckv_buf.at[slot]
    kpe_s = kpe_buf.at[slot]
    n_chunks = past // chunk

    cn = cn_ref[0].astype(BF16)
    kn = kn_ref[0].astype(BF16)
    s_new = (_dot_nt(ql, cn) + _dot_nt(qp, kn)) * scale
    t = lax.broadcasted_iota(jnp.int32, s_new.shape, 0) // heads
    col = lax.broadcasted_iota(jnp.int32, s_new.shape, 1)
    s_new = jnp.where((col <= t) & (col < tokens), s_new, NEG_BIG)
    m = jnp.max(s_new, axis=-1, keepdims=True)
    p_new = jnp.exp(s_new - m)
    l = jnp.sum(p_new, axis=-1, keepdims=True)
    acc = _dot(p_new.astype(BF16), cn)

    def scores(c):
        sl = slice(c * chunk, (c + 1) * chunk)
        kc = ckv_s[sl, :].astype(BF16)
        kbf[sl, :] = kc
        s_scr[:, sl] = (_dot_nt(ql, kc) + _dot(qp, kpe_s[:, sl].astype(BF16))) * scale

    def values(c, m, l, acc):
        sl = slice(c * chunk, (c + 1) * chunk)
        s = s_scr[:, sl]
        m_new = jnp.maximum(m, jnp.max(s, axis=-1, keepdims=True))
        a = jnp.exp(m - m_new)
        p = jnp.exp(s - m_new)
        l = a * l + jnp.sum(p, axis=-1, keepdims=True)
        return m_new, l, a * acc + _dot(p.astype(BF16), kbf[sl, :])

    scores(0)
    for c in range(n_chunks):
        if c + 1 < n_chunks:
            scores(c + 1)
        m, l, acc = values(c, m, l, acc)
    o_ref[0] = acc / l


def _mla_sample(ql, qp, ckv_new, kpe_new, cache_ckv, cache_kpe_t, layer, page_table, tokens, dims):
    heads, d_nope, d_rope, kv_lora = dims["heads"], dims["d_nope"], dims["d_rope"], dims["kv_lora"]
    b, rows, _ = ql.shape
    n_pages = page_table.shape[1]
    page = cache_ckv.shape[2]
    past = n_pages * page
    tp = ckv_new.shape[1]
    per_b = lambda r, c: pl.BlockSpec((1, r, c), lambda i, pt: (i, 0, 0))
    hbm = pl.BlockSpec(memory_space=pl.ANY)
    kern = functools.partial(_mla_sample_kernel, layer=layer, n_pages=n_pages, page=page, chunk=DECODE_CHUNK,
                             tokens=tokens, heads=heads, scale=(d_nope + d_rope) ** -0.5)
    grid_spec = pltpu.PrefetchScalarGridSpec(
        num_scalar_prefetch=1,
        grid=(b,),
        in_specs=[per_b(rows, kv_lora), per_b(rows, d_rope), per_b(tp, kv_lora), per_b(tp, d_rope), hbm, hbm],
        out_specs=per_b(rows, kv_lora),
        scratch_shapes=[pltpu.VMEM((2, past, kv_lora), F32), pltpu.VMEM((2, d_rope, past), F32),
                        pltpu.VMEM((past, kv_lora), BF16), pltpu.VMEM((rows, past), F32),
                        pltpu.SemaphoreType.DMA((2, 2))],
    )
    return pl.pallas_call(
        kern,
        grid_spec=grid_spec,
        out_shape=jax.ShapeDtypeStruct((b, rows, kv_lora), F32),
        compiler_params=_params(1),
        name="mla_sample",
    )(page_table, ql, qp, ckv_new, kpe_new, cache_ckv, cache_kpe_t)


def _mem_kv_kernel(m_ref, g_ref, wk, wv, k_o, v_o):
    mn = _rms(m_ref[0], g_ref[...]).astype(BF16)
    k_o[0] = _dot(mn, wk[...])
    v_o[0] = _dot(mn, wv[...])


def _mem_kv(mem, g, wk, wv):
    b, m, d = mem.shape
    c = wk.shape[1]
    return pl.pallas_call(
        _mem_kv_kernel,
        grid=(b,),
        in_specs=[pl.BlockSpec((1, m, d), lambda i: (i, 0, 0)), _resident(g.shape), _resident(wk.shape),
                  _resident(wv.shape)],
        out_specs=[pl.BlockSpec((1, m, c), lambda i: (i, 0, 0))] * 2,
        out_shape=[jax.ShapeDtypeStruct((b, m, c), F32)] * 2,
        compiler_params=_params(1),
        name="mem_kv",
    )(mem, g, wk, wv)


def _x_attend_heads(q, mk, mv, heads, hd):
    r = q.shape[0]
    lane = lax.broadcasted_iota(jnp.int32, q.shape, 1)
    sels = [(lane >= h * hd) & (lane < (h + 1) * hd) for h in range(heads)]
    q_heads = jnp.concatenate([jnp.where(sel, q, jnp.zeros_like(q)) for sel in sels], axis=0)
    s = _dot_nt(q_heads, mk) * (hd ** -0.5)
    p = jnp.exp(s - jnp.max(s, axis=-1, keepdims=True))
    p = p / jnp.sum(p, axis=-1, keepdims=True)
    pv = _dot(p.astype(BF16), mv)
    out = jnp.zeros(q.shape, F32)
    for h, sel in enumerate(sels):
        out = out + jnp.where(sel, pv[h * r:(h + 1) * r], 0.0)
    return out


def _x_sample_kernel(q_ref, kt_ref, vt_ref, o_ref, *, group, heads, hd):
    tp = q_ref.shape[1]
    lane = lax.broadcasted_iota(jnp.int32, q_ref.shape[1:], 1)
    sels = [(lane >= h * hd) & (lane < (h + 1) * hd) for h in range(heads)]
    for g in range(group):
        q = q_ref[g]
        q_heads = jnp.concatenate([jnp.where(sel, q, 0.0) for sel in sels], axis=0).astype(BF16)
        s = _dot(q_heads, kt_ref[g].astype(BF16)) * (hd ** -0.5)
        p = jnp.exp(s - jnp.max(s, axis=-1, keepdims=True))
        p = p / jnp.sum(p, axis=-1, keepdims=True)
        r = _dot_nt(p.astype(BF16), vt_ref[g].astype(BF16))
        out = jnp.zeros(q.shape, F32)
        for h, sel in enumerate(sels):
            out = out + jnp.where(sel, r[h * tp:(h + 1) * tp], 0.0)
        o_ref[g] = out


def _x_sample(xq, mkt, mvt, heads):
    b, tp, c = xq.shape
    m = mkt.shape[2]
    G = X_SAMPLE_GROUP
    kern = functools.partial(_x_sample_kernel, group=G, heads=heads, hd=c // heads)
    return pl.pallas_call(
        kern,
        grid=(b // G,),
        in_specs=[pl.BlockSpec((G, tp, c), lambda i: (i, 0, 0)),
                  pl.BlockSpec((G, c, m), lambda i: (i, 0, 0)),
                  pl.BlockSpec((G, c, m), lambda i: (i, 0, 0))],
        out_specs=pl.BlockSpec((G, tp, c), lambda i: (i, 0, 0)),
        out_shape=jax.ShapeDtypeStruct((b, tp, c), F32),
        compiler_params=_params(1),
        name="x_attend_sample",
    )(xq, mkt, mvt)


def _merge_kernel(x_ref, gpre, wgate, oret, wret, omla, wmla, wuv, ox, wx, wout, gpost, *rest,
                  heads, kv_lora, d_v, absorbed, x_heads):
    h_o = rest[-1]
    if x_heads:
        mk, mv = rest[:2]
        o_x = _x_attend_heads(ox[...], mk[0].astype(BF16), mv[0].astype(BF16), x_heads, ox.shape[1] // x_heads)
    else:
        o_x = ox[...]
    x = x_ref[...]
    d = x.shape[1]
    u = _rms(x, gpre[...]).astype(BF16)
    a_ret = _dot(oret[...], wret[...])
    if absorbed:
        a_mla = jnp.zeros(x.shape, F32)
        for h in range(heads):
            o_h = _dot(omla[:, h * kv_lora:(h + 1) * kv_lora].astype(BF16), wuv[h]).astype(BF16)
            a_mla = a_mla + _dot(o_h, wmla[h * d_v:(h + 1) * d_v, :])
    else:
        a_mla = _dot(omla[...], wmla[...])
    a_x = _dot(o_x.astype(BF16), wx[...])
    mixed = jnp.zeros(x.shape, F32)
    for i, a in enumerate((a_ret, a_mla, a_x)):
        mixed = mixed + _sigmoid(_dot(u, wgate[:, i * d:(i + 1) * d])) * a
    h_o[...] = x + _rms(_dot(mixed.astype(BF16), wout[...]), gpost[...])


def _merge(x, oret, omla, ox, w, dims, *, absorbed, tm, mem=None):
    n, d = x.shape
    row = lambda a: pl.BlockSpec((tm, a.shape[1]), lambda i: (i, 0))
    res = lambda a: _resident(a.shape)
    args = [x, w["g_pre"], w["w_gates"], oret, w["w_ret_o"], omla, w["w_mla_o"], w["w_uv_h"], ox, w["w_x_o"],
            w["w_out"], w["g_post"]]
    specs = [row(x), res(w["g_pre"]), res(w["w_gates"]), row(oret), res(w["w_ret_o"]), row(omla),
             res(w["w_mla_o"]), res(w["w_uv_h"]), row(ox), res(w["w_x_o"]), res(w["w_out"]), res(w["g_post"])]
    x_heads = 0
    if mem is not None:
        mk, mv, x_heads, seq_rows = mem
        tiles_per_seq = seq_rows // tm
        per_seq = pl.BlockSpec((1,) + mk.shape[1:], lambda i: (i // tiles_per_seq, 0, 0))
        args += [mk, mv]
        specs += [per_seq, per_seq]
    kern = functools.partial(_merge_kernel, heads=dims["heads"], kv_lora=dims["kv_lora"], d_v=dims["d_v"],
                             absorbed=absorbed, x_heads=x_heads)
    return pl.pallas_call(
        kern,
        grid=(n // tm,),
        in_specs=specs,
        out_specs=pl.BlockSpec((tm, d), lambda i: (i, 0)),
        out_shape=jax.ShapeDtypeStruct((n, d), F32),
        compiler_params=_params(1),
        name="merge_sample" if absorbed else "merge_prompt",
    )(*args)


def _ffn_kernel(h_ref, gpre, wg, wu, wd, gpost, y_o):
    h = h_ref[...]
    f = _rms(h, gpre[...]).astype(BF16)
    gate = _dot(f, wg[...])
    act = (gate * _sigmoid(gate) * _dot(f, wu[...])).astype(BF16)
    y_o[...] = h + _rms(_dot(act, wd[...]), gpost[...])


def _ffn(h, w, *, tm):
    n, d = h.shape
    res = lambda a: _resident(a.shape)
    return pl.pallas_call(
        _ffn_kernel,
        grid=(n // tm,),
        in_specs=[pl.BlockSpec((tm, d), lambda i: (i, 0)), res(w["g_ffn_pre"]), res(w["w_ffn_gate"]),
                  res(w["w_ffn_up"]), res(w["w_ffn_down"]), res(w["g_ffn_post"])],
        out_specs=pl.BlockSpec((tm, d), lambda i: (i, 0)),
        out_shape=jax.ShapeDtypeStruct((n, d), F32),
        compiler_params=_params(1),
        name="ffn",
    )(h, w["g_ffn_pre"], w["w_ffn_gate"], w["w_ffn_up"], w["w_ffn_down"], w["g_ffn_post"])


def _rope_tables(pos, ret_dk, d_rope):
    posf = pos.astype(F32)[:, None]

    def angles(half):
        inv = ROPE_BASE ** (-jnp.arange(half, dtype=F32) / half)
        ang = posf * inv[None, :]
        return jnp.cos(ang), jnp.sin(ang)

    cr, sr = angles(ret_dk // 2)
    cp, sp = angles(d_rope // 2)
    z = jnp.zeros_like(cp)
    pad = jnp.zeros((pos.shape[0], LANES - d_rope), F32)
    return {
        "cosr": jnp.concatenate([cr, cr], axis=1),
        "sinr": jnp.concatenate([-sr, sr], axis=1),
        "cosp": jnp.concatenate([cp, cp, pad], axis=1),
        "sinlo": jnp.concatenate([-sp, z, pad], axis=1),
        "sinhi": jnp.concatenate([z, sp, pad], axis=1),
    }


def _layer_weights(l, dims, sizes, norm_mix_pre, norm_mix_post, norm_ffn_pre, norm_ffn_post, norm_mem, norm_q_lat,
                   norm_kv_lat, w_in, w_uq, w_uk, w_uv, w_mem_k, w_mem_v, w_ret_o, w_mla_o, w_x_o, w_out,
                   w_ffn_gate, w_ffn_up, w_ffn_down):
    heads, d_nope, d_rope, kv_lora, d_v = (dims[k] for k in ("heads", "d_nope", "d_rope", "kv_lora", "d_v"))
    bf = lambda a: a.astype(BF16)
    gain = lambda a: a[l].astype(F32)[None, :]
    offs = np.concatenate([[0], np.cumsum(sizes)])
    seg = [w_in[l][:, offs[i]:offs[i + 1]] for i in range(len(sizes))]
    q_lora = w_uq.shape[1]
    uq = w_uq[l].reshape(q_lora, heads, d_nope + d_rope)
    uq_p = jnp.pad(uq[:, :, d_nope:], ((0, 0), (0, 0), (0, LANES - d_rope)))
    return {
        "g_pre": gain(norm_mix_pre), "g_post": gain(norm_mix_post), "g_ffn_pre": gain(norm_ffn_pre),
        "g_ffn_post": gain(norm_ffn_post), "g_mem": gain(norm_mem), "g_q": gain(norm_q_lat),
        "g_kv": gain(norm_kv_lat),
        "w_rq": bf(seg[0]), "w_rk": bf(seg[1]), "w_rv": bf(seg[2]), "w_rg": bf(seg[3]), "w_cq": bf(seg[4]),
        "w_ckv": bf(seg[5]), "w_kpe": bf(jnp.pad(seg[6], ((0, 0), (0, LANES - d_rope)))), "w_xq": bf(seg[7]),
        "w_gates": bf(seg[8]),
        "w_uq_n": bf(uq[:, :, :d_nope].reshape(q_lora, heads * d_nope)),
        "w_uq_p": bf(uq_p.reshape(q_lora, heads * LANES)),
        "w_uk_t": bf(jnp.swapaxes(w_uk[l], 1, 2)),
        "w_kn": bf(jnp.swapaxes(w_uk[l], 0, 1).reshape(kv_lora, heads * d_nope)),
        "w_vn": bf(jnp.swapaxes(w_uv[l], 0, 1).reshape(kv_lora, heads * d_v)),
        "w_uv_h": bf(w_uv[l]),
        "w_mem_k": bf(w_mem_k[l]), "w_mem_v": bf(w_mem_v[l]),
        "w_ret_o": bf(w_ret_o[l]), "w_mla_o": bf(w_mla_o[l]), "w_x_o": bf(w_x_o[l]), "w_out": bf(w_out[l]),
        "w_ffn_gate": bf(w_ffn_gate[l]), "w_ffn_up": bf(w_ffn_up[l]), "w_ffn_down": bf(w_ffn_down[l]),
    }


def _pad_tokens(a, b, tokens, rows=SAMPLE_TOK_PAD):
    a = a.reshape(b, tokens, a.shape[-1])
    return jnp.pad(a, ((0, 0), (0, rows - tokens), (0, 0)))


def kernel(x_prompt, x_sample, mem_prompt, cache_ckv, cache_kpe, page_table, state_ret, cache_mem_k, cache_mem_v,
           norm_mix_pre, norm_mix_post, norm_ffn_pre, norm_ffn_post, norm_mem, norm_q_lat, norm_kv_lat, w_in,
           w_uq, w_uk, w_uv, w_mem_k, w_mem_v, w_ret_o, w_mla_o, w_x_o, w_out, w_ffn_gate, w_ffn_up, w_ffn_down):
    depth = w_in.shape[0]
    batch, seq, d_model = x_prompt.shape
    db, tokens, _ = x_sample.shape
    ret_heads, ret_dk, ret_dv = state_ret.shape[2:]
    heads, kv_lora, d_nope = w_uk.shape[1:]
    d_rope = cache_kpe.shape[-1]
    d_v = w_uv.shape[-1]
    n_mem, x_heads, x_hd = cache_mem_k.shape[2:]
    q_lora = w_uq.shape[1]
    past_len = page_table.shape[1] * cache_ckv.shape[2]
    dims = dict(heads=heads, d_nope=d_nope, d_rope=d_rope, kv_lora=kv_lora, d_v=d_v,
                ret_heads=ret_heads, ret_dk=ret_dk, ret_dv=ret_dv)
    assert ret_dk == LANES and d_nope == LANES and d_rope <= LANES // 2 and tokens <= SAMPLE_TOK_PAD
    sizes = (ret_heads * ret_dk, ret_heads * ret_dk, ret_heads * ret_dv, ret_heads * ret_dv, q_lora, kv_lora,
             d_rope, x_heads * x_hd, w_in.shape[2] - (2 * ret_heads * ret_dk + 2 * ret_heads * ret_dv + q_lora
                                                       + kv_lora + d_rope + x_heads * x_hd))

    cache_kpe_t = jnp.swapaxes(cache_kpe, 2, 3)

    tabs_p = _rope_tables(jnp.arange(seq), ret_dk, d_rope)
    n_s = db * tokens
    tabs_s = _rope_tables(past_len + (jnp.arange(n_s) % tokens), ret_dk, d_rope)

    y_p = x_prompt.reshape(batch * seq, d_model)
    y_s = x_sample.reshape(n_s, d_model)
    outs = [[] for _ in range(8)]
    for l in range(depth):
        w = _layer_weights(l, dims, sizes, norm_mix_pre, norm_mix_post, norm_ffn_pre, norm_ffn_post, norm_mem,
                           norm_q_lat, norm_kv_lat, w_in, w_uq, w_uk, w_uv, w_mem_k, w_mem_v, w_ret_o, w_mla_o,
                           w_x_o, w_out, w_ffn_gate, w_ffn_up, w_ffn_down)

        mk_p, mv_p = _mem_kv(mem_prompt, w["g_mem"], w["w_mem_k"], w["w_mem_v"])
        rq, rk, rv, rg, qcat, xq, ckv_p, kpe_p, kcat, vn = _in_proj(y_p, w, tabs_p, dims, absorbed=False,
                                                                    tm=ROW_TILE)
        o_ret, ret_p = _ret_prompt(rq, rk, rv, rg, batch, seq, dims)
        o_mla = _mla_prompt(qcat, kcat, vn, batch, seq, dims)
        h_p = _merge(y_p, o_ret, o_mla, xq, w, dims, absorbed=False, tm=ROW_TILE, mem=(mk_p, mv_p, x_heads, seq))
        y_p = _ffn(h_p, w, tm=ROW_TILE)

        rq, rk, rv, rg, qcat, xq, ckv_s, kpe_s, qlat = _in_proj(y_s, w, tabs_s, dims, absorbed=True, tm=ROW_TILE)
        pad = lambda a: _pad_tokens(a, db, tokens)
        o_ret, ret_s = _ret_sample(pad(rq), pad(rk), pad(rv), pad(rg), state_ret[l].astype(F32), tokens, dims)
        o_ret = o_ret[:, :tokens].reshape(n_s, ret_heads * ret_dv)
        group = d_nope + LANES
        q_pe = qcat.reshape(db, tokens * heads, group)[:, :, d_nope:d_nope + d_rope]
        o_lat = _mla_sample(qlat.reshape(db, tokens * heads, kv_lora), q_pe, pad(ckv_s), pad(kpe_s),
                            cache_ckv, cache_kpe_t, l, page_table, tokens, dims)
        o_lat = o_lat.reshape(n_s, heads * kv_lora)
        mem_t = lambda c: jnp.transpose(c[l], (0, 2, 3, 1)).reshape(db, x_heads * x_hd, n_mem)
        o_x = _x_sample(_pad_tokens(xq.astype(F32), db, tokens, rows=X_SAMPLE_TOK_PAD), mem_t(cache_mem_k),
                        mem_t(cache_mem_v), x_heads)
        o_x = o_x[:, :tokens].reshape(n_s, x_heads * x_hd)
        h_s = _merge(y_s, o_ret, o_lat, o_x, w, dims, absorbed=True, tm=ROW_TILE)
        y_s = _ffn(h_s, w, tm=ROW_TILE)

        for lst, val in zip(outs, (ckv_p.reshape(batch, seq, kv_lora), kpe_p.reshape(batch, seq, d_rope),
                                   ckv_s.reshape(db, tokens, kv_lora), kpe_s.reshape(db, tokens, d_rope),
                                   ret_p.astype(x_prompt.dtype), ret_s.astype(state_ret.dtype),
                                   mk_p.reshape(batch, n_mem, x_heads, x_hd),
                                   mv_p.reshape(batch, n_mem, x_heads, x_hd))):
            lst.append(val)

    return (y_p.reshape(batch, seq, d_model), y_s.reshape(db, tokens, d_model)) + tuple(jnp.stack(o) for o in outs)
```

```python
import functools
import math

import numpy as np
import jax
import jax.numpy as jnp
from jax import lax
from jax.experimental import pallas as pl
from jax.experimental.pallas import tpu as pltpu

F32 = jnp.float32
BF16 = jnp.bfloat16

ROPE_BASE = 10000.0
RMS_EPS = 1e-6
LANES = 128
VMEM_LIMIT = 56 * 1024 * 1024
NEG_BIG = -1e30

ROW_TILE = 512
RET_CHUNK = 256
ATTN_Q_TILE = 512
ATTN_KV_TILE = 512
ATTN_HEADS_PER_STEP = 8
SAMPLE_TOK_PAD = 16
X_SAMPLE_TOK_PAD = 8
RET_SAMPLE_GROUP = 8
X_SAMPLE_GROUP = 8
DECODE_CHUNK = 4096
DMA_LOOP_UNROLL = 4


def _resident(shape):
    nd = len(shape)
    return pl.BlockSpec(shape, lambda *_: (0,) * nd, pipeline_mode=pl.Buffered(1))


def _params(n_axes, flags=None):
    return pltpu.CompilerParams(dimension_semantics=("arbitrary",) * n_axes, vmem_limit_bytes=VMEM_LIMIT,
                                flags=flags)


def _rms(x, g=None):
    y = x * lax.rsqrt(jnp.mean(x * x, axis=-1, keepdims=True) + RMS_EPS)
    return y if g is None else y * g


def _sigmoid(x):
    return 1.0 / (1.0 + jnp.exp(-x))


def _dot(a, b):
    return jnp.dot(a, b, preferred_element_type=F32)


def _dot_nt(a, b):
    return lax.dot_general(a, b, (((1,), (1,)), ((), ())), preferred_element_type=F32)


def _dot_tn(a, b):
    return lax.dot_general(a, b, (((0,), (0,)), ((), ())), preferred_element_type=F32)


def _rope_half_vreg(z, cos_t, sin_lo, sin_hi, quarter):
    return (z * cos_t + pltpu.roll(z, LANES - quarter, 1) * sin_lo + pltpu.roll(z, quarter, 1) * sin_hi)


def _in_proj_kernel(x_ref, g_ref, wrq, wrk, wrv, wrg, wcq, wckv, wkpe, wxq, gq_ref, gkv_ref, wuqn, wuqp,
                    cosr, sinr, cosp, sinlo, sinhi, *rest,
                    ret_heads, ret_dk, heads, d_nope, d_rope, kv_lora, absorbed, q_scale):
    if absorbed:
        wa, (rq_o, rk_o, rv_o, rg_o, qcat_o, xq_o, ckv_o, kpe_o, a_o) = rest[0], rest[1:]
    else:
        wa, wb, (rq_o, rk_o, rv_o, rg_o, qcat_o, xq_o, ckv_o, kpe_o, a_o, b_o) = rest[0], rest[1], rest[2:]
    u = _rms(x_ref[...], g_ref[...]).astype(BF16)
    cr, sr = cosr[...], sinr[...]
    cp, slo, shi = cosp[...], sinlo[...], sinhi[...]
    group = d_nope + LANES
    k_scale = ret_dk ** -0.5

    cq = _dot(u, wcq[...])
    ckv = _dot(u, wckv[...])
    kpe_raw = _dot(u, wkpe[...])
    zq = _dot(u, wrq[...])
    zk = _dot(u, wrk[...])

    cqn = _rms(cq, gq_ref[...]).astype(BF16)
    ckvn = _rms(ckv, gkv_ref[...])
    ckv_o[...] = ckvn
    kpe = _rope_half_vreg(kpe_raw, cp, slo, shi, d_rope // 2)
    kpe_o[...] = kpe[:, :d_rope]

    qn = _dot(cqn, wuqn[...])
    qp = _dot(cqn, wuqp[...])
    if not absorbed:
        ckvb = ckvn.astype(BF16)
        kn = _dot(ckvb, wa[...])
        vn = _dot(ckvb, wb[...])

    for h in range(ret_heads):
        sl = slice(h * ret_dk, (h + 1) * ret_dk)
        q_h = zq[:, sl]
        k_h = zk[:, sl]
        rq_o[:, sl] = (q_h * cr + pltpu.roll(q_h, ret_dk // 2, 1) * sr).astype(BF16)
        rk_o[:, sl] = ((k_h * cr + pltpu.roll(k_h, ret_dk // 2, 1) * sr) * k_scale).astype(BF16)

    rv = _dot(u, wrv[...])
    rg = _dot(u, wrg[...])
    xq = _dot(u, wxq[...])

    for h in range(heads):
        qn_h = qn[:, h * d_nope:(h + 1) * d_nope]
        qp_h = _rope_half_vreg(qp[:, h * LANES:(h + 1) * LANES], cp, slo, shi, d_rope // 2)
        if absorbed:
            a_o[:, h * kv_lora:(h + 1) * kv_lora] = _dot(qn_h.astype(BF16), wa[h]).astype(BF16)
        else:
            qn_h = qn_h * q_scale
            qp_h = qp_h * q_scale
        qcat_o[:, h * group:h * group + d_nope] = qn_h.astype(BF16)
        qcat_o[:, h * group + d_nope:(h + 1) * group] = qp_h.astype(BF16)
    if not absorbed:
        kpeb = kpe.astype(BF16)
        for h in range(heads):
            a_o[:, h * group:h * group + d_nope] = kn[:, h * d_nope:(h + 1) * d_nope].astype(BF16)
            a_o[:, h * group + d_nope:(h + 1) * group] = kpeb
        b_o[...] = vn.astype(BF16)

    rv_o[...] = rv.astype(BF16)
    rg_o[...] = rg.astype(BF16)
    xq_o[...] = xq.astype(BF16)


def _in_proj(x, w, tabs, dims, *, absorbed, tm):
    n, d_model = x.shape
    heads, d_nope, d_rope, kv_lora = dims["heads"], dims["d_nope"], dims["d_rope"], dims["kv_lora"]
    ret_heads, ret_dk, ret_dv = dims["ret_heads"], dims["ret_dk"], dims["ret_dv"]
    group = d_nope + LANES
    tab_rows = tabs["cosr"].shape[0]
    tab_tiles = tab_rows // tm
    row = lambda c: pl.BlockSpec((tm, c), lambda i: (i, 0))
    tab = lambda: pl.BlockSpec((tm, LANES), lambda i: (i % tab_tiles, 0))
    weights = [w["g_pre"], w["w_rq"], w["w_rk"], w["w_rv"], w["w_rg"], w["w_cq"], w["w_ckv"], w["w_kpe"],
               w["w_xq"], w["g_q"], w["g_kv"], w["w_uq_n"], w["w_uq_p"]]
    out_cols = [(ret_heads * ret_dk, BF16), (ret_heads * ret_dk, BF16), (ret_heads * ret_dv, BF16),
                (ret_heads * ret_dv, BF16), (heads * group, BF16), (w["w_xq"].shape[1], BF16), (kv_lora, F32),
                (d_rope, F32)]
    if absorbed:
        mode_weights = [w["w_uk_t"]]
        out_cols += [(heads * kv_lora, BF16)]
    else:
        mode_weights = [w["w_kn"], w["w_vn"]]
        out_cols += [(heads * group, BF16), (w["w_vn"].shape[1], BF16)]
    kern = functools.partial(_in_proj_kernel, ret_heads=ret_heads, ret_dk=ret_dk, heads=heads, d_nope=d_nope,
                             d_rope=d_rope, kv_lora=kv_lora, absorbed=absorbed,
                             q_scale=(d_nope + d_rope) ** -0.5 * math.log2(math.e))
    return pl.pallas_call(
        kern,
        grid=(n // tm,),
        in_specs=([row(d_model)] + [_resident(a.shape) for a in weights] + [tab() for _ in range(5)]
                  + [_resident(a.shape) for a in mode_weights]),
        out_specs=[row(c) for c, _ in out_cols],
        out_shape=[jax.ShapeDtypeStruct((n, c), dt) for c, dt in out_cols],
        compiler_params=_params(1),
        name="in_proj_sample" if absorbed else "in_proj_prompt",
    )(x, *weights, tabs["cosr"], tabs["sinr"], tabs["cosp"], tabs["sinlo"], tabs["sinhi"], *mode_weights)


def _ret_head(q, k, v, g, s, dec, qd, kd, g_l):
    inner = _dot_nt(q, k) * dec
    o = _dot(inner.astype(BF16), v) + _dot((q.astype(F32) * qd).astype(BF16), s.astype(BF16))
    s_new = s * g_l + _dot_tn((k.astype(F32) * kd).astype(BF16), v)
    gf = g.astype(F32)
    o = (gf * _sigmoid(gf)) * _rms(o)
    return o.astype(BF16), s_new


def _ret_prompt_kernel(rq, rk, rv, rg, dec, qd, kd, o_ref, s_out, s_scr, *, heads, dk, dv, g_l):
    c = pl.program_id(1)

    @pl.when(c == 0)
    def _():
        s_scr[...] = jnp.zeros(s_scr.shape, F32)

    for h in range(heads):
        o, s_new = _ret_head(rq[:, h * dk:(h + 1) * dk], rk[:, h * dk:(h + 1) * dk],
                             rv[:, h * dv:(h + 1) * dv], rg[:, h * dv:(h + 1) * dv],
                             s_scr[h], dec[h], qd[h], kd[h], g_l[h])
        o_ref[:, h * dv:(h + 1) * dv] = o
        s_scr[h] = s_new

    @pl.when(c == pl.num_programs(1) - 1)
    def _():
        s_out[0] = s_scr[...]


def _ret_consts(heads, length, dk, chunk_rows=None):
    rows = length if chunk_rows is None else chunk_rows
    lg = np.log1p(-np.exp2(-5.0 - np.arange(heads, dtype=np.float64)))
    i = np.arange(rows, dtype=np.float64)
    diff = i[:, None] - i[None, :]
    valid = (diff >= 0) & (i[:, None] < length) & (i[None, :] < length)
    dec = np.where(valid[None], np.exp(np.maximum(diff, 0.0)[None] * lg[:, None, None]), 0.0)
    qd = np.exp((i[None, :] + 1.0) * lg[:, None])
    kd = np.where(i[None, :] < length, np.exp((length - 1.0 - i[None, :]) * lg[:, None]), 0.0)
    qd = np.broadcast_to(qd[:, :, None], (heads, rows, dk))
    kd = np.broadcast_to(kd[:, :, None], (heads, rows, dk))
    g_l = tuple(float(v) for v in np.exp(length * lg))
    return (jnp.asarray(dec, F32), jnp.asarray(qd, F32), jnp.asarray(kd, F32), g_l)


def _ret_prompt(rq, rk, rv, rg, batch, seq, dims):
    heads, dk, dv = dims["ret_heads"], dims["ret_dk"], dims["ret_dv"]
    L = RET_CHUNK
    nc = seq // L
    dec, qd, kd, g_l = _ret_consts(heads, L, dk)
    row = lambda c: pl.BlockSpec((L, c), lambda b, i: (b * nc + i, 0))
    kern = functools.partial(_ret_prompt_kernel, heads=heads, dk=dk, dv=dv, g_l=g_l)
    return pl.pallas_call(
        kern,
        grid=(batch, nc),
        in_specs=[row(heads * dk), row(heads * dk), row(heads * dv), row(heads * dv),
                  _resident(dec.shape), _resident(qd.shape), _resident(kd.shape)],
        out_specs=[row(heads * dv), pl.BlockSpec((1, heads, dk, dv), lambda b, i: (b, 0, 0, 0))],
        out_shape=[jax.ShapeDtypeStruct((batch * seq, heads * dv), BF16),
                   jax.ShapeDtypeStruct((batch, heads, dk, dv), F32)],
        scratch_shapes=[pltpu.VMEM((heads, dk, dv), F32)],
        compiler_params=_params(2),
        name="retention_prompt",
    )(rq, rk, rv, rg, dec, qd, kd)


def _ret_sample_kernel(rq, rk, rv, rg, s0, dec, qd, kd, o_ref, s_out, *, group, heads, dk, dv, g_l):
    for i in range(group):
        for h in range(heads):
            o, s_new = _ret_head(rq[i, :, h * dk:(h + 1) * dk], rk[i, :, h * dk:(h + 1) * dk],
                                 rv[i, :, h * dv:(h + 1) * dv], rg[i, :, h * dv:(h + 1) * dv],
                                 s0[i, h], dec[h], qd[h], kd[h], g_l[h])
            o_ref[i, :, h * dv:(h + 1) * dv] = o
            s_out[i, h] = s_new


def _ret_sample(rq, rk, rv, rg, state, tokens, dims):
    heads, dk, dv = dims["ret_heads"], dims["ret_dk"], dims["ret_dv"]
    b, tp, _ = rq.shape
    G = RET_SAMPLE_GROUP
    dec, qd, kd, g_l = _ret_consts(heads, tokens, dk, chunk_rows=tp)
    blk = lambda c: pl.BlockSpec((G, tp, c), lambda i: (i, 0, 0))
    st = pl.BlockSpec((G, heads, dk, dv), lambda i: (i, 0, 0, 0))
    kern = functools.partial(_ret_sample_kernel, group=G, heads=heads, dk=dk, dv=dv, g_l=g_l)
    return pl.pallas_call(
        kern,
        grid=(b // G,),
        in_specs=[blk(heads * dk), blk(heads * dk), blk(heads * dv), blk(heads * dv), st,
                  _resident(dec.shape), _resident(qd.shape), _resident(kd.shape)],
        out_specs=[blk(heads * dv), st],
        out_shape=[jax.ShapeDtypeStruct((b, tp, heads * dv), BF16),
                   jax.ShapeDtypeStruct((b, heads, dk, dv), F32)],
        compiler_params=_params(1),
        name="retention_sample",
    )(rq, rk, rv, rg, state, dec, qd, kd)


def _mla_prompt_kernel(q_ref, k_ref, v_ref, o_ref, m_scr, l_scr, acc_scr, *, tq, tk, hp, group, d_v):
    qi = pl.program_id(2)
    ri = lax.broadcasted_iota(jnp.int32, (tq, tk), 0)
    ci = lax.broadcasted_iota(jnp.int32, (tq, tk), 1)
    causal_bias = jnp.where(ci <= ri, 0.0, NEG_BIG)

    def block(j, carry, first):
        start = pl.multiple_of(j * tk, tk)

        def scores(h):
            return _dot_nt(q_ref[:, h * group:(h + 1) * group], k_ref[pl.ds(start, tk), h * group:(h + 1) * group])

        s_next = scores(0)
        for h in range(hp):
            s = s_next
            if h + 1 < hp:
                s_next = scores(h + 1)
            vals = v_ref[pl.ds(start, tk), h * d_v:(h + 1) * d_v]
            if first:
                s = s + causal_bias
                m_new = jnp.broadcast_to(jnp.max(s, axis=-1, keepdims=True), (tq, LANES))
                p = jnp.exp2(s - jnp.tile(m_new, (1, tk // LANES)))
                l_scr[h] = jnp.broadcast_to(jnp.sum(p, axis=-1, keepdims=True), (tq, LANES))
                acc_scr[h] = _dot(p.astype(BF16), vals)
            else:
                m_prev = m_scr[h]
                m_new = jnp.maximum(m_prev, jnp.max(s, axis=-1, keepdims=True))
                a = jnp.exp2(m_prev - m_new)
                p = jnp.exp2(s - jnp.tile(m_new, (1, tk // LANES)))
                l_scr[h] = a * l_scr[h] + jnp.sum(p, axis=-1, keepdims=True)
                acc_scr[h] = a * acc_scr[h] + _dot(p.astype(BF16), vals)
            m_scr[h] = m_new
        return carry

    block(qi, 0, True)
    lax.fori_loop(0, qi, functools.partial(block, first=False), 0)
    for h in range(hp):
        o_ref[:, h * d_v:(h + 1) * d_v] = (acc_scr[h] / l_scr[h]).astype(BF16)


def _mla_prompt(qcat, kcat, v, batch, seq, dims):
    heads, d_nope, d_v = dims["heads"], dims["d_nope"], dims["d_v"]
    group = d_nope + LANES
    t = ATTN_Q_TILE
    hp = ATTN_HEADS_PER_STEP
    nq = seq // t
    assert ATTN_KV_TILE == t and d_v == LANES
    kern = functools.partial(_mla_prompt_kernel, tq=t, tk=ATTN_KV_TILE, hp=hp, group=group, d_v=d_v)
    return pl.pallas_call(
        kern,
        grid=(batch, heads // hp, nq),
        in_specs=[pl.BlockSpec((t, hp * group), lambda b, h, i: (b * nq + i, h)),
                  pl.BlockSpec((seq, hp * group), lambda b, h, i: (b, h)),
                  pl.BlockSpec((seq, hp * d_v), lambda b, h, i: (b, h))],
        out_specs=pl.BlockSpec((t, hp * d_v), lambda b, h, i: (b * nq + i, h)),
        out_shape=jax.ShapeDtypeStruct((batch * seq, heads * d_v), BF16),
        scratch_shapes=[pltpu.VMEM((hp, t, LANES), F32), pltpu.VMEM((hp, t, LANES), F32),
                        pltpu.VMEM((hp, t, d_v), F32)],
        compiler_params=_params(3),
        name="mla_prompt",
    )(qcat, kcat, v)


def _mla_sample_kernel(pt_ref, ql_ref, qp_ref, cn_ref, kn_ref, ckv_hbm, kpe_hbm, o_ref,
                       ckv_buf, kpe_buf, kbf, s_scr, sem,
                       *, layer, n_pages, page, chunk, tokens, heads, scale):
    b = pl.program_id(0)
    past = n_pages * page

    def page_copies(seq, slot, p):
        pg = pt_ref[seq, p]
        off = pl.multiple_of(p * page, page)
        return (pltpu.make_async_copy(ckv_hbm.at[layer, pg], ckv_buf.at[slot, pl.ds(off, page), :], sem.at[0, slot]),
                pltpu.make_async_copy(kpe_hbm.at[layer, pg], kpe_buf.at[slot, :, pl.ds(off, page)], sem.at[1, slot]))

    def for_each_page(seq, slot, fn):
        def body(p, carry):
            for cp in page_copies(seq, slot, p):
                fn(cp)
            return carry
        lax.fori_loop(0, n_pages, body, 0, unroll=DMA_LOOP_UNROLL)

    @pl.when(b == 0)
    def _():
        for_each_page(0, 0, lambda cp: cp.start())

    @pl.when(b + 1 < pl.num_programs(0))
    def _():
        for_each_page(b + 1, (b + 1) % 2, lambda cp: cp.start())

    slot = b % 2
    for_each_page(b, slot, lambda cp: cp.wait())

    ql = ql_ref[0]
    qp = qp_ref[0]
    ckv_s = ckv_buf.at[slot]
    kpe_s = kpe_buf.at[slot]
    n_chunks = past // chunk

    cn = cn_ref[0].astype(BF16)
    kn = kn_ref[0].astype(BF16)
    s_new = (_dot_nt(ql, cn) + _dot_nt(qp, kn)) * scale
    t = lax.broadcasted_iota(jnp.int32, s_new.shape, 0) // heads
    col = lax.broadcasted_iota(jnp.int32, s_new.shape, 1)
    s_new = jnp.where((col <= t) & (col < tokens), s_new, NEG_BIG)
    m = jnp.max(s_new, axis=-1, keepdims=True)
    p_new = jnp.exp(s_new - m)
    l = jnp.sum(p_new, axis=-1, keepdims=True)
    acc = _dot(p_new.astype(BF16), cn)

    def scores(c):
        sl = slice(c * chunk, (c + 1) * chunk)
        kc = ckv_s[sl, :].astype(BF16)
        kbf[sl, :] = kc
        s_scr[:, sl] = (_dot_nt(ql, kc) + _dot(qp, kpe_s[:, sl].astype(BF16))) * scale

    def values(c, m, l, acc):
        sl = slice(c * chunk, (c + 1) * chunk)
        s = s_scr[:, sl]
        m_new = jnp.maximum(m, jnp.max(s, axis=-1, keepdims=True))
        a = jnp.exp(m - m_new)
        p = jnp.exp(s - m_new)
        l = a * l + jnp.sum(p, axis=-1, keepdims=True)
        return m_new, l, a * acc + _dot(p.astype(BF16), kbf[sl, :])

    scores(0)
    for c in range(n_chunks):
        if c + 1 < n_chunks:
            scores(c + 1)
        m, l, acc = values(c, m, l, acc)
    o_ref[0] = acc / l


def _mla_sample(ql, qp, ckv_new, kpe_new, cache_ckv, cache_kpe_t, layer, page_table, tokens, dims):
    heads, d_nope, d_rope, kv_lora = dims["heads"], dims["d_nope"], dims["d_rope"], dims["kv_lora"]
    b, rows, _ = ql.shape
    n_pages = page_table.shape[1]
    page = cache_ckv.shape[2]
    past = n_pages * page
    tp = ckv_new.shape[1]
    per_b = lambda r, c: pl.BlockSpec((1, r, c), lambda i, pt: (i, 0, 0))
    hbm = pl.BlockSpec(memory_space=pl.ANY)
    kern = functools.partial(_mla_sample_kernel, layer=layer, n_pages=n_pages, page=page, chunk=DECODE_CHUNK,
                             tokens=tokens, heads=heads, scale=(d_nope + d_rope) ** -0.5)
    grid_spec = pltpu.PrefetchScalarGridSpec(
        num_scalar_prefetch=1,
        grid=(b,),
        in_specs=[per_b(rows, kv_lora), per_b(rows, d_rope), per_b(tp, kv_lora), per_b(tp, d_rope), hbm, hbm],
        out_specs=per_b(rows, kv_lora),
        scratch_shapes=[pltpu.VMEM((2, past, kv_lora), F32), pltpu.VMEM((2, d_rope, past), F32),
                        pltpu.VMEM((past, kv_lora), BF16), pltpu.VMEM((rows, past), F32),
                        pltpu.SemaphoreType.DMA((2, 2))],
    )
    return pl.pallas_call(
        kern,
        grid_spec=grid_spec,
        out_shape=jax.ShapeDtypeStruct((b, rows, kv_lora), F32),
        compiler_params=_params(1),
        name="mla_sample",
    )(page_table, ql, qp, ckv_new, kpe_new, cache_ckv, cache_kpe_t)


def _mem_kv_kernel(m_ref, g_ref, wk, wv, k_o, v_o):
    mn = _rms(m_ref[0], g_ref[...]).astype(BF16)
    k_o[0] = _dot(mn, wk[...])
    v_o[0] = _dot(mn, wv[...])


def _mem_kv(mem, g, wk, wv):
    b, m, d = mem.shape
    c = wk.shape[1]
    return pl.pallas_call(
        _mem_kv_kernel,
        grid=(b,),
        in_specs=[pl.BlockSpec((1, m, d), lambda i: (i, 0, 0)), _resident(g.shape), _resident(wk.shape),
                  _resident(wv.shape)],
        out_specs=[pl.BlockSpec((1, m, c), lambda i: (i, 0, 0))] * 2,
        out_shape=[jax.ShapeDtypeStruct((b, m, c), F32)] * 2,
        compiler_params=_params(1),
        name="mem_kv",
    )(mem, g, wk, wv)


def _x_attend_heads(q, mk, mv, heads, hd):
    r = q.shape[0]
    lane = lax.broadcasted_iota(jnp.int32, q.shape, 1)
    sels = [(lane >= h * hd) & (lane < (h + 1) * hd) for h in range(heads)]
    q_heads = jnp.concatenate([jnp.where(sel, q, jnp.zeros_like(q)) for sel in sels], axis=0)
    s = _dot_nt(q_heads, mk) * (hd ** -0.5)
    p = jnp.exp(s - jnp.max(s, axis=-1, keepdims=True))
    p = p / jnp.sum(p, axis=-1, keepdims=True)
    pv = _dot(p.astype(BF16), mv)
    out = jnp.zeros(q.shape, F32)
    for h, sel in enumerate(sels):
        out = out + jnp.where(sel, pv[h * r:(h + 1) * r], 0.0)
    return out


def _x_sample_kernel(q_ref, kt_ref, vt_ref, o_ref, *, group, heads, hd):
    tp = q_ref.shape[1]
    lane = lax.broadcasted_iota(jnp.int32, q_ref.shape[1:], 1)
    sels = [(lane >= h * hd) & (lane < (h + 1) * hd) for h in range(heads)]
    for g in range(group):
        q = q_ref[g]
        q_heads = jnp.concatenate([jnp.where(sel, q, 0.0) for sel in sels], axis=0).astype(BF16)
        s = _dot(q_heads, kt_ref[g].astype(BF16)) * (hd ** -0.5)
        p = jnp.exp(s - jnp.max(s, axis=-1, keepdims=True))
        p = p / jnp.sum(p, axis=-1, keepdims=True)
        r = _dot_nt(p.astype(BF16), vt_ref[g].astype(BF16))
        out = jnp.zeros(q.shape, F32)
        for h, sel in enumerate(sels):
            out = out + jnp.where(sel, r[h * tp:(h + 1) * tp], 0.0)
        o_ref[g] = out


def _x_sample(xq, mkt, mvt, heads):
    b, tp, c = xq.shape
    m = mkt.shape[2]
    G = X_SAMPLE_GROUP
    kern = functools.partial(_x_sample_kernel, group=G, heads=heads, hd=c // heads)
    return pl.pallas_call(
        kern,
        grid=(b // G,),
        in_specs=[pl.BlockSpec((G, tp, c), lambda i: (i, 0, 0)),
                  pl.BlockSpec((G, c, m), lambda i: (i, 0, 0)),
                  pl.BlockSpec((G, c, m), lambda i: (i, 0, 0))],
        out_specs=pl.BlockSpec((G, tp, c), lambda i: (i, 0, 0)),
        out_shape=jax.ShapeDtypeStruct((b, tp, c), F32),
        compiler_params=_params(1),
        name="x_attend_sample",
    )(xq, mkt, mvt)


def _merge_kernel(x_ref, gpre, wgate, oret, wret, omla, wmla, wuv, ox, wx, wout, gpost, *rest,
                  heads, kv_lora, d_v, absorbed, x_heads):
    h_o = rest[-1]
    if x_heads:
        mk, mv = rest[:2]
        o_x = _x_attend_heads(ox[...], mk[0].astype(BF16), mv[0].astype(BF16), x_heads, ox.shape[1] // x_heads)
    else:
        o_x = ox[...]
    x = x_ref[...]
    d = x.shape[1]
    u = _rms(x, gpre[...]).astype(BF16)
    a_ret = _dot(oret[...], wret[...])
    if absorbed:
        a_mla = jnp.zeros(x.shape, F32)
        for h in range(heads):
            o_h = _dot(omla[:, h * kv_lora:(h + 1) * kv_lora].astype(BF16), wuv[h]).astype(BF16)
            a_mla = a_mla + _dot(o_h, wmla[h * d_v:(h + 1) * d_v, :])
    else:
        a_mla = _dot(omla[...], wmla[...])
    a_x = _dot(o_x.astype(BF16), wx[...])
    mixed = jnp.zeros(x.shape, F32)
    for i, a in enumerate((a_ret, a_mla, a_x)):
        mixed = mixed + _sigmoid(_dot(u, wgate[:, i * d:(i + 1) * d])) * a
    h_o[...] = x + _rms(_dot(mixed.astype(BF16), wout[...]), gpost[...])


def _merge(x, oret, omla, ox, w, dims, *, absorbed, tm, mem=None):
    n, d = x.shape
    row = lambda a: pl.BlockSpec((tm, a.shape[1]), lambda i: (i, 0))
    res = lambda a: _resident(a.shape)
    args = [x, w["g_pre"], w["w_gates"], oret, w["w_ret_o"], omla, w["w_mla_o"], w["w_uv_h"], ox, w["w_x_o"],
            w["w_out"], w["g_post"]]
    specs = [row(x), res(w["g_pre"]), res(w["w_gates"]), row(oret), res(w["w_ret_o"]), row(omla),
             res(w["w_mla_o"]), res(w["w_uv_h"]), row(ox), res(w["w_x_o"]), res(w["w_out"]), res(w["g_post"])]
    x_heads = 0
    if mem is not None:
        mk, mv, x_heads, seq_rows = mem
        tiles_per_seq = seq_rows // tm
        per_seq = pl.BlockSpec((1,) + mk.shape[1:], lambda i: (i // tiles_per_seq, 0, 0))
        args += [mk, mv]
        specs += [per_seq, per_seq]
    kern = functools.partial(_merge_kernel, heads=dims["heads"], kv_lora=dims["kv_lora"], d_v=dims["d_v"],
                             absorbed=absorbed, x_heads=x_heads)
    return pl.pallas_call(
        kern,
        grid=(n // tm,),
        in_specs=specs,
        out_specs=pl.BlockSpec((tm, d), lambda i: (i, 0)),
        out_shape=jax.ShapeDtypeStruct((n, d), F32),
        compiler_params=_params(1),
        name="merge_sample" if absorbed else "merge_prompt",
    )(*args)


def _ffn_kernel(h_ref, gpre, wg, wu, wd, gpost, y_o):
    h = h_ref[...]
    f = _rms(h, gpre[...]).astype(BF16)
    gate = _dot(f, wg[...])
    act = (gate * _sigmoid(gate) * _dot(f, wu[...])).astype(BF16)
    y_o[...] = h + _rms(_dot(act, wd[...]), gpost[...])


def _ffn(h, w, *, tm):
    n, d = h.shape
    res = lambda a: _resident(a.shape)
    return pl.pallas_call(
        _ffn_kernel,
        grid=(n // tm,),
        in_specs=[pl.BlockSpec((tm, d), lambda i: (i, 0)), res(w["g_ffn_pre"]), res(w["w_ffn_gate"]),
                  res(w["w_ffn_up"]), res(w["w_ffn_down"]), res(w["g_ffn_post"])],
        out_specs=pl.BlockSpec((tm, d), lambda i: (i, 0)),
        out_shape=jax.ShapeDtypeStruct((n, d), F32),
        compiler_params=_params(1),
        name="ffn",
    )(h, w["g_ffn_pre"], w["w_ffn_gate"], w["w_ffn_up"], w["w_ffn_down"], w["g_ffn_post"])


def _rope_tables(pos, ret_dk, d_rope):
    posf = pos.astype(F32)[:, None]

    def angles(half):
        inv = ROPE_BASE ** (-jnp.arange(half, dtype=F32) / half)
        ang = posf * inv[None, :]
        return jnp.cos(ang), jnp.sin(ang)

    cr, sr = angles(ret_dk // 2)
    cp, sp = angles(d_rope // 2)
    z = jnp.zeros_like(cp)
    pad = jnp.zeros((pos.shape[0], LANES - d_rope), F32)
    return {
        "cosr": jnp.concatenate([cr, cr], axis=1),
        "sinr": jnp.concatenate([-sr, sr], axis=1),
        "cosp": jnp.concatenate([cp, cp, pad], axis=1),
        "sinlo": jnp.concatenate([-sp, z, pad], axis=1),
        "sinhi": jnp.concatenate([z, sp, pad], axis=1),
    }


def _layer_weights(l, dims, sizes, norm_mix_pre, norm_mix_post, norm_ffn_pre, norm_ffn_post, norm_mem, norm_q_lat,
                   norm_kv_lat, w_in, w_uq, w_uk, w_uv, w_mem_k, w_mem_v, w_ret_o, w_mla_o, w_x_o, w_out,
                   w_ffn_gate, w_ffn_up, w_ffn_down):
    heads, d_nope, d_rope, kv_lora, d_v = (dims[k] for k in ("heads", "d_nope", "d_rope", "kv_lora", "d_v"))
    bf = lambda a: a.astype(BF16)
    gain = lambda a: a[l].astype(F32)[None, :]
    offs = np.concatenate([[0], np.cumsum(sizes)])
    seg = [w_in[l][:, offs[i]:offs[i + 1]] for i in range(len(sizes))]
    q_lora = w_uq.shape[1]
    uq = w_uq[l].reshape(q_lora, heads, d_nope + d_rope)
    uq_p = jnp.pad(uq[:, :, d_nope:], ((0, 0), (0, 0), (0, LANES - d_rope)))
    return {
        "g_pre": gain(norm_mix_pre), "g_post": gain(norm_mix_post), "g_ffn_pre": gain(norm_ffn_pre),
        "g_ffn_post": gain(norm_ffn_post), "g_mem": gain(norm_mem), "g_q": gain(norm_q_lat),
        "g_kv": gain(norm_kv_lat),
        "w_rq": bf(seg[0]), "w_rk": bf(seg[1]), "w_rv": bf(seg[2]), "w_rg": bf(seg[3]), "w_cq": bf(seg[4]),
        "w_ckv": bf(seg[5]), "w_kpe": bf(jnp.pad(seg[6], ((0, 0), (0, LANES - d_rope)))), "w_xq": bf(seg[7]),
        "w_gates": bf(seg[8]),
        "w_uq_n": bf(uq[:, :, :d_nope].reshape(q_lora, heads * d_nope)),
        "w_uq_p": bf(uq_p.reshape(q_lora, heads * LANES)),
        "w_uk_t": bf(jnp.swapaxes(w_uk[l], 1, 2)),
        "w_kn": bf(jnp.swapaxes(w_uk[l], 0, 1).reshape(kv_lora, heads * d_nope)),
        "w_vn": bf(jnp.swapaxes(w_uv[l], 0, 1).reshape(kv_lora, heads * d_v)),
        "w_uv_h": bf(w_uv[l]),
        "w_mem_k": bf(w_mem_k[l]), "w_mem_v": bf(w_mem_v[l]),
        "w_ret_o": bf(w_ret_o[l]), "w_mla_o": bf(w_mla_o[l]), "w_x_o": bf(w_x_o[l]), "w_out": bf(w_out[l]),
        "w_ffn_gate": bf(w_ffn_gate[l]), "w_ffn_up": bf(w_ffn_up[l]), "w_ffn_down": bf(w_ffn_down[l]),
    }


def _pad_tokens(a, b, tokens, rows=SAMPLE_TOK_PAD):
    a = a.reshape(b, tokens, a.shape[-1])
    return jnp.pad(a, ((0, 0), (0, rows - tokens), (0, 0)))


def kernel(x_prompt, x_sample, mem_prompt, cache_ckv, cache_kpe, page_table, state_ret, cache_mem_k, cache_mem_v,
           norm_mix_pre, norm_mix_post, norm_ffn_pre, norm_ffn_post, norm_mem, norm_q_lat, norm_kv_lat, w_in,
           w_uq, w_uk, w_uv, w_mem_k, w_mem_v, w_ret_o, w_mla_o, w_x_o, w_out, w_ffn_gate, w_ffn_up, w_ffn_down):
    depth = w_in.shape[0]
    batch, seq, d_model = x_prompt.shape
    db, tokens, _ = x_sample.shape
    ret_heads, ret_dk, ret_dv = state_ret.shape[2:]
    heads, kv_lora, d_nope = w_uk.shape[1:]
    d_rope = cache_kpe.shape[-1]
    d_v = w_uv.shape[-1]
    n_mem, x_heads, x_hd = cache_mem_k.shape[2:]
    q_lora = w_uq.shape[1]
    past_len = page_table.shape[1] * cache_ckv.shape[2]
    dims = dict(heads=heads, d_nope=d_nope, d_rope=d_rope, kv_lora=kv_lora, d_v=d_v,
                ret_heads=ret_heads, ret_dk=ret_dk, ret_dv=ret_dv)
    assert ret_dk == LANES and d_nope == LANES and d_rope <= LANES // 2 and tokens <= SAMPLE_TOK_PAD
    sizes = (ret_heads * ret_dk, ret_heads * ret_dk, ret_heads * ret_dv, ret_heads * ret_dv, q_lora, kv_lora,
             d_rope, x_heads * x_hd, w_in.shape[2] - (2 * ret_heads * ret_dk + 2 * ret_heads * ret_dv + q_lora
                                                       + kv_lora + d_rope + x_heads * x_hd))

    cache_kpe_t = jnp.swapaxes(cache_kpe, 2, 3)

    tabs_p = _rope_tables(jnp.arange(seq), ret_dk, d_rope)
    n_s = db * tokens
    tabs_s = _rope_tables(past_len + (jnp.arange(n_s) % tokens), ret_dk, d_rope)

    y_p = x_prompt.reshape(batch * seq, d_model)
    y_s = x_sample.reshape(n_s, d_model)
    outs = [[] for _ in range(8)]
    for l in range(depth):
        w = _layer_weights(l, dims, sizes, norm_mix_pre, norm_mix_post, norm_ffn_pre, norm_ffn_post, norm_mem,
                           norm_q_lat, norm_kv_lat, w_in, w_uq, w_uk, w_uv, w_mem_k, w_mem_v, w_ret_o, w_mla_o,
                           w_x_o, w_out, w_ffn_gate, w_ffn_up, w_ffn_down)

        mk_p, mv_p = _mem_kv(mem_prompt, w["g_mem"], w["w_mem_k"], w["w_mem_v"])
        rq, rk, rv, rg, qcat, xq, ckv_p, kpe_p, kcat, vn = _in_proj(y_p, w, tabs_p, dims, absorbed=False,
                                                                    tm=ROW_TILE)
        o_ret, ret_p = _ret_prompt(rq, rk, rv, rg, batch, seq, dims)
        o_mla = _mla_prompt(qcat, kcat, vn, batch, seq, dims)
        h_p = _merge(y_p, o_ret, o_mla, xq, w, dims, absorbed=False, tm=ROW_TILE, mem=(mk_p, mv_p, x_heads, seq))
        y_p = _ffn(h_p, w, tm=ROW_TILE)

        rq, rk, rv, rg, qcat, xq, ckv_s, kpe_s, qlat = _in_proj(y_s, w, tabs_s, dims, absorbed=True, tm=ROW_TILE)
        pad = lambda a: _pad_tokens(a, db, tokens)
        o_ret, ret_s = _ret_sample(pad(rq), pad(rk), pad(rv), pad(rg), state_ret[l].astype(F32), tokens, dims)
        o_ret = o_ret[:, :tokens].reshape(n_s, ret_heads * ret_dv)
        group = d_nope + LANES
        q_pe = qcat.reshape(db, tokens * heads, group)[:, :, d_nope:d_nope + d_rope]
        o_lat = _mla_sample(qlat.reshape(db, tokens * heads, kv_lora), q_pe, pad(ckv_s), pad(kpe_s),
                            cache_ckv, cache_kpe_t, l, page_table, tokens, dims)
        o_lat = o_lat.reshape(n_s, heads * kv_lora)
        mem_t = lambda c: jnp.transpose(c[l], (0, 2, 3, 1)).reshape(db, x_heads * x_hd, n_mem)
        o_x = _x_sample(_pad_tokens(xq.astype(F32), db, tokens, rows=X_SAMPLE_TOK_PAD), mem_t(cache_mem_k),
                        mem_t(cache_mem_v), x_heads)
        o_x = o_x[:, :tokens].reshape(n_s, x_heads * x_hd)
        h_s = _merge(y_s, o_ret, o_lat, o_x, w, dims, absorbed=True, tm=ROW_TILE)
        y_s = _ffn(h_s, w, tm=ROW_TILE)

        for lst, val in zip(outs, (ckv_p.reshape(batch, seq, kv_lora), kpe_p.reshape(batch, seq, d_rope),
                                   ckv_s.reshape(db, tokens, kv_lora), kpe_s.reshape(db, tokens, d_rope),
                                   ret_p.astype(x_prompt.dtype), ret_s.astype(state_ret.dtype),
                                   mk_p.reshape(batch, n_mem, x_heads, x_hd),
                                   mv_p.reshape(batch, n_mem, x_heads, x_hd))):
            lst.append(val)

    return (y_p.reshape(batch, seq, d_model), y_s.reshape(db, tokens, d_model)) + tuple(jnp.stack(o) for o in outs)
```

```python
import functools
import math

import numpy as np
import jax
import jax.numpy as jnp
from jax import lax
from jax.experimental import pallas as pl
from jax.experimental.pallas import tpu as pltpu

F32 = jnp.float32
BF16 = jnp.bfloat16

ROPE_BASE = 10000.0
RMS_EPS = 1e-6
LANES = 128
VMEM_LIMIT = 56 * 1024 * 1024
NEG_BIG = -1e30

ROW_TILE = 512
RET_CHUNK = 256
ATTN_Q_TILE = 512
ATTN_KV_TILE = 512
ATTN_HEADS_PER_STEP = 8
SAMPLE_TOK_PAD = 16
X_SAMPLE_TOK_PAD = 8
RET_SAMPLE_GROUP = 8
X_SAMPLE_GROUP = 8
DECODE_CHUNK = 4096
DMA_LOOP_UNROLL = 4


def _resident(shape):
    nd = len(shape)
    return pl.BlockSpec(shape, lambda *_: (0,) * nd, pipeline_mode=pl.Buffered(1))


def _params(n_axes, flags=None):
    return pltpu.CompilerParams(dimension_semantics=("arbitrary",) * n_axes, vmem_limit_bytes=VMEM_LIMIT,
                                flags=flags)


def _rms(x, g=None):
    y = x * lax.rsqrt(jnp.mean(x * x, axis=-1, keepdims=True) + RMS_EPS)
    return y if g is None else y * g


def _sigmoid(x):
    return 1.0 / (1.0 + jnp.exp(-x))


def _dot(a, b):
    return jnp.dot(a, b, preferred_element_type=F32)


def _dot_nt(a, b):
    return lax.dot_general(a, b, (((1,), (1,)), ((), ())), preferred_element_type=F32)


def _dot_tn(a, b):
    return lax.dot_general(a, b, (((0,), (0,)), ((), ())), preferred_element_type=F32)


def _rope_half_vreg(z, cos_t, sin_lo, sin_hi, quarter):
    return (z * cos_t + pltpu.roll(z, LANES - quarter, 1) * sin_lo + pltpu.roll(z, quarter, 1) * sin_hi)


def _in_proj_kernel(x_ref, g_ref, wrq, wrk, wrv, wrg, wcq, wckv, wkpe, wxq, gq_ref, gkv_ref, wuqn, wuqp,
                    cosr, sinr, cosp, sinlo, sinhi, *rest,
                    ret_heads, ret_dk, heads, d_nope, d_rope, kv_lora, absorbed, q_scale):
    if absorbed:
        wa, (rq_o, rk_o, rv_o, rg_o, qcat_o, xq_o, ckv_o, kpe_o, a_o) = rest[0], rest[1:]
    else:
        wa, wb, (rq_o, rk_o, rv_o, rg_o, qcat_o, xq_o, ckv_o, kpe_o, a_o, b_o) = rest[0], rest[1], rest[2:]
    u = _rms(x_ref[...], g_ref[...]).astype(BF16)
    cr, sr = cosr[...], sinr[...]
    cp, slo, shi = cosp[...], sinlo[...], sinhi[...]
    group = d_nope + LANES
    k_scale = ret_dk ** -0.5

    cq = _dot(u, wcq[...])
    ckv = _dot(u, wckv[...])
    kpe_raw = _dot(u, wkpe[...])
    zq = _dot(u, wrq[...])
    zk = _dot(u, wrk[...])

    cqn = _rms(cq, gq_ref[...]).astype(BF16)
    ckvn = _rms(ckv, gkv_ref[...])
    ckv_o[...] = ckvn
    kpe = _rope_half_vreg(kpe_raw, cp, slo, shi, d_rope // 2)
    kpe_o[...] = kpe[:, :d_rope]

    qn = _dot(cqn, wuqn[...])
    qp = _dot(cqn, wuqp[...])
    if not absorbed:
        ckvb = ckvn.astype(BF16)
        kn = _dot(ckvb, wa[...])
        vn = _dot(ckvb, wb[...])

    for h in range(ret_heads):
        sl = slice(h * ret_dk, (h + 1) * ret_dk)
        q_h = zq[:, sl]
        k_h = zk[:, sl]
        rq_o[:, sl] = (q_h * cr + pltpu.roll(q_h, ret_dk // 2, 1) * sr).astype(BF16)
        rk_o[:, sl] = ((k_h * cr + pltpu.roll(k_h, ret_dk // 2, 1) * sr) * k_scale).astype(BF16)

    rv = _dot(u, wrv[...])
    rg = _dot(u, wrg[...])
    xq = _dot(u, wxq[...])

    for h in range(heads):
        qn_h = qn[:, h * d_nope:(h + 1) * d_nope]
        qp_h = _rope_half_vreg(qp[:, h * LANES:(h + 1) * LANES], cp, slo, shi, d_rope // 2)
        if absorbed:
            a_o[:, h * kv_lora:(h + 1) * kv_lora] = _dot(qn_h.astype(BF16), wa[h]).astype(BF16)
        else:
            qn_h = qn_h * q_scale
            qp_h = qp_h * q_scale
        qcat_o[:, h * group:h * group + d_nope] = qn_h.astype(BF16)
        qcat_o[:, h * group + d_nope:(h + 1) * group] = qp_h.astype(BF16)
    if not absorbed:
        kpeb = kpe.astype(BF16)
        for h in range(heads):
            a_o[:, h * group:h * group + d_nope] = kn[:, h * d_nope:(h + 1) * d_nope].astype(BF16)
            a_o[:, h * group + d_nope:(h + 1) * group] = kpeb
        b_o[...] = vn.astype(BF16)

    rv_o[...] = rv.astype(BF16)
    rg_o[...] = rg.astype(BF16)
    xq_o[...] = xq.astype(BF16)


def _in_proj(x, w, tabs, dims, *, absorbed, tm):
    n, d_model = x.shape
    heads, d_nope, d_rope, kv_lora = dims["heads"], dims["d_nope"], dims["d_rope"], dims["kv_lora"]
    ret_heads, ret_dk, ret_dv = dims["ret_heads"], dims["ret_dk"], dims["ret_dv"]
    group = d_nope + LANES
    tab_rows = tabs["cosr"].shape[0]
    tab_tiles = tab_rows // tm
    row = lambda c: pl.BlockSpec((tm, c), lambda i: (i, 0))
    tab = lambda: pl.BlockSpec((tm, LANES), lambda i: (i % tab_tiles, 0))
    weights = [w["g_pre"], w["w_rq"], w["w_rk"], w["w_rv"], w["w_rg"], w["w_cq"], w["w_ckv"], w["w_kpe"],
               w["w_xq"], w["g_q"], w["g_kv"], w["w_uq_n"], w["w_uq_p"]]
    out_cols = [(ret_heads * ret_dk, BF16), (ret_heads * ret_dk, BF16), (ret_heads * ret_dv, BF16),
                (ret_heads * ret_dv, BF16), (heads * group, BF16), (w["w_xq"].shape[1], BF16), (kv_lora, F32),
                (d_rope, F32)]
    if absorbed:
        mode_weights = [w["w_uk_t"]]
        out_cols += [(heads * kv_lora, BF16)]
    else:
        mode_weights = [w["w_kn"], w["w_vn"]]
        out_cols += [(heads * group, BF16), (w["w_vn"].shape[1], BF16)]
    kern = functools.partial(_in_proj_kernel, ret_heads=ret_heads, ret_dk=ret_dk, heads=heads, d_nope=d_nope,
                             d_rope=d_rope, kv_lora=kv_lora, absorbed=absorbed,
                             q_scale=(d_nope + d_rope) ** -0.5 * math.log2(math.e))
    return pl.pallas_call(
        kern,
        grid=(n // tm,),
        in_specs=([row(d_model)] + [_resident(a.shape) for a in weights] + [tab() for _ in range(5)]
                  + [_resident(a.shape) for a in mode_weights]),
        out_specs=[row(c) for c, _ in out_cols],
        out_shape=[jax.ShapeDtypeStruct((n, c), dt) for c, dt in out_cols],
        compiler_params=_params(1),
        name="in_proj_sample" if absorbed else "in_proj_prompt",
    )(x, *weights, tabs["cosr"], tabs["sinr"], tabs["cosp"], tabs["sinlo"], tabs["sinhi"], *mode_weights)


def _ret_head(q, k, v, g, s, dec, qd, kd, g_l):
    inner = _dot_nt(q, k) * dec
    o = _dot(inner.astype(BF16), v) + _dot((q.astype(F32) * qd).astype(BF16), s.astype(BF16))
    s_new = s * g_l + _dot_tn((k.astype(F32) * kd).astype(BF16), v)
    gf = g.astype(F32)
    o = (gf * _sigmoid(gf)) * _rms(o)
    return o.astype(BF16), s_new


def _ret_prompt_kernel(rq, rk, rv, rg, dec, qd, kd, o_ref, s_out, s_scr, *, heads, dk, dv, g_l):
    c = pl.program_id(1)

    @pl.when(c == 0)
    def _():
        s_scr[...] = jnp.zeros(s_scr.shape, F32)

    for h in range(heads):
        o, s_new = _ret_head(rq[:, h * dk:(h + 1) * dk], rk[:, h * dk:(h + 1) * dk],
                             rv[:, h * dv:(h + 1) * dv], rg[:, h * dv:(h + 1) * dv],
                             s_scr[h], dec[h], qd[h], kd[h], g_l[h])
        o_ref[:, h * dv:(h + 1) * dv] = o
        s_scr[h] = s_new

    @pl.when(c == pl.num_programs(1) - 1)
    def _():
        s_out[0] = s_scr[...]


def _ret_consts(heads, length, dk, chunk_rows=None):
    rows = length if chunk_rows is None else chunk_rows
    lg = np.log1p(-np.exp2(-5.0 - np.arange(heads, dtype=np.float64)))
    i = np.arange(rows, dtype=np.float64)
    diff = i[:, None] - i[None, :]
    valid = (diff >= 0) & (i[:, None] < length) & (i[None, :] < length)
    dec = np.where(valid[None], np.exp(np.maximum(diff, 0.0)[None] * lg[:, None, None]), 0.0)
    qd = np.exp((i[None, :] + 1.0) * lg[:, None])
    kd = np.where(i[None, :] < length, np.exp((length - 1.0 - i[None, :]) * lg[:, None]), 0.0)
    qd = np.broadcast_to(qd[:, :, None], (heads, rows, dk))
    kd = np.broadcast_to(kd[:, :, None], (heads, rows, dk))
    g_l = tuple(float(v) for v in np.exp(length * lg))
    return (jnp.asarray(dec, F32), jnp.asarray(qd, F32), jnp.asarray(kd, F32), g_l)


def _ret_prompt(rq, rk, rv, rg, batch, seq, dims):
    heads, dk, dv = dims["ret_heads"], dims["ret_dk"], dims["ret_dv"]
    L = RET_CHUNK
    nc = seq // L
    dec, qd, kd, g_l = _ret_consts(heads, L, dk)
    row = lambda c: pl.BlockSpec((L, c), lambda b, i: (b * nc + i, 0))
    kern = functools.partial(_ret_prompt_kernel, heads=heads, dk=dk, dv=dv, g_l=g_l)
    return pl.pallas_call(
        kern,
        grid=(batch, nc),
        in_specs=[row(heads * dk), row(heads * dk), row(heads * dv), row(heads * dv),
                  _resident(dec.shape), _resident(qd.shape), _resident(kd.shape)],
        out_specs=[row(heads * dv), pl.BlockSpec((1, heads, dk, dv), lambda b, i: (b, 0, 0, 0))],
        out_shape=[jax.ShapeDtypeStruct((batch * seq, heads * dv), BF16),
                   jax.ShapeDtypeStruct((batch, heads, dk, dv), F32)],
        scratch_shapes=[pltpu.VMEM((heads, dk, dv), F32)],
        compiler_params=_params(2),
        name="retention_prompt",
    )(rq, rk, rv, rg, dec, qd, kd)


def _ret_sample_kernel(rq, rk, rv, rg, s0, dec, qd, kd, o_ref, s_out, *, group, heads, dk, dv, g_l):
    pairs = [(i, h) for i in range(group) for h in range(heads)]

    def state_side(i, h):
        q = rq[i, :, h * dk:(h + 1) * dk]
        k = rk[i, :, h * dk:(h + 1) * dk]
        v = rv[i, :, h * dv:(h + 1) * dv]
        s = s0[i, h]
        read = _dot((q.astype(F32) * qd[h]).astype(BF16), s.astype(BF16))
        s_out[i, h] = s * g_l[h] + _dot_tn((k.astype(F32) * kd[h]).astype(BF16), v)
        return read

    def output_side(i, h, read):
        q = rq[i, :, h * dk:(h + 1) * dk]
        k = rk[i, :, h * dk:(h + 1) * dk]
        v = rv[i, :, h * dv:(h + 1) * dv]
        o = _dot((_dot_nt(q, k) * dec[h]).astype(BF16), v) + read
        gf = rg[i, :, h * dv:(h + 1) * dv].astype(F32)
        o_ref[i, :, h * dv:(h + 1) * dv] = ((gf * _sigmoid(gf)) * _rms(o)).astype(BF16)

    read = state_side(*pairs[0])
    for n, (i, h) in enumerate(pairs):
        nxt = state_side(*pairs[n + 1]) if n + 1 < len(pairs) else None
        output_side(i, h, read)
        read = nxt


def _ret_sample(rq, rk, rv, rg, state, tokens, dims):
    heads, dk, dv = dims["ret_heads"], dims["ret_dk"], dims["ret_dv"]
    b, tp, _ = rq.shape
    G = RET_SAMPLE_GROUP
    dec, qd, kd, g_l = _ret_consts(heads, tokens, dk, chunk_rows=tp)
    blk = lambda c: pl.BlockSpec((G, tp, c), lambda i: (i, 0, 0))
    st = pl.BlockSpec((G, heads, dk, dv), lambda i: (i, 0, 0, 0))
    kern = functools.partial(_ret_sample_kernel, group=G, heads=heads, dk=dk, dv=dv, g_l=g_l)
    return pl.pallas_call(
        kern,
        grid=(b // G,),
        in_specs=[blk(heads * dk), blk(heads * dk), blk(heads * dv), blk(heads * dv), st,
                  _resident(dec.shape), _resident(qd.shape), _resident(kd.shape)],
        out_specs=[blk(heads * dv), st],
        out_shape=[jax.ShapeDtypeStruct((b, tp, heads * dv), BF16),
                   jax.ShapeDtypeStruct((b, heads, dk, dv), F32)],
        compiler_params=_params(1),
        name="retention_sample",
    )(rq, rk, rv, rg, state, dec, qd, kd)


def _mla_prompt_kernel(q_ref, k_ref, v_ref, o_ref, m_scr, l_scr, acc_scr, *, tq, tk, hp, group, d_v):
    qi = pl.program_id(2)
    ri = lax.broadcasted_iota(jnp.int32, (tq, tk), 0)
    ci = lax.broadcasted_iota(jnp.int32, (tq, tk), 1)
    causal_bias = jnp.where(ci <= ri, 0.0, NEG_BIG)

    def block(j, carry, first):
        start = pl.multiple_of(j * tk, tk)

        def scores(h):
            return _dot_nt(q_ref[:, h * group:(h + 1) * group], k_ref[pl.ds(start, tk), h * group:(h + 1) * group])

        s_next = scores(0)
        for h in range(hp):
            s = s_next
            if h + 1 < hp:
                s_next = scores(h + 1)
            vals = v_ref[pl.ds(start, tk), h * d_v:(h + 1) * d_v]
            if first:
                s = s + causal_bias
                m_new = jnp.broadcast_to(jnp.max(s, axis=-1, keepdims=True), (tq, LANES))
                p = jnp.exp2(s - jnp.tile(m_new, (1, tk // LANES)))
                l_scr[h] = jnp.broadcast_to(jnp.sum(p, axis=-1, keepdims=True), (tq, LANES))
                acc_scr[h] = _dot(p.astype(BF16), vals)
            else:
                m_prev = m_scr[h]
                m_new = jnp.maximum(m_prev, jnp.max(s, axis=-1, keepdims=True))
                a = jnp.exp2(m_prev - m_new)
                p = jnp.exp2(s - jnp.tile(m_new, (1, tk // LANES)))
                l_scr[h] = a * l_scr[h] + jnp.sum(p, axis=-1, keepdims=True)
                acc_scr[h] = a * acc_scr[h] + _dot(p.astype(BF16), vals)
            m_scr[h] = m_new
        return carry

    block(qi, 0, True)
    lax.fori_loop(0, qi, functools.partial(block, first=False), 0)
    for h in range(hp):
        o_ref[:, h * d_v:(h + 1) * d_v] = (acc_scr[h] / l_scr[h]).astype(BF16)


def _mla_prompt(qcat, kcat, v, batch, seq, dims):
    heads, d_nope, d_v = dims["heads"], dims["d_nope"], dims["d_v"]
    group = d_nope + LANES
    t = ATTN_Q_TILE
    hp = ATTN_HEADS_PER_STEP
    nq = seq // t
    assert ATTN_KV_TILE == t and d_v == LANES
    kern = functools.partial(_mla_prompt_kernel, tq=t, tk=ATTN_KV_TILE, hp=hp, group=group, d_v=d_v)
    return pl.pallas_call(
        kern,
        grid=(batch, heads // hp, nq),
        in_specs=[pl.BlockSpec((t, hp * group), lambda b, h, i: (b * nq + i, h)),
                  pl.BlockSpec((seq, hp * group), lambda b, h, i: (b, h)),
                  pl.BlockSpec((seq, hp * d_v), lambda b, h, i: (b, h))],
        out_specs=pl.BlockSpec((t, hp * d_v), lambda b, h, i: (b * nq + i, h)),
        out_shape=jax.ShapeDtypeStruct((batch * seq, heads * d_v), BF16),
        scratch_shapes=[pltpu.VMEM((hp, t, LANES), F32), pltpu.VMEM((hp, t, LANES), F32),
                        pltpu.VMEM((hp, t, d_v), F32)],
        compiler_params=_params(3),
        name="mla_prompt",
    )(qcat, kcat, v)


def _mla_sample_kernel(pt_ref, ql_ref, qp_ref, cn_ref, kn_ref, ckv_hbm, kpe_hbm, o_ref,
                       ckv_buf, kpe_buf, kbf, s_scr, sem,
                       *, layer, n_pages, page, chunk, tokens, heads, scale):
    b = pl.program_id(0)
    past = n_pages * page

    def page_copy(seq, slot, p, array):
        pg = pt_ref[seq, p]
        off = pl.multiple_of(p * page, page)
        if array == 0:
            return pltpu.make_async_copy(ckv_hbm.at[layer, pg], ckv_buf.at[slot, pl.ds(off, page), :],
                                         sem.at[0, slot])
        return pltpu.make_async_copy(kpe_hbm.at[layer, pg], kpe_buf.at[slot, :, pl.ds(off, page)], sem.at[1, slot])

    def for_each_page(seq, slot, fn, arrays=(0, 1), unroll=DMA_LOOP_UNROLL):
        def body(p, carry):
            for a in arrays:
                fn(page_copy(seq, slot, p, a))
            return carry
        lax.fori_loop(0, n_pages, body, 0, unroll=unroll)

    @pl.when(b == 0)
    def _():
        for_each_page(0, 0, lambda cp: cp.start())

    @pl.when(b + 1 < pl.num_programs(0))
    def _():
        for_each_page(b + 1, (b + 1) % 2, lambda cp: cp.start())

    slot = b % 2
    for_each_page(b, slot, lambda cp: cp.wait(), arrays=(0,), unroll=True)
    for_each_page(b, slot, lambda cp: cp.wait(), arrays=(1,), unroll=True)

    ql = ql_ref[0]
    qp = qp_ref[0]
    ckv_s = ckv_buf.at[slot]
    kpe_s = kpe_buf.at[slot]
    n_chunks = past // chunk

    cn = cn_ref[0].astype(BF16)
    kn = kn_ref[0].astype(BF16)
    s_new = (_dot_nt(ql, cn) + _dot_nt(qp, kn)) * scale
    t = lax.broadcasted_iota(jnp.int32, s_new.shape, 0) // heads
    col = lax.broadcasted_iota(jnp.int32, s_new.shape, 1)
    s_new = jnp.where((col <= t) & (col < tokens), s_new, NEG_BIG)
    m = jnp.max(s_new, axis=-1, keepdims=True)
    p_new = jnp.exp(s_new - m)
    l = jnp.sum(p_new, axis=-1, keepdims=True)
    acc = _dot(p_new.astype(BF16), cn)

    def scores(c):
        sl = slice(c * chunk, (c + 1) * chunk)
        kc = ckv_s[sl, :].astype(BF16)
        kbf[sl, :] = kc
        s_scr[:, sl] = (_dot_nt(ql, kc) + _dot(qp, kpe_s[:, sl].astype(BF16))) * scale

    def values(c, m, l, acc):
        sl = slice(c * chunk, (c + 1) * chunk)
        s = s_scr[:, sl]
        m_new = jnp.maximum(m, jnp.max(s, axis=-1, keepdims=True))
        a = jnp.exp(m - m_new)
        p = jnp.exp(s - m_new)
        l = a * l + jnp.sum(p, axis=-1, keepdims=True)
        return m_new, l, a * acc + _dot(p.astype(BF16), kbf[sl, :])

    scores(0)
    for c in range(n_chunks):
        if c + 1 < n_chunks:
            scores(c + 1)
        m, l, acc = values(c, m, l, acc)
    o_ref[0] = acc / l


def _mla_sample(ql, qp, ckv_new, kpe_new, cache_ckv, cache_kpe_t, layer, page_table, tokens, dims):
    heads, d_nope, d_rope, kv_lora = dims["heads"], dims["d_nope"], dims["d_rope"], dims["kv_lora"]
    b, rows, _ = ql.shape
    n_pages = page_table.shape[1]
    page = cache_ckv.shape[2]
    past = n_pages * page
    tp = ckv_new.shape[1]
    per_b = lambda r, c: pl.BlockSpec((1, r, c), lambda i, pt: (i, 0, 0))
    hbm = pl.BlockSpec(memory_space=pl.ANY)
    kern = functools.partial(_mla_sample_kernel, layer=layer, n_pages=n_pages, page=page, chunk=DECODE_CHUNK,
                             tokens=tokens, heads=heads, scale=(d_nope + d_rope) ** -0.5)
    grid_spec = pltpu.PrefetchScalarGridSpec(
        num_scalar_prefetch=1,
        grid=(b,),
        in_specs=[per_b(rows, kv_lora), per_b(rows, d_rope), per_b(tp, kv_lora), per_b(tp, d_rope), hbm, hbm],
        out_specs=per_b(rows, kv_lora),
        scratch_shapes=[pltpu.VMEM((2, past, kv_lora), F32), pltpu.VMEM((2, d_rope, past), F32),
                        pltpu.VMEM((past, kv_lora), BF16), pltpu.VMEM((rows, past), F32),
                        pltpu.SemaphoreType.DMA((2, 2))],
    )
    return pl.pallas_call(
        kern,
        grid_spec=grid_spec,
        out_shape=jax.ShapeDtypeStruct((b, rows, kv_lora), F32),
        compiler_params=_params(1),
        name="mla_sample",
    )(page_table, ql, qp, ckv_new, kpe_new, cache_ckv, cache_kpe_t)


def _mem_kv_kernel(m_ref, g_ref, wk, wv, k_o, v_o):
    mn = _rms(m_ref[0], g_ref[...]).astype(BF16)
    k_o[0] = _dot(mn, wk[...])
    v_o[0] = _dot(mn, wv[...])


def _mem_kv(mem, g, wk, wv):
    b, m, d = mem.shape
    c = wk.shape[1]
    return pl.pallas_call(
        _mem_kv_kernel,
        grid=(b,),
        in_specs=[pl.BlockSpec((1, m, d), lambda i: (i, 0, 0)), _resident(g.shape), _resident(wk.shape),
                  _resident(wv.shape)],
        out_specs=[pl.BlockSpec((1, m, c), lambda i: (i, 0, 0))] * 2,
        out_shape=[jax.ShapeDtypeStruct((b, m, c), F32)] * 2,
        compiler_params=_params(1),
        name="mem_kv",
    )(mem, g, wk, wv)


def _x_attend_heads(q, mk, mv, heads, hd):
    r = q.shape[0]
    lane = lax.broadcasted_iota(jnp.int32, q.shape, 1)
    sels = [(lane >= h * hd) & (lane < (h + 1) * hd) for h in range(heads)]
    q_heads = jnp.concatenate([jnp.where(sel, q, jnp.zeros_like(q)) for sel in sels], axis=0)
    s = _dot_nt(q_heads, mk) * (hd ** -0.5)
    p = jnp.exp(s - jnp.max(s, axis=-1, keepdims=True))
    p = p / jnp.sum(p, axis=-1, keepdims=True)
    pv = _dot(p.astype(BF16), mv)
    out = jnp.zeros(q.shape, F32)
    for h, sel in enumerate(sels):
        out = out + jnp.where(sel, pv[h * r:(h + 1) * r], 0.0)
    return out


def _x_sample_kernel(q_ref, kt_ref, vt_ref, o_ref, *, group, heads, hd):
    tp = q_ref.shape[1]
    lane = lax.broadcasted_iota(jnp.int32, q_ref.shape[1:], 1)
    sels = [(lane >= h * hd) & (lane < (h + 1) * hd) for h in range(heads)]
    def scores(g):
        q_heads = jnp.concatenate([jnp.where(sel, q_ref[g], 0.0) for sel in sels], axis=0).astype(BF16)
        return _dot(q_heads, kt_ref[g].astype(BF16)) * (hd ** -0.5)

    s_next = scores(0)
    for g in range(group):
        q = q_ref[g]
        s = s_next
        if g + 1 < group:
            s_next = scores(g + 1)
        p = jnp.exp(s - jnp.max(s, axis=-1, keepdims=True))
        p = p / jnp.sum(p, axis=-1, keepdims=True)
        r = _dot_nt(p.astype(BF16), vt_ref[g].astype(BF16))
        out = jnp.zeros(q.shape, F32)
        for h, sel in enumerate(sels):
            out = out + jnp.where(sel, r[h * tp:(h + 1) * tp], 0.0)
        o_ref[g] = out


def _x_sample(xq, mkt, mvt, heads):
    b, tp, c = xq.shape
    m = mkt.shape[2]
    G = X_SAMPLE_GROUP
    kern = functools.partial(_x_sample_kernel, group=G, heads=heads, hd=c // heads)
    return pl.pallas_call(
        kern,
        grid=(b // G,),
        in_specs=[pl.BlockSpec((G, tp, c), lambda i: (i, 0, 0)),
                  pl.BlockSpec((G, c, m), lambda i: (i, 0, 0)),
                  pl.BlockSpec((G, c, m), lambda i: (i, 0, 0))],
        out_specs=pl.BlockSpec((G, tp, c), lambda i: (i, 0, 0)),
        out_shape=jax.ShapeDtypeStruct((b, tp, c), F32),
        compiler_params=_params(1),
        name="x_attend_sample",
    )(xq, mkt, mvt)


def _merge_kernel(x_ref, gpre, wgate, oret, wret, omla, wmla, wuv, ox, wx, wout, gpost, *rest,
                  heads, kv_lora, d_v, absorbed, x_heads):
    h_o = rest[-1]
    if x_heads:
        mk, mv = rest[:2]
        o_x = _x_attend_heads(ox[...], mk[0].astype(BF16), mv[0].astype(BF16), x_heads, ox.shape[1] // x_heads)
    else:
        o_x = ox[...]
    x = x_ref[...]
    d = x.shape[1]
    u = _rms(x, gpre[...]).astype(BF16)
    a_ret = _dot(oret[...], wret[...])
    if absorbed:
        a_mla = jnp.zeros(x.shape, F32)
        for h in range(heads):
            o_h = _dot(omla[:, h * kv_lora:(h + 1) * kv_lora].astype(BF16), wuv[h]).astype(BF16)
            a_mla = a_mla + _dot(o_h, wmla[h * d_v:(h + 1) * d_v, :])
    else:
        a_mla = _dot(omla[...], wmla[...])
    a_x = _dot(o_x.astype(BF16), wx[...])
    mixed = jnp.zeros(x.shape, F32)
    for i, a in enumerate((a_ret, a_mla, a_x)):
        mixed = mixed + _sigmoid(_dot(u, wgate[:, i * d:(i + 1) * d])) * a
    h_o[...] = x + _rms(_dot(mixed.astype(BF16), wout[...]), gpost[...])


def _merge(x, oret, omla, ox, w, dims, *, absorbed, tm, mem=None):
    n, d = x.shape
    row = lambda a: pl.BlockSpec((tm, a.shape[1]), lambda i: (i, 0))
    res = lambda a: _resident(a.shape)
    args = [x, w["g_pre"], w["w_gates"], oret, w["w_ret_o"], omla, w["w_mla_o"], w["w_uv_h"], ox, w["w_x_o"],
            w["w_out"], w["g_post"]]
    specs = [row(x), res(w["g_pre"]), res(w["w_gates"]), row(oret), res(w["w_ret_o"]), row(omla),
             res(w["w_mla_o"]), res(w["w_uv_h"]), row(ox), res(w["w_x_o"]), res(w["w_out"]), res(w["g_post"])]
    x_heads = 0
    if mem is not None:
        mk, mv, x_heads, seq_rows = mem
        tiles_per_seq = seq_rows // tm
        per_seq = pl.BlockSpec((1,) + mk.shape[1:], lambda i: (i // tiles_per_seq, 0, 0))
        args += [mk, mv]
        specs += [per_seq, per_seq]
    kern = functools.partial(_merge_kernel, heads=dims["heads"], kv_lora=dims["kv_lora"], d_v=dims["d_v"],
                             absorbed=absorbed, x_heads=x_heads)
    return pl.pallas_call(
        kern,
        grid=(n // tm,),
        in_specs=specs,
        out_specs=pl.BlockSpec((tm, d), lambda i: (i, 0)),
        out_shape=jax.ShapeDtypeStruct((n, d), F32),
        compiler_params=_params(1),
        name="merge_sample" if absorbed else "merge_prompt",
    )(*args)


def _ffn_kernel(h_ref, gpre, wg, wu, wd, gpost, y_o):
    h = h_ref[...]
    f = _rms(h, gpre[...]).astype(BF16)
    gate = _dot(f, wg[...])
    act = (gate * _sigmoid(gate) * _dot(f, wu[...])).astype(BF16)
    y_o[...] = h + _rms(_dot(act, wd[...]), gpost[...])


def _ffn(h, w, *, tm):
    n, d = h.shape
    res = lambda a: _resident(a.shape)
    return pl.pallas_call(
        _ffn_kernel,
        grid=(n // tm,),
        in_specs=[pl.BlockSpec((tm, d), lambda i: (i, 0)), res(w["g_ffn_pre"]), res(w["w_ffn_gate"]),
                  res(w["w_ffn_up"]), res(w["w_ffn_down"]), res(w["g_ffn_post"])],
        out_specs=pl.BlockSpec((tm, d), lambda i: (i, 0)),
        out_shape=jax.ShapeDtypeStruct((n, d), F32),
        compiler_params=_params(1),
        name="ffn",
    )(h, w["g_ffn_pre"], w["w_ffn_gate"], w["w_ffn_up"], w["w_ffn_down"], w["g_ffn_post"])


def _rope_tables(pos, ret_dk, d_rope):
    posf = pos.astype(F32)[:, None]

    def angles(half):
        inv = ROPE_BASE ** (-jnp.arange(half, dtype=F32) / half)
        ang = posf * inv[None, :]
        return jnp.cos(ang), jnp.sin(ang)

    cr, sr = angles(ret_dk // 2)
    cp, sp = angles(d_rope // 2)
    z = jnp.zeros_like(cp)
    pad = jnp.zeros((pos.shape[0], LANES - d_rope), F32)
    return {
        "cosr": jnp.concatenate([cr, cr], axis=1),
        "sinr": jnp.concatenate([-sr, sr], axis=1),
        "cosp": jnp.concatenate([cp, cp, pad], axis=1),
        "sinlo": jnp.concatenate([-sp, z, pad], axis=1),
        "sinhi": jnp.concatenate([z, sp, pad], axis=1),
    }


def _layer_weights(l, dims, sizes, norm_mix_pre, norm_mix_post, norm_ffn_pre, norm_ffn_post, norm_mem, norm_q_lat,
                   norm_kv_lat, w_in, w_uq, w_uk, w_uv, w_mem_k, w_mem_v, w_ret_o, w_mla_o, w_x_o, w_out,
                   w_ffn_gate, w_ffn_up, w_ffn_down):
    heads, d_nope, d_rope, kv_lora, d_v = (dims[k] for k in ("heads", "d_nope", "d_rope", "kv_lora", "d_v"))
    bf = lambda a: a.astype(BF16)
    gain = lambda a: a[l].astype(F32)[None, :]
    offs = np.concatenate([[0], np.cumsum(sizes)])
    seg = [w_in[l][:, offs[i]:offs[i + 1]] for i in range(len(sizes))]
    q_lora = w_uq.shape[1]
    uq = w_uq[l].reshape(q_lora, heads, d_nope + d_rope)
    uq_p = jnp.pad(uq[:, :, d_nope:], ((0, 0), (0, 0), (0, LANES - d_rope)))
    return {
        "g_pre": gain(norm_mix_pre), "g_post": gain(norm_mix_post), "g_ffn_pre": gain(norm_ffn_pre),
        "g_ffn_post": gain(norm_ffn_post), "g_mem": gain(norm_mem), "g_q": gain(norm_q_lat),
        "g_kv": gain(norm_kv_lat),
        "w_rq": bf(seg[0]), "w_rk": bf(seg[1]), "w_rv": bf(seg[2]), "w_rg": bf(seg[3]), "w_cq": bf(seg[4]),
        "w_ckv": bf(seg[5]), "w_kpe": bf(jnp.pad(seg[6], ((0, 0), (0, LANES - d_rope)))), "w_xq": bf(seg[7]),
        "w_gates": bf(seg[8]),
        "w_uq_n": bf(uq[:, :, :d_nope].reshape(q_lora, heads * d_nope)),
        "w_uq_p": bf(uq_p.reshape(q_lora, heads * LANES)),
        "w_uk_t": bf(jnp.swapaxes(w_uk[l], 1, 2)),
        "w_kn": bf(jnp.swapaxes(w_uk[l], 0, 1).reshape(kv_lora, heads * d_nope)),
        "w_vn": bf(jnp.swapaxes(w_uv[l], 0, 1).reshape(kv_lora, heads * d_v)),
        "w_uv_h": bf(w_uv[l]),
        "w_mem_k": bf(w_mem_k[l]), "w_mem_v": bf(w_mem_v[l]),
        "w_ret_o": bf(w_ret_o[l]), "w_mla_o": bf(w_mla_o[l]), "w_x_o": bf(w_x_o[l]), "w_out": bf(w_out[l]),
        "w_ffn_gate": bf(w_ffn_gate[l]), "w_ffn_up": bf(w_ffn_up[l]), "w_ffn_down": bf(w_ffn_down[l]),
    }


def _pad_tokens(a, b, tokens, rows=SAMPLE_TOK_PAD):
    a = a.reshape(b, tokens, a.shape[-1])
    return jnp.pad(a, ((0, 0), (0, rows - tokens), (0, 0)))


def kernel(x_prompt, x_sample, mem_prompt, cache_ckv, cache_kpe, page_table, state_ret, cache_mem_k, cache_mem_v,
           norm_mix_pre, norm_mix_post, norm_ffn_pre, norm_ffn_post, norm_mem, norm_q_lat, norm_kv_lat, w_in,
           w_uq, w_uk, w_uv, w_mem_k, w_mem_v, w_ret_o, w_mla_o, w_x_o, w_out, w_ffn_gate, w_ffn_up, w_ffn_down):
    depth = w_in.shape[0]
    batch, seq, d_model = x_prompt.shape
    db, tokens, _ = x_sample.shape
    ret_heads, ret_dk, ret_dv = state_ret.shape[2:]
    heads, kv_lora, d_nope = w_uk.shape[1:]
    d_rope = cache_kpe.shape[-1]
    d_v = w_uv.shape[-1]
    n_mem, x_heads, x_hd = cache_mem_k.shape[2:]
    q_lora = w_uq.shape[1]
    past_len = page_table.shape[1] * cache_ckv.shape[2]
    dims = dict(heads=heads, d_nope=d_nope, d_rope=d_rope, kv_lora=kv_lora, d_v=d_v,
                ret_heads=ret_heads, ret_dk=ret_dk, ret_dv=ret_dv)
    assert ret_dk == LANES and d_nope == LANES and d_rope <= LANES // 2 and tokens <= SAMPLE_TOK_PAD
    sizes = (ret_heads * ret_dk, ret_heads * ret_dk, ret_heads * ret_dv, ret_heads * ret_dv, q_lora, kv_lora,
             d_rope, x_heads * x_hd, w_in.shape[2] - (2 * ret_heads * ret_dk + 2 * ret_heads * ret_dv + q_lora
                                                       + kv_lora + d_rope + x_heads * x_hd))

    cache_kpe_t = jnp.swapaxes(cache_kpe, 2, 3)

    tabs_p = _rope_tables(jnp.arange(seq), ret_dk, d_rope)
    n_s = db * tokens
    tabs_s = _rope_tables(past_len + (jnp.arange(n_s) % tokens), ret_dk, d_rope)

    y_p = x_prompt.reshape(batch * seq, d_model)
    y_s = x_sample.reshape(n_s, d_model)
    outs = [[] for _ in range(8)]
    for l in range(depth):
        w = _layer_weights(l, dims, sizes, norm_mix_pre, norm_mix_post, norm_ffn_pre, norm_ffn_post, norm_mem,
                           norm_q_lat, norm_kv_lat, w_in, w_uq, w_uk, w_uv, w_mem_k, w_mem_v, w_ret_o, w_mla_o,
                           w_x_o, w_out, w_ffn_gate, w_ffn_up, w_ffn_down)

        mk_p, mv_p = _mem_kv(mem_prompt, w["g_mem"], w["w_mem_k"], w["w_mem_v"])
        rq, rk, rv, rg, qcat, xq, ckv_p, kpe_p, kcat, vn = _in_proj(y_p, w, tabs_p, dims, absorbed=False,
                                                                    tm=ROW_TILE)
        o_ret, ret_p = _ret_prompt(rq, rk, rv, rg, batch, seq, dims)
        o_mla = _mla_prompt(qcat, kcat, vn, batch, seq, dims)
        h_p = _merge(y_p, o_ret, o_mla, xq, w, dims, absorbed=False, tm=ROW_TILE, mem=(mk_p, mv_p, x_heads, seq))
        y_p = _ffn(h_p, w, tm=ROW_TILE)

        rq, rk, rv, rg, qcat, xq, ckv_s, kpe_s, qlat = _in_proj(y_s, w, tabs_s, dims, absorbed=True, tm=ROW_TILE)
        pad = lambda a: _pad_tokens(a, db, tokens)
        o_ret, ret_s = _ret_sample(pad(rq), pad(rk), pad(rv), pad(rg), state_ret[l].astype(F32), tokens, dims)
        o_ret = o_ret[:, :tokens].reshape(n_s, ret_heads * ret_dv)
        group = d_nope + LANES
        q_pe = qcat.reshape(db, tokens * heads, group)[:, :, d_nope:d_nope + d_rope]
        o_lat = _mla_sample(qlat.reshape(db, tokens * heads, kv_lora), q_pe, pad(ckv_s), pad(kpe_s),
                            cache_ckv, cache_kpe_t, l, page_table, tokens, dims)
        o_lat = o_lat.reshape(n_s, heads * kv_lora)
        mem_t = lambda c: jnp.transpose(c[l], (0, 2, 3, 1)).reshape(db, x_heads * x_hd, n_mem)
        o_x = _x_sample(_pad_tokens(xq.astype(F32), db, tokens, rows=X_SAMPLE_TOK_PAD), mem_t(cache_mem_k),
                        mem_t(cache_mem_v), x_heads)
        o_x = o_x[:, :tokens].reshape(n_s, x_heads * x_hd)
        h_s = _merge(y_s, o_ret, o_lat, o_x, w, dims, absorbed=True, tm=ROW_TILE)
        y_s = _ffn(h_s, w, tm=ROW_TILE)

        for lst, val in zip(outs, (ckv_p.reshape(batch, seq, kv_lora), kpe_p.reshape(batch, seq, d_rope),
                                   ckv_s.reshape(db, tokens, kv_lora), kpe_s.reshape(db, tokens, d_rope),
                                   ret_p.astype(x_prompt.dtype), ret_s.astype(state_ret.dtype),
                                   mk_p.reshape(batch, n_mem, x_heads, x_hd),
                                   mv_p.reshape(batch, n_mem, x_heads, x_hd))):
            lst.append(val)

    return (y_p.reshape(batch, seq, d_model), y_s.reshape(db, tokens, d_model)) + tuple(jnp.stack(o) for o in outs)
```

```python
import functools
import math

import numpy as np
import jax
import jax.numpy as jnp
from jax import lax
from jax.experimental import pallas as pl
from jax.experimental.pallas import tpu as pltpu

F32 = jnp.float32
BF16 = jnp.bfloat16

ROPE_BASE = 10000.0
RMS_EPS = 1e-6
LANES = 128
VMEM_LIMIT = 56 * 1024 * 1024
NEG_BIG = -1e30

ROW_TILE = 512
RET_CHUNK = 256
ATTN_Q_TILE = 512
ATTN_KV_TILE = 512
ATTN_HEADS_PER_STEP = 8
SAMPLE_TOK_PAD = 16
X_SAMPLE_TOK_PAD = 8
RET_SAMPLE_GROUP = 8
X_SAMPLE_GROUP = 8
DECODE_CHUNK = 4096
DMA_LOOP_UNROLL = 4


def _resident(shape):
    nd = len(shape)
    return pl.BlockSpec(shape, lambda *_: (0,) * nd, pipeline_mode=pl.Buffered(1))


def _params(n_axes, flags=None):
    return pltpu.CompilerParams(dimension_semantics=("arbitrary",) * n_axes, vmem_limit_bytes=VMEM_LIMIT,
                                flags=flags)


def _rms(x, g=None):
    y = x * lax.rsqrt(jnp.mean(x * x, axis=-1, keepdims=True) + RMS_EPS)
    return y if g is None else y * g


def _sigmoid(x):
    return 1.0 / (1.0 + jnp.exp(-x))


def _dot(a, b):
    return jnp.dot(a, b, preferred_element_type=F32)


def _dot_nt(a, b):
    return lax.dot_general(a, b, (((1,), (1,)), ((), ())), preferred_element_type=F32)


def _dot_tn(a, b):
    return lax.dot_general(a, b, (((0,), (0,)), ((), ())), preferred_element_type=F32)


def _rope_half_vreg(z, cos_t, sin_lo, sin_hi, quarter):
    return (z * cos_t + pltpu.roll(z, LANES - quarter, 1) * sin_lo + pltpu.roll(z, quarter, 1) * sin_hi)


def _in_proj_kernel(x_ref, g_ref, wrq, wrk, wrv, wrg, wcq, wckv, wkpe, wxq, gq_ref, gkv_ref, wuqn, wuqp,
                    cosr, sinr, cosp, sinlo, sinhi, *rest,
                    ret_heads, ret_dk, heads, d_nope, d_rope, kv_lora, absorbed, q_scale):
    if absorbed:
        wa, (rq_o, rk_o, rv_o, rg_o, qcat_o, xq_o, ckv_o, kpe_o, a_o) = rest[0], rest[1:]
    else:
        wa, wb, (rq_o, rk_o, rv_o, rg_o, qcat_o, xq_o, ckv_o, kpe_o, a_o, b_o) = rest[0], rest[1], rest[2:]
    u = _rms(x_ref[...], g_ref[...]).astype(BF16)
    cr, sr = cosr[...], sinr[...]
    cp, slo, shi = cosp[...], sinlo[...], sinhi[...]
    group = d_nope + LANES
    k_scale = ret_dk ** -0.5

    cq = _dot(u, wcq[...])
    ckv = _dot(u, wckv[...])
    kpe_raw = _dot(u, wkpe[...])
    zq = _dot(u, wrq[...])
    zk = _dot(u, wrk[...])

    cqn = _rms(cq, gq_ref[...]).astype(BF16)
    ckvn = _rms(ckv, gkv_ref[...])
    ckv_o[...] = ckvn
    kpe = _rope_half_vreg(kpe_raw, cp, slo, shi, d_rope // 2)
    kpe_o[...] = kpe[:, :d_rope]

    qn = _dot(cqn, wuqn[...])
    qp = _dot(cqn, wuqp[...])
    if not absorbed:
        ckvb = ckvn.astype(BF16)
        kn = _dot(ckvb, wa[...])
        vn = _dot(ckvb, wb[...])

    for h in range(ret_heads):
        sl = slice(h * ret_dk, (h + 1) * ret_dk)
        q_h = zq[:, sl]
        k_h = zk[:, sl]
        rq_o[:, sl] = (q_h * cr + pltpu.roll(q_h, ret_dk // 2, 1) * sr).astype(BF16)
        rk_o[:, sl] = ((k_h * cr + pltpu.roll(k_h, ret_dk // 2, 1) * sr) * k_scale).astype(BF16)

    rv = _dot(u, wrv[...])
    rg = _dot(u, wrg[...])
    xq = _dot(u, wxq[...])

    for h in range(heads):
        qn_h = qn[:, h * d_nope:(h + 1) * d_nope]
        qp_h = _rope_half_vreg(qp[:, h * LANES:(h + 1) * LANES], cp, slo, shi, d_rope // 2)
        if absorbed:
            a_o[:, h * kv_lora:(h + 1) * kv_lora] = _dot(qn_h.astype(BF16), wa[h]).astype(BF16)
        else:
            qn_h = qn_h * q_scale
            qp_h = qp_h * q_scale
        qcat_o[:, h * group:h * group + d_nope] = qn_h.astype(BF16)
        qcat_o[:, h * group + d_nope:(h + 1) * group] = qp_h.astype(BF16)
    if not absorbed:
        kpeb = kpe.astype(BF16)
        for h in range(heads):
            a_o[:, h * group:h * group + d_nope] = kn[:, h * d_nope:(h + 1) * d_nope].astype(BF16)
            a_o[:, h * group + d_nope:(h + 1) * group] = kpeb
        b_o[...] = vn.astype(BF16)

    rv_o[...] = rv.astype(BF16)
    rg_o[...] = rg.astype(BF16)
    xq_o[...] = xq.astype(BF16)


def _in_proj(x, w, tabs, dims, *, absorbed, tm):
    n, d_model = x.shape
    heads, d_nope, d_rope, kv_lora = dims["heads"], dims["d_nope"], dims["d_rope"], dims["kv_lora"]
    ret_heads, ret_dk, ret_dv = dims["ret_heads"], dims["ret_dk"], dims["ret_dv"]
    group = d_nope + LANES
    tab_rows = tabs["cosr"].shape[0]
    tab_tiles = tab_rows // tm
    row = lambda c: pl.BlockSpec((tm, c), lambda i: (i, 0))
    tab = lambda: pl.BlockSpec((tm, LANES), lambda i: (i % tab_tiles, 0))
    weights = [w["g_pre"], w["w_rq"], w["w_rk"], w["w_rv"], w["w_rg"], w["w_cq"], w["w_ckv"], w["w_kpe"],
               w["w_xq"], w["g_q"], w["g_kv"], w["w_uq_n"], w["w_uq_p"]]
    out_cols = [(ret_heads * ret_dk, BF16), (ret_heads * ret_dk, BF16), (ret_heads * ret_dv, BF16),
                (ret_heads * ret_dv, BF16), (heads * group, BF16), (w["w_xq"].shape[1], BF16), (kv_lora, F32),
                (d_rope, F32)]
    if absorbed:
        mode_weights = [w["w_uk_t"]]
        out_cols += [(heads * kv_lora, BF16)]
    else:
        mode_weights = [w["w_kn"], w["w_vn"]]
        out_cols += [(heads * group, BF16), (w["w_vn"].shape[1], BF16)]
    kern = functools.partial(_in_proj_kernel, ret_heads=ret_heads, ret_dk=ret_dk, heads=heads, d_nope=d_nope,
                             d_rope=d_rope, kv_lora=kv_lora, absorbed=absorbed,
                             q_scale=(d_nope + d_rope) ** -0.5 * math.log2(math.e))
    return pl.pallas_call(
        kern,
        grid=(n // tm,),
        in_specs=([row(d_model)] + [_resident(a.shape) for a in weights] + [tab() for _ in range(5)]
                  + [_resident(a.shape) for a in mode_weights]),
        out_specs=[row(c) for c, _ in out_cols],
        out_shape=[jax.ShapeDtypeStruct((n, c), dt) for c, dt in out_cols],
        compiler_params=_params(1),
        name="in_proj_sample" if absorbed else "in_proj_prompt",
    )(x, *weights, tabs["cosr"], tabs["sinr"], tabs["cosp"], tabs["sinlo"], tabs["sinhi"], *mode_weights)


def _ret_head(q, k, v, g, s, dec, qd, kd, g_l):
    inner = _dot_nt(q, k) * dec
    o = _dot(inner.astype(BF16), v) + _dot((q.astype(F32) * qd).astype(BF16), s.astype(BF16))
    s_new = s * g_l + _dot_tn((k.astype(F32) * kd).astype(BF16), v)
    gf = g.astype(F32)
    o = (gf * _sigmoid(gf)) * _rms(o)
    return o.astype(BF16), s_new


def _ret_prompt_kernel(rq, rk, rv, rg, dec, qd, kd, o_ref, s_out, s_scr, *, heads, dk, dv, g_l):
    c = pl.program_id(1)

    @pl.when(c == 0)
    def _():
        s_scr[...] = jnp.zeros(s_scr.shape, F32)

    for h in range(heads):
        o, s_new = _ret_head(rq[:, h * dk:(h + 1) * dk], rk[:, h * dk:(h + 1) * dk],
                             rv[:, h * dv:(h + 1) * dv], rg[:, h * dv:(h + 1) * dv],
                             s_scr[h], dec[h], qd[h], kd[h], g_l[h])
        o_ref[:, h * dv:(h + 1) * dv] = o
        s_scr[h] = s_new

    @pl.when(c == pl.num_programs(1) - 1)
    def _():
        s_out[0] = s_scr[...]


def _ret_consts(heads, length, dk, chunk_rows=None):
    rows = length if chunk_rows is None else chunk_rows
    lg = np.log1p(-np.exp2(-5.0 - np.arange(heads, dtype=np.float64)))
    i = np.arange(rows, dtype=np.float64)
    diff = i[:, None] - i[None, :]
    valid = (diff >= 0) & (i[:, None] < length) & (i[None, :] < length)
    dec = np.where(valid[None], np.exp(np.maximum(diff, 0.0)[None] * lg[:, None, None]), 0.0)
    qd = np.exp((i[None, :] + 1.0) * lg[:, None])
    kd = np.where(i[None, :] < length, np.exp((length - 1.0 - i[None, :]) * lg[:, None]), 0.0)
    qd = np.broadcast_to(qd[:, :, None], (heads, rows, dk))
    kd = np.broadcast_to(kd[:, :, None], (heads, rows, dk))
    g_l = tuple(float(v) for v in np.exp(length * lg))
    return (jnp.asarray(dec, F32), jnp.asarray(qd, F32), jnp.asarray(kd, F32), g_l)


def _ret_prompt(rq, rk, rv, rg, batch, seq, dims):
    heads, dk, dv = dims["ret_heads"], dims["ret_dk"], dims["ret_dv"]
    L = RET_CHUNK
    nc = seq // L
    dec, qd, kd, g_l = _ret_consts(heads, L, dk)
    row = lambda c: pl.BlockSpec((L, c), lambda b, i: (b * nc + i, 0))
    kern = functools.partial(_ret_prompt_kernel, heads=heads, dk=dk, dv=dv, g_l=g_l)
    return pl.pallas_call(
        kern,
        grid=(batch, nc),
        in_specs=[row(heads * dk), row(heads * dk), row(heads * dv), row(heads * dv),
                  _resident(dec.shape), _resident(qd.shape), _resident(kd.shape)],
        out_specs=[row(heads * dv), pl.BlockSpec((1, heads, dk, dv), lambda b, i: (b, 0, 0, 0))],
        out_shape=[jax.ShapeDtypeStruct((batch * seq, heads * dv), BF16),
                   jax.ShapeDtypeStruct((batch, heads, dk, dv), F32)],
        scratch_shapes=[pltpu.VMEM((heads, dk, dv), F32)],
        compiler_params=_params(2),
        name="retention_prompt",
    )(rq, rk, rv, rg, dec, qd, kd)


def _ret_sample_kernel(rq, rk, rv, rg, s0, dec, qd, kd, o_ref, s_out, *, group, heads, dk, dv, g_l):
    pairs = [(i, h) for i in range(group) for h in range(heads)]

    def state_side(i, h):
        q = rq[i, :, h * dk:(h + 1) * dk]
        k = rk[i, :, h * dk:(h + 1) * dk]
        v = rv[i, :, h * dv:(h + 1) * dv]
        s = s0[i, h]
        read = _dot((q.astype(F32) * qd[h]).astype(BF16), s.astype(BF16))
        s_out[i, h] = s * g_l[h] + _dot_tn((k.astype(F32) * kd[h]).astype(BF16), v)
        return read

    def output_side(i, h, read):
        q = rq[i, :, h * dk:(h + 1) * dk]
        k = rk[i, :, h * dk:(h + 1) * dk]
        v = rv[i, :, h * dv:(h + 1) * dv]
        o = _dot((_dot_nt(q, k) * dec[h]).astype(BF16), v) + read
        gf = rg[i, :, h * dv:(h + 1) * dv].astype(F32)
        o_ref[i, :, h * dv:(h + 1) * dv] = ((gf * _sigmoid(gf)) * _rms(o)).astype(BF16)

    read = state_side(*pairs[0])
    for n, (i, h) in enumerate(pairs):
        nxt = state_side(*pairs[n + 1]) if n + 1 < len(pairs) else None
        output_side(i, h, read)
        read = nxt


def _ret_sample(rq, rk, rv, rg, state, tokens, dims):
    heads, dk, dv = dims["ret_heads"], dims["ret_dk"], dims["ret_dv"]
    b, tp, _ = rq.shape
    G = RET_SAMPLE_GROUP
    dec, qd, kd, g_l = _ret_consts(heads, tokens, dk, chunk_rows=tp)
    blk = lambda c: pl.BlockSpec((G, tp, c), lambda i: (i, 0, 0))
    st = pl.BlockSpec((G, heads, dk, dv), lambda i: (i, 0, 0, 0))
    kern = functools.partial(_ret_sample_kernel, group=G, heads=heads, dk=dk, dv=dv, g_l=g_l)
    return pl.pallas_call(
        kern,
        grid=(b // G,),
        in_specs=[blk(heads * dk), blk(heads * dk), blk(heads * dv), blk(heads * dv), st,
                  _resident(dec.shape), _resident(qd.shape), _resident(kd.shape)],
        out_specs=[blk(heads * dv), st],
        out_shape=[jax.ShapeDtypeStruct((b, tp, heads * dv), BF16),
                   jax.ShapeDtypeStruct((b, heads, dk, dv), F32)],
        compiler_params=_params(1),
        name="retention_sample",
    )(rq, rk, rv, rg, state, dec, qd, kd)


def _mla_prompt_kernel(q_ref, k_ref, v_ref, o_ref, m_scr, l_scr, acc_scr, *, tq, tk, hp, group, d_v):
    qi = pl.program_id(2)
    ri = lax.broadcasted_iota(jnp.int32, (tq, tk), 0)
    ci = lax.broadcasted_iota(jnp.int32, (tq, tk), 1)
    causal_bias = jnp.where(ci <= ri, 0.0, NEG_BIG)

    def block(j, carry, first):
        start = pl.multiple_of(j * tk, tk)

        def scores(h):
            return _dot_nt(q_ref[:, h * group:(h + 1) * group], k_ref[pl.ds(start, tk), h * group:(h + 1) * group])

        s_next = scores(0)
        for h in range(hp):
            s = s_next
            if h + 1 < hp:
                s_next = scores(h + 1)
            vals = v_ref[pl.ds(start, tk), h * d_v:(h + 1) * d_v]
            if first:
                s = s + causal_bias
                m_new = jnp.broadcast_to(jnp.max(s, axis=-1, keepdims=True), (tq, LANES))
                p = jnp.exp2(s - jnp.tile(m_new, (1, tk // LANES)))
                l_scr[h] = jnp.broadcast_to(jnp.sum(p, axis=-1, keepdims=True), (tq, LANES))
                acc_scr[h] = _dot(p.astype(BF16), vals)
            else:
                m_prev = m_scr[h]
                m_new = jnp.maximum(m_prev, jnp.max(s, axis=-1, keepdims=True))
                a = jnp.exp2(m_prev - m_new)
                p = jnp.exp2(s - jnp.tile(m_new, (1, tk // LANES)))
                l_scr[h] = a * l_scr[h] + jnp.sum(p, axis=-1, keepdims=True)
                acc_scr[h] = a * acc_scr[h] + _dot(p.astype(BF16), vals)
            m_scr[h] = m_new
        return carry

    block(qi, 0, True)
    lax.fori_loop(0, qi, functools.partial(block, first=False), 0)
    for h in range(hp):
        o_ref[:, h * d_v:(h + 1) * d_v] = (acc_scr[h] / l_scr[h]).astype(BF16)


def _mla_prompt(qcat, kcat, v, batch, seq, dims):
    heads, d_nope, d_v = dims["heads"], dims["d_nope"], dims["d_v"]
    group = d_nope + LANES
    t = ATTN_Q_TILE
    hp = ATTN_HEADS_PER_STEP
    nq = seq // t
    assert ATTN_KV_TILE == t and d_v == LANES
    kern = functools.partial(_mla_prompt_kernel, tq=t, tk=ATTN_KV_TILE, hp=hp, group=group, d_v=d_v)
    return pl.pallas_call(
        kern,
        grid=(batch, heads // hp, nq),
        in_specs=[pl.BlockSpec((t, hp * group), lambda b, h, i: (b * nq + i, h)),
                  pl.BlockSpec((seq, hp * group), lambda b, h, i: (b, h)),
                  pl.BlockSpec((seq, hp * d_v), lambda b, h, i: (b, h))],
        out_specs=pl.BlockSpec((t, hp * d_v), lambda b, h, i: (b * nq + i, h)),
        out_shape=jax.ShapeDtypeStruct((batch * seq, heads * d_v), BF16),
        scratch_shapes=[pltpu.VMEM((hp, t, LANES), F32), pltpu.VMEM((hp, t, LANES), F32),
                        pltpu.VMEM((hp, t, d_v), F32)],
        compiler_params=_params(3),
        name="mla_prompt",
    )(qcat, kcat, v)


def _mla_sample_kernel(pt_ref, ql_ref, qp_ref, cn_ref, kn_ref, ckv_hbm, kpe_hbm, o_ref,
                       ckv_buf, kpe_buf, kbf, s_scr, sem,
                       *, layer, n_pages, page, chunk, tokens, heads, scale):
    b = pl.program_id(0)
    past = n_pages * page

    def page_copy(seq, slot, p, array):
        pg = pt_ref[seq, p]
        off = pl.multiple_of(p * page, page)
        if array == 0:
            return pltpu.make_async_copy(ckv_hbm.at[layer, pg], ckv_buf.at[slot, pl.ds(off, page), :],
                                         sem.at[0, slot])
        return pltpu.make_async_copy(kpe_hbm.at[layer, pg], kpe_buf.at[slot, :, pl.ds(off, page)], sem.at[1, slot])

    def for_each_page(seq, slot, fn, arrays=(0, 1), unroll=DMA_LOOP_UNROLL):
        def body(t, carry):
            for parity in range(2):
                for a in arrays:
                    fn(page_copy(seq, slot, 2 * t + parity, a), parity)
            return carry
        lax.fori_loop(0, n_pages // 2, body, 0, unroll=unroll)

    start = lambda cp, parity: cp.start(priority=parity)
    wait = lambda cp, parity: cp.wait()

    @pl.when(b == 0)
    def _():
        for_each_page(0, 0, start)

    @pl.when(b + 1 < pl.num_programs(0))
    def _():
        for_each_page(b + 1, (b + 1) % 2, start)

    slot = b % 2
    for_each_page(b, slot, wait, arrays=(0,), unroll=True)
    for_each_page(b, slot, wait, arrays=(1,), unroll=True)

    ql = ql_ref[0]
    qp = qp_ref[0]
    ckv_s = ckv_buf.at[slot]
    kpe_s = kpe_buf.at[slot]
    n_chunks = past // chunk

    cn = cn_ref[0].astype(BF16)
    kn = kn_ref[0].astype(BF16)
    s_new = (_dot_nt(ql, cn) + _dot_nt(qp, kn)) * scale
    t = lax.broadcasted_iota(jnp.int32, s_new.shape, 0) // heads
    col = lax.broadcasted_iota(jnp.int32, s_new.shape, 1)
    s_new = jnp.where((col <= t) & (col < tokens), s_new, NEG_BIG)
    m = jnp.max(s_new, axis=-1, keepdims=True)
    p_new = jnp.exp(s_new - m)
    l = jnp.sum(p_new, axis=-1, keepdims=True)
    acc = _dot(p_new.astype(BF16), cn)

    def scores(c):
        sl = slice(c * chunk, (c + 1) * chunk)
        kc = ckv_s[sl, :].astype(BF16)
        kbf[sl, :] = kc
        s_scr[:, sl] = (_dot_nt(ql, kc) + _dot(qp, kpe_s[:, sl].astype(BF16))) * scale

    def values(c, m, l, acc):
        sl = slice(c * chunk, (c + 1) * chunk)
        s = s_scr[:, sl]
        m_new = jnp.maximum(m, jnp.max(s, axis=-1, keepdims=True))
        a = jnp.exp(m - m_new)
        p = jnp.exp(s - m_new)
        l = a * l + jnp.sum(p, axis=-1, keepdims=True)
        return m_new, l, a * acc + _dot(p.astype(BF16), kbf[sl, :])

    scores(0)
    for c in range(n_chunks):
        if c + 1 < n_chunks:
            scores(c + 1)
        m, l, acc = values(c, m, l, acc)
    o_ref[0] = acc / l


def _mla_sample(ql, qp, ckv_new, kpe_new, cache_ckv, cache_kpe_t, layer, page_table, tokens, dims):
    heads, d_nope, d_rope, kv_lora = dims["heads"], dims["d_nope"], dims["d_rope"], dims["kv_lora"]
    b, rows, _ = ql.shape
    n_pages = page_table.shape[1]
    page = cache_ckv.shape[2]
    past = n_pages * page
    tp = ckv_new.shape[1]
    per_b = lambda r, c: pl.BlockSpec((1, r, c), lambda i, pt: (i, 0, 0))
    hbm = pl.BlockSpec(memory_space=pl.ANY)
    kern = functools.partial(_mla_sample_kernel, layer=layer, n_pages=n_pages, page=page, chunk=DECODE_CHUNK,
                             tokens=tokens, heads=heads, scale=(d_nope + d_rope) ** -0.5)
    grid_spec = pltpu.PrefetchScalarGridSpec(
        num_scalar_prefetch=1,
        grid=(b,),
        in_specs=[per_b(rows, kv_lora), per_b(rows, d_rope), per_b(tp, kv_lora), per_b(tp, d_rope), hbm, hbm],
        out_specs=per_b(rows, kv_lora),
        scratch_shapes=[pltpu.VMEM((2, past, kv_lora), F32), pltpu.VMEM((2, d_rope, past), F32),
                        pltpu.VMEM((past, kv_lora), BF16), pltpu.VMEM((rows, past), F32),
                        pltpu.SemaphoreType.DMA((2, 2))],
    )
    return pl.pallas_call(
        kern,
        grid_spec=grid_spec,
        out_shape=jax.ShapeDtypeStruct((b, rows, kv_lora), F32),
        compiler_params=_params(1),
        name="mla_sample",
    )(page_table, ql, qp, ckv_new, kpe_new, cache_ckv, cache_kpe_t)


def _mem_kv_kernel(m_ref, g_ref, wk, wv, k_o, v_o):
    mn = _rms(m_ref[0], g_ref[...]).astype(BF16)
    k_o[0] = _dot(mn, wk[...])
    v_o[0] = _dot(mn, wv[...])


def _mem_kv(mem, g, wk, wv):
    b, m, d = mem.shape
    c = wk.shape[1]
    return pl.pallas_call(
        _mem_kv_kernel,
        grid=(b,),
        in_specs=[pl.BlockSpec((1, m, d), lambda i: (i, 0, 0)), _resident(g.shape), _resident(wk.shape),
                  _resident(wv.shape)],
        out_specs=[pl.BlockSpec((1, m, c), lambda i: (i, 0, 0))] * 2,
        out_shape=[jax.ShapeDtypeStruct((b, m, c), F32)] * 2,
        compiler_params=_params(1),
        name="mem_kv",
    )(mem, g, wk, wv)


def _x_attend_heads(q, mk, mv, heads, hd):
    r = q.shape[0]
    lane = lax.broadcasted_iota(jnp.int32, q.shape, 1)
    sels = [(lane >= h * hd) & (lane < (h + 1) * hd) for h in range(heads)]
    q_heads = jnp.concatenate([jnp.where(sel, q, jnp.zeros_like(q)) for sel in sels], axis=0)
    s = _dot_nt(q_heads, mk) * (hd ** -0.5)
    p = jnp.exp(s - jnp.max(s, axis=-1, keepdims=True))
    p = p / jnp.sum(p, axis=-1, keepdims=True)
    pv = _dot(p.astype(BF16), mv)
    out = jnp.zeros(q.shape, F32)
    for h, sel in enumerate(sels):
        out = out + jnp.where(sel, pv[h * r:(h + 1) * r], 0.0)
    return out


def _x_sample_kernel(q_ref, kt_ref, vt_ref, o_ref, *, group, heads, hd):
    tp = q_ref.shape[1]
    lane = lax.broadcasted_iota(jnp.int32, q_ref.shape[1:], 1)
    sels = [(lane >= h * hd) & (lane < (h + 1) * hd) for h in range(heads)]
    def scores(g):
        q_heads = jnp.concatenate([jnp.where(sel, q_ref[g], 0.0) for sel in sels], axis=0).astype(BF16)
        return _dot(q_heads, kt_ref[g].astype(BF16)) * (hd ** -0.5)

    s_next = scores(0)
    for g in range(group):
        q = q_ref[g]
        s = s_next
        if g + 1 < group:
            s_next = scores(g + 1)
        p = jnp.exp(s - jnp.max(s, axis=-1, keepdims=True))
        p = p / jnp.sum(p, axis=-1, keepdims=True)
        r = _dot_nt(p.astype(BF16), vt_ref[g].astype(BF16))
        out = jnp.zeros(q.shape, F32)
        for h, sel in enumerate(sels):
            out = out + jnp.where(sel, r[h * tp:(h + 1) * tp], 0.0)
        o_ref[g] = out


def _x_sample(xq, mkt, mvt, heads):
    b, tp, c = xq.shape
    m = mkt.shape[2]
    G = X_SAMPLE_GROUP
    kern = functools.partial(_x_sample_kernel, group=G, heads=heads, hd=c // heads)
    return pl.pallas_call(
        kern,
        grid=(b // G,),
        in_specs=[pl.BlockSpec((G, tp, c), lambda i: (i, 0, 0)),
                  pl.BlockSpec((G, c, m), lambda i: (i, 0, 0)),
                  pl.BlockSpec((G, c, m), lambda i: (i, 0, 0))],
        out_specs=pl.BlockSpec((G, tp, c), lambda i: (i, 0, 0)),
        out_shape=jax.ShapeDtypeStruct((b, tp, c), F32),
        compiler_params=_params(1),
        name="x_attend_sample",
    )(xq, mkt, mvt)


def _merge_kernel(x_ref, gpre, wgate, oret, wret, omla, wmla, wuv, ox, wx, wout, gpost, *rest,
                  heads, kv_lora, d_v, absorbed, x_heads):
    h_o = rest[-1]
    if x_heads:
        mk, mv = rest[:2]
        o_x = _x_attend_heads(ox[...], mk[0].astype(BF16), mv[0].astype(BF16), x_heads, ox.shape[1] // x_heads)
    else:
        o_x = ox[...]
    x = x_ref[...]
    d = x.shape[1]
    u = _rms(x, gpre[...]).astype(BF16)
    a_ret = _dot(oret[...], wret[...])
    if absorbed:
        a_mla = jnp.zeros(x.shape, F32)
        for h in range(heads):
            o_h = _dot(omla[:, h * kv_lora:(h + 1) * kv_lora].astype(BF16), wuv[h]).astype(BF16)
            a_mla = a_mla + _dot(o_h, wmla[h * d_v:(h + 1) * d_v, :])
    else:
        a_mla = _dot(omla[...], wmla[...])
    a_x = _dot(o_x.astype(BF16), wx[...])
    mixed = jnp.zeros(x.shape, F32)
    for i, a in enumerate((a_ret, a_mla, a_x)):
        mixed = mixed + _sigmoid(_dot(u, wgate[:, i * d:(i + 1) * d])) * a
    h_o[...] = x + _rms(_dot(mixed.astype(BF16), wout[...]), gpost[...])


def _merge(x, oret, omla, ox, w, dims, *, absorbed, tm, mem=None):
    n, d = x.shape
    row = lambda a: pl.BlockSpec((tm, a.shape[1]), lambda i: (i, 0))
    res = lambda a: _resident(a.shape)
    args = [x, w["g_pre"], w["w_gates"], oret, w["w_ret_o"], omla, w["w_mla_o"], w["w_uv_h"], ox, w["w_x_o"],
            w["w_out"], w["g_post"]]
    specs = [row(x), res(w["g_pre"]), res(w["w_gates"]), row(oret), res(w["w_ret_o"]), row(omla),
             res(w["w_mla_o"]), res(w["w_uv_h"]), row(ox), res(w["w_x_o"]), res(w["w_out"]), res(w["g_post"])]
    x_heads = 0
    if mem is not None:
        mk, mv, x_heads, seq_rows = mem
        tiles_per_seq = seq_rows // tm
        per_seq = pl.BlockSpec((1,) + mk.shape[1:], lambda i: (i // tiles_per_seq, 0, 0))
        args += [mk, mv]
        specs += [per_seq, per_seq]
    kern = functools.partial(_merge_kernel, heads=dims["heads"], kv_lora=dims["kv_lora"], d_v=dims["d_v"],
                             absorbed=absorbed, x_heads=x_heads)
    return pl.pallas_call(
        kern,
        grid=(n // tm,),
        in_specs=specs,
        out_specs=pl.BlockSpec((tm, d), lambda i: (i, 0)),
        out_shape=jax.ShapeDtypeStruct((n, d), F32),
        compiler_params=_params(1),
        name="merge_sample" if absorbed else "merge_prompt",
    )(*args)


def _ffn_kernel(h_ref, gpre, wg, wu, wd, gpost, y_o):
    h = h_ref[...]
    f = _rms(h, gpre[...]).astype(BF16)
    gate = _dot(f, wg[...])
    act = (gate * _sigmoid(gate) * _dot(f, wu[...])).astype(BF16)
    y_o[...] = h + _rms(_dot(act, wd[...]), gpost[...])


def _ffn(h, w, *, tm):
    n, d = h.shape
    res = lambda a: _resident(a.shape)
    return pl.pallas_call(
        _ffn_kernel,
        grid=(n // tm,),
        in_specs=[pl.BlockSpec((tm, d), lambda i: (i, 0)), res(w["g_ffn_pre"]), res(w["w_ffn_gate"]),
                  res(w["w_ffn_up"]), res(w["w_ffn_down"]), res(w["g_ffn_post"])],
        out_specs=pl.BlockSpec((tm, d), lambda i: (i, 0)),
        out_shape=jax.ShapeDtypeStruct((n, d), F32),
        compiler_params=_params(1),
        name="ffn",
    )(h, w["g_ffn_pre"], w["w_ffn_gate"], w["w_ffn_up"], w["w_ffn_down"], w["g_ffn_post"])


def _rope_tables(pos, ret_dk, d_rope):
    posf = pos.astype(F32)[:, None]

    def angles(half):
        inv = ROPE_BASE ** (-jnp.arange(half, dtype=F32) / half)
        ang = posf * inv[None, :]
        return jnp.cos(ang), jnp.sin(ang)

    cr, sr = angles(ret_dk // 2)
    cp, sp = angles(d_rope // 2)
    z = jnp.zeros_like(cp)
    pad = jnp.zeros((pos.shape[0], LANES - d_rope), F32)
    return {
        "cosr": jnp.concatenate([cr, cr], axis=1),
        "sinr": jnp.concatenate([-sr, sr], axis=1),
        "cosp": jnp.concatenate([cp, cp, pad], axis=1),
        "sinlo": jnp.concatenate([-sp, z, pad], axis=1),
        "sinhi": jnp.concatenate([z, sp, pad], axis=1),
    }


def _layer_weights(l, dims, sizes, norm_mix_pre, norm_mix_post, norm_ffn_pre, norm_ffn_post, norm_mem, norm_q_lat,
                   norm_kv_lat, w_in, w_uq, w_uk, w_uv, w_mem_k, w_mem_v, w_ret_o, w_mla_o, w_x_o, w_out,
                   w_ffn_gate, w_ffn_up, w_ffn_down):
    heads, d_nope, d_rope, kv_lora, d_v = (dims[k] for k in ("heads", "d_nope", "d_rope", "kv_lora", "d_v"))
    bf = lambda a: a.astype(BF16)
    gain = lambda a: a[l].astype(F32)[None, :]
    offs = np.concatenate([[0], np.cumsum(sizes)])
    seg = [w_in[l][:, offs[i]:offs[i + 1]] for i in range(len(sizes))]
    q_lora = w_uq.shape[1]
    uq = w_uq[l].reshape(q_lora, heads, d_nope + d_rope)
    uq_p = jnp.pad(uq[:, :, d_nope:], ((0, 0), (0, 0), (0, LANES - d_rope)))
    return {
        "g_pre": gain(norm_mix_pre), "g_post": gain(norm_mix_post), "g_ffn_pre": gain(norm_ffn_pre),
        "g_ffn_post": gain(norm_ffn_post), "g_mem": gain(norm_mem), "g_q": gain(norm_q_lat),
        "g_kv": gain(norm_kv_lat),
        "w_rq": bf(seg[0]), "w_rk": bf(seg[1]), "w_rv": bf(seg[2]), "w_rg": bf(seg[3]), "w_cq": bf(seg[4]),
        "w_ckv": bf(seg[5]), "w_kpe": bf(jnp.pad(seg[6], ((0, 0), (0, LANES - d_rope)))), "w_xq": bf(seg[7]),
        "w_gates": bf(seg[8]),
        "w_uq_n": bf(uq[:, :, :d_nope].reshape(q_lora, heads * d_nope)),
        "w_uq_p": bf(uq_p.reshape(q_lora, heads * LANES)),
        "w_uk_t": bf(jnp.swapaxes(w_uk[l], 1, 2)),
        "w_kn": bf(jnp.swapaxes(w_uk[l], 0, 1).reshape(kv_lora, heads * d_nope)),
        "w_vn": bf(jnp.swapaxes(w_uv[l], 0, 1).reshape(kv_lora, heads * d_v)),
        "w_uv_h": bf(w_uv[l]),
        "w_mem_k": bf(w_mem_k[l]), "w_mem_v": bf(w_mem_v[l]),
        "w_ret_o": bf(w_ret_o[l]), "w_mla_o": bf(w_mla_o[l]), "w_x_o": bf(w_x_o[l]), "w_out": bf(w_out[l]),
        "w_ffn_gate": bf(w_ffn_gate[l]), "w_ffn_up": bf(w_ffn_up[l]), "w_ffn_down": bf(w_ffn_down[l]),
    }


def _pad_tokens(a, b, tokens, rows=SAMPLE_TOK_PAD):
    a = a.reshape(b, tokens, a.shape[-1])
    return jnp.pad(a, ((0, 0), (0, rows - tokens), (0, 0)))


def kernel(x_prompt, x_sample, mem_prompt, cache_ckv, cache_kpe, page_table, state_ret, cache_mem_k, cache_mem_v,
           norm_mix_pre, norm_mix_post, norm_ffn_pre, norm_ffn_post, norm_mem, norm_q_lat, norm_kv_lat, w_in,
           w_uq, w_uk, w_uv, w_mem_k, w_mem_v, w_ret_o, w_mla_o, w_x_o, w_out, w_ffn_gate, w_ffn_up, w_ffn_down):
    depth = w_in.shape[0]
    batch, seq, d_model = x_prompt.shape
    db, tokens, _ = x_sample.shape
    ret_heads, ret_dk, ret_dv = state_ret.shape[2:]
    heads, kv_lora, d_nope = w_uk.shape[1:]
    d_rope = cache_kpe.shape[-1]
    d_v = w_uv.shape[-1]
    n_mem, x_heads, x_hd = cache_mem_k.shape[2:]
    q_lora = w_uq.shape[1]
    past_len = page_table.shape[1] * cache_ckv.shape[2]
    dims = dict(heads=heads, d_nope=d_nope, d_rope=d_rope, kv_lora=kv_lora, d_v=d_v,
                ret_heads=ret_heads, ret_dk=ret_dk, ret_dv=ret_dv)
    assert ret_dk == LANES and d_nope == LANES and d_rope <= LANES // 2 and tokens <= SAMPLE_TOK_PAD
    sizes = (ret_heads * ret_dk, ret_heads * ret_dk, ret_heads * ret_dv, ret_heads * ret_dv, q_lora, kv_lora,
             d_rope, x_heads * x_hd, w_in.shape[2] - (2 * ret_heads * ret_dk + 2 * ret_heads * ret_dv + q_lora
                                                       + kv_lora + d_rope + x_heads * x_hd))

    cache_kpe_t = jnp.swapaxes(cache_kpe, 2, 3)

    tabs_p = _rope_tables(jnp.arange(seq), ret_dk, d_rope)
    n_s = db * tokens
    tabs_s = _rope_tables(past_len + (jnp.arange(n_s) % tokens), ret_dk, d_rope)

    y_p = x_prompt.reshape(batch * seq, d_model)
    y_s = x_sample.reshape(n_s, d_model)
    outs = [[] for _ in range(8)]
    for l in range(depth):
        w = _layer_weights(l, dims, sizes, norm_mix_pre, norm_mix_post, norm_ffn_pre, norm_ffn_post, norm_mem,
                           norm_q_lat, norm_kv_lat, w_in, w_uq, w_uk, w_uv, w_mem_k, w_mem_v, w_ret_o, w_mla_o,
                           w_x_o, w_out, w_ffn_gate, w_ffn_up, w_ffn_down)

        mk_p, mv_p = _mem_kv(mem_prompt, w["g_mem"], w["w_mem_k"], w["w_mem_v"])
        rq, rk, rv, rg, qcat, xq, ckv_p, kpe_p, kcat, vn = _in_proj(y_p, w, tabs_p, dims, absorbed=False,
                                                                    tm=ROW_TILE)
        o_ret, ret_p = _ret_prompt(rq, rk, rv, rg, batch, seq, dims)
        o_mla = _mla_prompt(qcat, kcat, vn, batch, seq, dims)
        h_p = _merge(y_p, o_ret, o_mla, xq, w, dims, absorbed=False, tm=ROW_TILE, mem=(mk_p, mv_p, x_heads, seq))
        y_p = _ffn(h_p, w, tm=ROW_TILE)

        rq, rk, rv, rg, qcat, xq, ckv_s, kpe_s, qlat = _in_proj(y_s, w, tabs_s, dims, absorbed=True, tm=ROW_TILE)
        pad = lambda a: _pad_tokens(a, db, tokens)
        o_ret, ret_s = _ret_sample(pad(rq), pad(rk), pad(rv), pad(rg), state_ret[l].astype(F32), tokens, dims)
        o_ret = o_ret[:, :tokens].reshape(n_s, ret_heads * ret_dv)
        group = d_nope + LANES
        q_pe = qcat.reshape(db, tokens * heads, group)[:, :, d_nope:d_nope + d_rope]
        o_lat = _mla_sample(qlat.reshape(db, tokens * heads, kv_lora), q_pe, pad(ckv_s), pad(kpe_s),
                            cache_ckv, cache_kpe_t, l, page_table, tokens, dims)
        o_lat = o_lat.reshape(n_s, heads * kv_lora)
        mem_t = lambda c: jnp.transpose(c[l], (0, 2, 3, 1)).reshape(db, x_heads * x_hd, n_mem)
        o_x = _x_sample(_pad_tokens(xq.astype(F32), db, tokens, rows=X_SAMPLE_TOK_PAD), mem_t(cache_mem_k),
                        mem_t(cache_mem_v), x_heads)
        o_x = o_x[:, :tokens].reshape(n_s, x_heads * x_hd)
        h_s = _merge(y_s, o_ret, o_lat, o_x, w, dims, absorbed=True, tm=ROW_TILE)
        y_s = _ffn(h_s, w, tm=ROW_TILE)

        for lst, val in zip(outs, (ckv_p.reshape(batch, seq, kv_lora), kpe_p.reshape(batch, seq, d_rope),
                                   ckv_s.reshape(db, tokens, kv_lora), kpe_s.reshape(db, tokens, d_rope),
                                   ret_p.astype(x_prompt.dtype), ret_s.astype(state_ret.dtype),
                                   mk_p.reshape(batch, n_mem, x_heads, x_hd),
                                   mv_p.reshape(batch, n_mem, x_heads, x_hd))):
            lst.append(val)

    return (y_p.reshape(batch, seq, d_model), y_s.reshape(db, tokens, d_model)) + tuple(jnp.stack(o) for o in outs)
```

```python
import functools
import math

import numpy as np
import jax
import jax.numpy as jnp
from jax import lax
from jax.experimental import pallas as pl
from jax.experimental.pallas import tpu as pltpu

F32 = jnp.float32
BF16 = jnp.bfloat16

ROPE_BASE = 10000.0
RMS_EPS = 1e-6
LANES = 128
VMEM_LIMIT = 56 * 1024 * 1024
NEG_BIG = -1e30

ROW_TILE = 512
RET_CHUNK = 256
ATTN_Q_TILE = 512
ATTN_KV_TILE = 512
ATTN_HEADS_PER_STEP = 8
SAMPLE_TOK_PAD = 16
X_SAMPLE_TOK_PAD = 8
RET_SAMPLE_GROUP = 8
X_SAMPLE_GROUP = 8
DECODE_CHUNK = 4096
DMA_LOOP_UNROLL = 4


def _resident(shape):
    nd = len(shape)
    return pl.BlockSpec(shape, lambda *_: (0,) * nd, pipeline_mode=pl.Buffered(1))


def _params(n_axes, flags=None):
    return pltpu.CompilerParams(dimension_semantics=("arbitrary",) * n_axes, vmem_limit_bytes=VMEM_LIMIT,
                                flags=flags)


def _rms(x, g=None):
    y = x * lax.rsqrt(jnp.mean(x * x, axis=-1, keepdims=True) + RMS_EPS)
    return y if g is None else y * g


def _sigmoid(x):
    return 1.0 / (1.0 + jnp.exp(-x))


def _dot(a, b):
    return jnp.dot(a, b, preferred_element_type=F32)


def _dot_nt(a, b):
    return lax.dot_general(a, b, (((1,), (1,)), ((), ())), preferred_element_type=F32)


def _dot_tn(a, b):
    return lax.dot_general(a, b, (((0,), (0,)), ((), ())), preferred_element_type=F32)


def _rope_half_vreg(z, cos_t, sin_lo, sin_hi, quarter):
    return (z * cos_t + pltpu.roll(z, LANES - quarter, 1) * sin_lo + pltpu.roll(z, quarter, 1) * sin_hi)


def _in_proj_kernel(x_ref, g_ref, wrq, wrk, wrv, wrg, wcq, wckv, wkpe, wxq, gq_ref, gkv_ref, wuqn, wuqp,
                    cosr, sinr, cosp, sinlo, sinhi, *rest,
                    ret_heads, ret_dk, heads, d_nope, d_rope, kv_lora, absorbed, q_scale):
    if absorbed:
        wa, (rq_o, rk_o, rv_o, rg_o, qcat_o, xq_o, ckv_o, kpe_o, a_o) = rest[0], rest[1:]
    else:
        wa, wb, (rq_o, rk_o, rv_o, rg_o, qcat_o, xq_o, ckv_o, kpe_o, a_o, b_o) = rest[0], rest[1], rest[2:]
    u = _rms(x_ref[...], g_ref[...]).astype(BF16)
    cr, sr = cosr[...], sinr[...]
    cp, slo, shi = cosp[...], sinlo[...], sinhi[...]
    group = d_nope + LANES
    k_scale = ret_dk ** -0.5

    cq = _dot(u, wcq[...])
    ckv = _dot(u, wckv[...])
    kpe_raw = _dot(u, wkpe[...])
    zq = _dot(u, wrq[...])
    zk = _dot(u, wrk[...])

    cqn = _rms(cq, gq_ref[...]).astype(BF16)
    ckvn = _rms(ckv, gkv_ref[...])
    ckv_o[...] = ckvn
    kpe = _rope_half_vreg(kpe_raw, cp, slo, shi, d_rope // 2)
    kpe_o[...] = kpe[:, :d_rope]

    qn = _dot(cqn, wuqn[...])
    qp = _dot(cqn, wuqp[...])
    if not absorbed:
        ckvb = ckvn.astype(BF16)
        kn = _dot(ckvb, wa[...])
        vn = _dot(ckvb, wb[...])

    for h in range(ret_heads):
        sl = slice(h * ret_dk, (h + 1) * ret_dk)
        q_h = zq[:, sl]
        k_h = zk[:, sl]
        rq_o[:, sl] = (q_h * cr + pltpu.roll(q_h, ret_dk // 2, 1) * sr).astype(BF16)
        rk_o[:, sl] = ((k_h * cr + pltpu.roll(k_h, ret_dk // 2, 1) * sr) * k_scale).astype(BF16)

    rv = _dot(u, wrv[...])
    rg = _dot(u, wrg[...])
    xq = _dot(u, wxq[...])

    for h in range(heads):
        qn_h = qn[:, h * d_nope:(h + 1) * d_nope]
        qp_h = _rope_half_vreg(qp[:, h * LANES:(h + 1) * LANES], cp, slo, shi, d_rope // 2)
        if absorbed:
            a_o[:, h * kv_lora:(h + 1) * kv_lora] = _dot(qn_h.astype(BF16), wa[h]).astype(BF16)
        else:
            qn_h = qn_h * q_scale
            qp_h = qp_h * q_scale
        qcat_o[:, h * group:h * group + d_nope] = qn_h.astype(BF16)
        qcat_o[:, h * group + d_nope:(h + 1) * group] = qp_h.astype(BF16)
    if not absorbed:
        kpeb = kpe.astype(BF16)
        for h in range(heads):
            a_o[:, h * group:h * group + d_nope] = kn[:, h * d_nope:(h + 1) * d_nope].astype(BF16)
            a_o[:, h * group + d_nope:(h + 1) * group] = kpeb
        b_o[...] = vn.astype(BF16)

    rv_o[...] = rv.astype(BF16)
    rg_o[...] = rg.astype(BF16)
    xq_o[...] = xq.astype(BF16)


def _in_proj(x, w, tabs, dims, *, absorbed, tm):
    n, d_model = x.shape
    heads, d_nope, d_rope, kv_lora = dims["heads"], dims["d_nope"], dims["d_rope"], dims["kv_lora"]
    ret_heads, ret_dk, ret_dv = dims["ret_heads"], dims["ret_dk"], dims["ret_dv"]
    group = d_nope + LANES
    tab_rows = tabs["cosr"].shape[0]
    tab_tiles = tab_rows // tm
    row = lambda c: pl.BlockSpec((tm, c), lambda i: (i, 0))
    tab = lambda: pl.BlockSpec((tm, LANES), lambda i: (i % tab_tiles, 0))
    weights = [w["g_pre"], w["w_rq"], w["w_rk"], w["w_rv"], w["w_rg"], w["w_cq"], w["w_ckv"], w["w_kpe"],
               w["w_xq"], w["g_q"], w["g_kv"], w["w_uq_n"], w["w_uq_p"]]
    out_cols = [(ret_heads * ret_dk, BF16), (ret_heads * ret_dk, BF16), (ret_heads * ret_dv, BF16),
                (ret_heads * ret_dv, BF16), (heads * group, BF16), (w["w_xq"].shape[1], BF16), (kv_lora, F32),
                (d_rope, F32)]
    if absorbed:
        mode_weights = [w["w_uk_t"]]
        out_cols += [(heads * kv_lora, BF16)]
    else:
        mode_weights = [w["w_kn"], w["w_vn"]]
        out_cols += [(heads * group, BF16), (w["w_vn"].shape[1], BF16)]
    kern = functools.partial(_in_proj_kernel, ret_heads=ret_heads, ret_dk=ret_dk, heads=heads, d_nope=d_nope,
                             d_rope=d_rope, kv_lora=kv_lora, absorbed=absorbed,
                             q_scale=(d_nope + d_rope) ** -0.5 * math.log2(math.e))
    return pl.pallas_call(
        kern,
        grid=(n // tm,),
        in_specs=([row(d_model)] + [_resident(a.shape) for a in weights] + [tab() for _ in range(5)]
                  + [_resident(a.shape) for a in mode_weights]),
        out_specs=[row(c) for c, _ in out_cols],
        out_shape=[jax.ShapeDtypeStruct((n, c), dt) for c, dt in out_cols],
        compiler_params=_params(1),
        name="in_proj_sample" if absorbed else "in_proj_prompt",
    )(x, *weights, tabs["cosr"], tabs["sinr"], tabs["cosp"], tabs["sinlo"], tabs["sinhi"], *mode_weights)


def _ret_head(q, k, v, g, s, dec, qd, kd, g_l):
    inner = _dot_nt(q, k) * dec
    o = _dot(inner.astype(BF16), v) + _dot((q.astype(F32) * qd).astype(BF16), s.astype(BF16))
    s_new = s * g_l + _dot_tn((k.astype(F32) * kd).astype(BF16), v)
    gf = g.astype(F32)
    o = (gf * _sigmoid(gf)) * _rms(o)
    return o.astype(BF16), s_new


def _ret_prompt_kernel(rq, rk, rv, rg, dec, qd, kd, o_ref, s_out, s_scr, *, heads, dk, dv, g_l):
    c = pl.program_id(1)

    @pl.when(c == 0)
    def _():
        s_scr[...] = jnp.zeros(s_scr.shape, F32)

    for h in range(heads):
        o, s_new = _ret_head(rq[:, h * dk:(h + 1) * dk], rk[:, h * dk:(h + 1) * dk],
                             rv[:, h * dv:(h + 1) * dv], rg[:, h * dv:(h + 1) * dv],
                             s_scr[h], dec[h], qd[h], kd[h], g_l[h])
        o_ref[:, h * dv:(h + 1) * dv] = o
        s_scr[h] = s_new

    @pl.when(c == pl.num_programs(1) - 1)
    def _():
        s_out[0] = s_scr[...]


def _ret_consts(heads, length, dk, chunk_rows=None):
    rows = length if chunk_rows is None else chunk_rows
    lg = np.log1p(-np.exp2(-5.0 - np.arange(heads, dtype=np.float64)))
    i = np.arange(rows, dtype=np.float64)
    diff = i[:, None] - i[None, :]
    valid = (diff >= 0) & (i[:, None] < length) & (i[None, :] < length)
    dec = np.where(valid[None], np.exp(np.maximum(diff, 0.0)[None] * lg[:, None, None]), 0.0)
    qd = np.exp((i[None, :] + 1.0) * lg[:, None])
    kd = np.where(i[None, :] < length, np.exp((length - 1.0 - i[None, :]) * lg[:, None]), 0.0)
    qd = np.broadcast_to(qd[:, :, None], (heads, rows, dk))
    kd = np.broadcast_to(kd[:, :, None], (heads, rows, dk))
    g_l = tuple(float(v) for v in np.exp(length * lg))
    return (jnp.asarray(dec, F32), jnp.asarray(qd, F32), jnp.asarray(kd, F32), g_l)


def _ret_prompt(rq, rk, rv, rg, batch, seq, dims):
    heads, dk, dv = dims["ret_heads"], dims["ret_dk"], dims["ret_dv"]
    L = RET_CHUNK
    nc = seq // L
    dec, qd, kd, g_l = _ret_consts(heads, L, dk)
    row = lambda c: pl.BlockSpec((L, c), lambda b, i: (b * nc + i, 0))
    kern = functools.partial(_ret_prompt_kernel, heads=heads, dk=dk, dv=dv, g_l=g_l)
    return pl.pallas_call(
        kern,
        grid=(batch, nc),
        in_specs=[row(heads * dk), row(heads * dk), row(heads * dv), row(heads * dv),
                  _resident(dec.shape), _resident(qd.shape), _resident(kd.shape)],
        out_specs=[row(heads * dv), pl.BlockSpec((1, heads, dk, dv), lambda b, i: (b, 0, 0, 0))],
        out_shape=[jax.ShapeDtypeStruct((batch * seq, heads * dv), BF16),
                   jax.ShapeDtypeStruct((batch, heads, dk, dv), F32)],
        scratch_shapes=[pltpu.VMEM((heads, dk, dv), F32)],
        compiler_params=_params(2),
        name="retention_prompt",
    )(rq, rk, rv, rg, dec, qd, kd)


def _ret_sample_kernel(rq, rk, rv, rg, s0, dec, qd, kd, o_ref, s_out, *, group, heads, dk, dv, g_l):
    pairs = [(i, h) for i in range(group) for h in range(heads)]

    def state_side(i, h):
        q = rq[i, :, h * dk:(h + 1) * dk]
        k = rk[i, :, h * dk:(h + 1) * dk]
        v = rv[i, :, h * dv:(h + 1) * dv]
        s = s0[i, h]
        read = _dot((q.astype(F32) * qd[h]).astype(BF16), s.astype(BF16))
        s_out[i, h] = s * g_l[h] + _dot_tn((k.astype(F32) * kd[h]).astype(BF16), v)
        return read

    def output_side(i, h, read):
        q = rq[i, :, h * dk:(h + 1) * dk]
        k = rk[i, :, h * dk:(h + 1) * dk]
        v = rv[i, :, h * dv:(h + 1) * dv]
        o = _dot((_dot_nt(q, k) * dec[h]).astype(BF16), v) + read
        gf = rg[i, :, h * dv:(h + 1) * dv].astype(F32)
        o_ref[i, :, h * dv:(h + 1) * dv] = ((gf * _sigmoid(gf)) * _rms(o)).astype(BF16)

    read = state_side(*pairs[0])
    for n, (i, h) in enumerate(pairs):
        nxt = state_side(*pairs[n + 1]) if n + 1 < len(pairs) else None
        output_side(i, h, read)
        read = nxt


def _ret_sample(rq, rk, rv, rg, state, tokens, dims):
    heads, dk, dv = dims["ret_heads"], dims["ret_dk"], dims["ret_dv"]
    b, tp, _ = rq.shape
    G = RET_SAMPLE_GROUP
    dec, qd, kd, g_l = _ret_consts(heads, tokens, dk, chunk_rows=tp)
    blk = lambda c: pl.BlockSpec((G, tp, c), lambda i: (i, 0, 0))
    st = pl.BlockSpec((G, heads, dk, dv), lambda i: (i, 0, 0, 0))
    kern = functools.partial(_ret_sample_kernel, group=G, heads=heads, dk=dk, dv=dv, g_l=g_l)
    return pl.pallas_call(
        kern,
        grid=(b // G,),
        in_specs=[blk(heads * dk), blk(heads * dk), blk(heads * dv), blk(heads * dv), st,
                  _resident(dec.shape), _resident(qd.shape), _resident(kd.shape)],
        out_specs=[blk(heads * dv), st],
        out_shape=[jax.ShapeDtypeStruct((b, tp, heads * dv), BF16),
                   jax.ShapeDtypeStruct((b, heads, dk, dv), F32)],
        compiler_params=_params(1),
        name="retention_sample",
    )(rq, rk, rv, rg, state, dec, qd, kd)


def _mla_prompt_kernel(q_ref, k_ref, v_ref, o_ref, m_scr, l_scr, acc_scr, *, tq, tk, hp, group, d_v):
    qi = pl.program_id(2)
    ri = lax.broadcasted_iota(jnp.int32, (tq, tk), 0)
    ci = lax.broadcasted_iota(jnp.int32, (tq, tk), 1)
    causal_bias = jnp.where(ci <= ri, 0.0, NEG_BIG)

    def block(j, carry, first):
        start = pl.multiple_of(j * tk, tk)

        def scores(h):
            return _dot_nt(q_ref[:, h * group:(h + 1) * group], k_ref[pl.ds(start, tk), h * group:(h + 1) * group])

        s_next = scores(0)
        for h in range(hp):
            s = s_next
            if h + 1 < hp:
                s_next = scores(h + 1)
            vals = v_ref[pl.ds(start, tk), h * d_v:(h + 1) * d_v]
            if first:
                s = s + causal_bias
                m_new = jnp.broadcast_to(jnp.max(s, axis=-1, keepdims=True), (tq, LANES))
                p = jnp.exp2(s - jnp.tile(m_new, (1, tk // LANES)))
                l_scr[h] = jnp.broadcast_to(jnp.sum(p, axis=-1, keepdims=True), (tq, LANES))
                acc_scr[h] = _dot(p.astype(BF16), vals)
            else:
                m_prev = m_scr[h]
                m_new = jnp.maximum(m_prev, jnp.max(s, axis=-1, keepdims=True))
                a = jnp.exp2(m_prev - m_new)
                p = jnp.exp2(s - jnp.tile(m_new, (1, tk // LANES)))
                l_scr[h] = a * l_scr[h] + jnp.sum(p, axis=-1, keepdims=True)
                acc_scr[h] = a * acc_scr[h] + _dot(p.astype(BF16), vals)
            m_scr[h] = m_new
        return carry

    block(qi, 0, True)
    lax.fori_loop(0, qi, functools.partial(block, first=False), 0)
    for h in range(hp):
        o_ref[:, h * d_v:(h + 1) * d_v] = (acc_scr[h] / l_scr[h]).astype(BF16)


def _mla_prompt(qcat, kcat, v, batch, seq, dims):
    heads, d_nope, d_v = dims["heads"], dims["d_nope"], dims["d_v"]
    group = d_nope + LANES
    t = ATTN_Q_TILE
    hp = ATTN_HEADS_PER_STEP
    nq = seq // t
    assert ATTN_KV_TILE == t and d_v == LANES
    kern = functools.partial(_mla_prompt_kernel, tq=t, tk=ATTN_KV_TILE, hp=hp, group=group, d_v=d_v)
    return pl.pallas_call(
        kern,
        grid=(batch, heads // hp, nq),
        in_specs=[pl.BlockSpec((t, hp * group), lambda b, h, i: (b * nq + i, h)),
                  pl.BlockSpec((seq, hp * group), lambda b, h, i: (b, h)),
                  pl.BlockSpec((seq, hp * d_v), lambda b, h, i: (b, h))],
        out_specs=pl.BlockSpec((t, hp * d_v), lambda b, h, i: (b * nq + i, h)),
        out_shape=jax.ShapeDtypeStruct((batch * seq, heads * d_v), BF16),
        scratch_shapes=[pltpu.VMEM((hp, t, LANES), F32), pltpu.VMEM((hp, t, LANES), F32),
                        pltpu.VMEM((hp, t, d_v), F32)],
        compiler_params=_params(3),
        name="mla_prompt",
    )(qcat, kcat, v)


def _mla_sample_kernel(pt_ref, ql_ref, qp_ref, cn_ref, kn_ref, ckv_hbm, kpe_hbm, o_ref,
                       ckv_buf, kpe_buf, kbf, s_scr, sem,
                       *, layer, n_pages, page, chunk, tokens, heads, scale):
    b = pl.program_id(0)
    past = n_pages * page
    d_rope = qp_ref.shape[2]

    def page_copy(seq, slot, p, array):
        pg = pt_ref[seq, p]
        if array == 0:
            off = pl.multiple_of(p * page, page)
            return pltpu.make_async_copy(ckv_hbm.at[layer, pg], ckv_buf.at[slot, pl.ds(off, page), :],
                                         sem.at[0, slot])
        off = pl.multiple_of(p * d_rope, d_rope)
        return pltpu.make_async_copy(kpe_hbm.at[layer, pg], kpe_buf.at[slot, pl.ds(off, d_rope), :],
                                     sem.at[1, slot])

    def for_each_page(seq, slot, fn, arrays=(0, 1), unroll=DMA_LOOP_UNROLL):
        def body(p, carry):
            for a in arrays:
                fn(page_copy(seq, slot, p, a))
            return carry
        lax.fori_loop(0, n_pages, body, 0, unroll=unroll)

    @pl.when(b == 0)
    def _():
        for_each_page(0, 0, lambda cp: cp.start())

    @pl.when(b + 1 < pl.num_programs(0))
    def _():
        for_each_page(b + 1, (b + 1) % 2, lambda cp: cp.start())

    slot = b % 2
    for_each_page(b, slot, lambda cp: cp.wait(), arrays=(0,), unroll=True)
    for_each_page(b, slot, lambda cp: cp.wait(), arrays=(1,), unroll=True)

    ql = ql_ref[0]
    qp = qp_ref[0]
    ckv_s = ckv_buf.at[slot]
    kpe_s = kpe_buf.at[slot]
    n_chunks = past // chunk

    cn = cn_ref[0].astype(BF16)
    kn = kn_ref[0].astype(BF16)
    s_new = (_dot_nt(ql, cn) + _dot_nt(qp, kn)) * scale
    t = lax.broadcasted_iota(jnp.int32, s_new.shape, 0) // heads
    col = lax.broadcasted_iota(jnp.int32, s_new.shape, 1)
    s_new = jnp.where((col <= t) & (col < tokens), s_new, NEG_BIG)
    m = jnp.max(s_new, axis=-1, keepdims=True)
    p_new = jnp.exp(s_new - m)
    l = jnp.sum(p_new, axis=-1, keepdims=True)
    acc = _dot(p_new.astype(BF16), cn)

    def scores(c):
        sl = slice(c * chunk, (c + 1) * chunk)
        kc = ckv_s[sl, :].astype(BF16)
        kbf[sl, :] = kc
        pe = jnp.concatenate([kpe_s[p * d_rope:(p + 1) * d_rope, :]
                              for p in range(c * chunk // page, (c + 1) * chunk // page)], axis=1)
        s_scr[:, sl] = (_dot_nt(ql, kc) + _dot(qp, pe.astype(BF16))) * scale

    def values(c, m, l, acc):
        sl = slice(c * chunk, (c + 1) * chunk)
        s = s_scr[:, sl]
        m_new = jnp.maximum(m, jnp.max(s, axis=-1, keepdims=True))
        a = jnp.exp(m - m_new)
        p = jnp.exp(s - m_new)
        l = a * l + jnp.sum(p, axis=-1, keepdims=True)
        return m_new, l, a * acc + _dot(p.astype(BF16), kbf[sl, :])

    scores(0)
    for c in range(n_chunks):
        if c + 1 < n_chunks:
            scores(c + 1)
        m, l, acc = values(c, m, l, acc)
    o_ref[0] = acc / l


def _mla_sample(ql, qp, ckv_new, kpe_new, cache_ckv, cache_kpe_t, layer, page_table, tokens, dims):
    heads, d_nope, d_rope, kv_lora = dims["heads"], dims["d_nope"], dims["d_rope"], dims["kv_lora"]
    b, rows, _ = ql.shape
    n_pages = page_table.shape[1]
    page = cache_ckv.shape[2]
    past = n_pages * page
    tp = ckv_new.shape[1]
    per_b = lambda r, c: pl.BlockSpec((1, r, c), lambda i, pt: (i, 0, 0))
    hbm = pl.BlockSpec(memory_space=pl.ANY)
    kern = functools.partial(_mla_sample_kernel, layer=layer, n_pages=n_pages, page=page, chunk=DECODE_CHUNK,
                             tokens=tokens, heads=heads, scale=(d_nope + d_rope) ** -0.5)
    grid_spec = pltpu.PrefetchScalarGridSpec(
        num_scalar_prefetch=1,
        grid=(b,),
        in_specs=[per_b(rows, kv_lora), per_b(rows, d_rope), per_b(tp, kv_lora), per_b(tp, d_rope), hbm, hbm],
        out_specs=per_b(rows, kv_lora),
        scratch_shapes=[pltpu.VMEM((2, past, kv_lora), F32), pltpu.VMEM((2, n_pages * d_rope, page), F32),
                        pltpu.VMEM((past, kv_lora), BF16), pltpu.VMEM((rows, past), F32),
                        pltpu.SemaphoreType.DMA((2, 2))],
    )
    return pl.pallas_call(
        kern,
        grid_spec=grid_spec,
        out_shape=jax.ShapeDtypeStruct((b, rows, kv_lora), F32),
        compiler_params=_params(1),
        name="mla_sample",
    )(page_table, ql, qp, ckv_new, kpe_new, cache_ckv, cache_kpe_t)


def _mem_kv_kernel(m_ref, g_ref, wk, wv, k_o, v_o):
    mn = _rms(m_ref[0], g_ref[...]).astype(BF16)
    k_o[0] = _dot(mn, wk[...])
    v_o[0] = _dot(mn, wv[...])


def _mem_kv(mem, g, wk, wv):
    b, m, d = mem.shape
    c = wk.shape[1]
    return pl.pallas_call(
        _mem_kv_kernel,
        grid=(b,),
        in_specs=[pl.BlockSpec((1, m, d), lambda i: (i, 0, 0)), _resident(g.shape), _resident(wk.shape),
                  _resident(wv.shape)],
        out_specs=[pl.BlockSpec((1, m, c), lambda i: (i, 0, 0))] * 2,
        out_shape=[jax.ShapeDtypeStruct((b, m, c), F32)] * 2,
        compiler_params=_params(1),
        name="mem_kv",
    )(mem, g, wk, wv)


def _x_attend_heads(q, mk, mv, heads, hd):
    r = q.shape[0]
    lane = lax.broadcasted_iota(jnp.int32, q.shape, 1)
    sels = [(lane >= h * hd) & (lane < (h + 1) * hd) for h in range(heads)]
    q_heads = jnp.concatenate([jnp.where(sel, q, jnp.zeros_like(q)) for sel in sels], axis=0)
    s = _dot_nt(q_heads, mk) * (hd ** -0.5)
    p = jnp.exp(s - jnp.max(s, axis=-1, keepdims=True))
    p = p / jnp.sum(p, axis=-1, keepdims=True)
    pv = _dot(p.astype(BF16), mv)
    out = jnp.zeros(q.shape, F32)
    for h, sel in enumerate(sels):
        out = out + jnp.where(sel, pv[h * r:(h + 1) * r], 0.0)
    return out


def _x_sample_kernel(q_ref, kt_ref, vt_ref, o_ref, *, group, heads, hd):
    tp = q_ref.shape[1]
    lane = lax.broadcasted_iota(jnp.int32, q_ref.shape[1:], 1)
    sels = [(lane >= h * hd) & (lane < (h + 1) * hd) for h in range(heads)]
    def scores(g):
        q_heads = jnp.concatenate([jnp.where(sel, q_ref[g], 0.0) for sel in sels], axis=0).astype(BF16)
        return _dot(q_heads, kt_ref[g].astype(BF16)) * (hd ** -0.5)

    s_next = scores(0)
    for g in range(group):
        q = q_ref[g]
        s = s_next
        if g + 1 < group:
            s_next = scores(g + 1)
        p = jnp.exp(s - jnp.max(s, axis=-1, keepdims=True))
        p = p / jnp.sum(p, axis=-1, keepdims=True)
        r = _dot_nt(p.astype(BF16), vt_ref[g].astype(BF16))
        out = jnp.zeros(q.shape, F32)
        for h, sel in enumerate(sels):
            out = out + jnp.where(sel, r[h * tp:(h + 1) * tp], 0.0)
        o_ref[g] = out


def _x_sample(xq, mkt, mvt, heads):
    b, tp, c = xq.shape
    m = mkt.shape[2]
    G = X_SAMPLE_GROUP
    kern = functools.partial(_x_sample_kernel, group=G, heads=heads, hd=c // heads)
    return pl.pallas_call(
        kern,
        grid=(b // G,),
        in_specs=[pl.BlockSpec((G, tp, c), lambda i: (i, 0, 0)),
                  pl.BlockSpec((G, c, m), lambda i: (i, 0, 0)),
                  pl.BlockSpec((G, c, m), lambda i: (i, 0, 0))],
        out_specs=pl.BlockSpec((G, tp, c), lambda i: (i, 0, 0)),
        out_shape=jax.ShapeDtypeStruct((b, tp, c), F32),
        compiler_params=_params(1),
        name="x_attend_sample",
    )(xq, mkt, mvt)


def _merge_kernel(x_ref, gpre, wgate, oret, wret, omla, wmla, wuv, ox, wx, wout, gpost, *rest,
                  heads, kv_lora, d_v, absorbed, x_heads):
    h_o = rest[-1]
    if x_heads:
        mk, mv = rest[:2]
        o_x = _x_attend_heads(ox[...], mk[0].astype(BF16), mv[0].astype(BF16), x_heads, ox.shape[1] // x_heads)
    else:
        o_x = ox[...]
    x = x_ref[...]
    d = x.shape[1]
    u = _rms(x, gpre[...]).astype(BF16)
    a_ret = _dot(oret[...], wret[...])
    if absorbed:
        a_mla = jnp.zeros(x.shape, F32)
        for h in range(heads):
            o_h = _dot(omla[:, h * kv_lora:(h + 1) * kv_lora].astype(BF16), wuv[h]).astype(BF16)
            a_mla = a_mla + _dot(o_h, wmla[h * d_v:(h + 1) * d_v, :])
    else:
        a_mla = _dot(omla[...], wmla[...])
    a_x = _dot(o_x.astype(BF16), wx[...])
    mixed = jnp.zeros(x.shape, F32)
    for i, a in enumerate((a_ret, a_mla, a_x)):
        mixed = mixed + _sigmoid(_dot(u, wgate[:, i * d:(i + 1) * d])) * a
    h_o[...] = x + _rms(_dot(mixed.astype(BF16), wout[...]), gpost[...])


def _merge(x, oret, omla, ox, w, dims, *, absorbed, tm, mem=None):
    n, d = x.shape
    row = lambda a: pl.BlockSpec((tm, a.shape[1]), lambda i: (i, 0))
    res = lambda a: _resident(a.shape)
    args = [x, w["g_pre"], w["w_gates"], oret, w["w_ret_o"], omla, w["w_mla_o"], w["w_uv_h"], ox, w["w_x_o"],
            w["w_out"], w["g_post"]]
    specs = [row(x), res(w["g_pre"]), res(w["w_gates"]), row(oret), res(w["w_ret_o"]), row(omla),
             res(w["w_mla_o"]), res(w["w_uv_h"]), row(ox), res(w["w_x_o"]), res(w["w_out"]), res(w["g_post"])]
    x_heads = 0
    if mem is not None:
        mk, mv, x_heads, seq_rows = mem
        tiles_per_seq = seq_rows // tm
        per_seq = pl.BlockSpec((1,) + mk.shape[1:], lambda i: (i // tiles_per_seq, 0, 0))
        args += [mk, mv]
        specs += [per_seq, per_seq]
    kern = functools.partial(_merge_kernel, heads=dims["heads"], kv_lora=dims["kv_lora"], d_v=dims["d_v"],
                             absorbed=absorbed, x_heads=x_heads)
    return pl.pallas_call(
        kern,
        grid=(n // tm,),
        in_specs=specs,
        out_specs=pl.BlockSpec((tm, d), lambda i: (i, 0)),
        out_shape=jax.ShapeDtypeStruct((n, d), F32),
        compiler_params=_params(1),
        name="merge_sample" if absorbed else "merge_prompt",
    )(*args)


def _ffn_kernel(h_ref, gpre, wg, wu, wd, gpost, y_o):
    h = h_ref[...]
    f = _rms(h, gpre[...]).astype(BF16)
    gate = _dot(f, wg[...])
    act = (gate * _sigmoid(gate) * _dot(f, wu[...])).astype(BF16)
    y_o[...] = h + _rms(_dot(act, wd[...]), gpost[...])


def _ffn(h, w, *, tm):
    n, d = h.shape
    res = lambda a: _resident(a.shape)
    return pl.pallas_call(
        _ffn_kernel,
        grid=(n // tm,),
        in_specs=[pl.BlockSpec((tm, d), lambda i: (i, 0)), res(w["g_ffn_pre"]), res(w["w_ffn_gate"]),
                  res(w["w_ffn_up"]), res(w["w_ffn_down"]), res(w["g_ffn_post"])],
        out_specs=pl.BlockSpec((tm, d), lambda i: (i, 0)),
        out_shape=jax.ShapeDtypeStruct((n, d), F32),
        compiler_params=_params(1),
        name="ffn",
    )(h, w["g_ffn_pre"], w["w_ffn_gate"], w["w_ffn_up"], w["w_ffn_down"], w["g_ffn_post"])


def _rope_tables(pos, ret_dk, d_rope):
    posf = pos.astype(F32)[:, None]

    def angles(half):
        inv = ROPE_BASE ** (-jnp.arange(half, dtype=F32) / half)
        ang = posf * inv[None, :]
        return jnp.cos(ang), jnp.sin(ang)

    cr, sr = angles(ret_dk // 2)
    cp, sp = angles(d_rope // 2)
    z = jnp.zeros_like(cp)
    pad = jnp.zeros((pos.shape[0], LANES - d_rope), F32)
    return {
        "cosr": jnp.concatenate([cr, cr], axis=1),
        "sinr": jnp.concatenate([-sr, sr], axis=1),
        "cosp": jnp.concatenate([cp, cp, pad], axis=1),
        "sinlo": jnp.concatenate([-sp, z, pad], axis=1),
        "sinhi": jnp.concatenate([z, sp, pad], axis=1),
    }


def _layer_weights(l, dims, sizes, norm_mix_pre, norm_mix_post, norm_ffn_pre, norm_ffn_post, norm_mem, norm_q_lat,
                   norm_kv_lat, w_in, w_uq, w_uk, w_uv, w_mem_k, w_mem_v, w_ret_o, w_mla_o, w_x_o, w_out,
                   w_ffn_gate, w_ffn_up, w_ffn_down):
    heads, d_nope, d_rope, kv_lora, d_v = (dims[k] for k in ("heads", "d_nope", "d_rope", "kv_lora", "d_v"))
    bf = lambda a: a.astype(BF16)
    gain = lambda a: a[l].astype(F32)[None, :]
    offs = np.concatenate([[0], np.cumsum(sizes)])
    seg = [w_in[l][:, offs[i]:offs[i + 1]] for i in range(len(sizes))]
    q_lora = w_uq.shape[1]
    uq = w_uq[l].reshape(q_lora, heads, d_nope + d_rope)
    uq_p = jnp.pad(uq[:, :, d_nope:], ((0, 0), (0, 0), (0, LANES - d_rope)))
    return {
        "g_pre": gain(norm_mix_pre), "g_post": gain(norm_mix_post), "g_ffn_pre": gain(norm_ffn_pre),
        "g_ffn_post": gain(norm_ffn_post), "g_mem": gain(norm_mem), "g_q": gain(norm_q_lat),
        "g_kv": gain(norm_kv_lat),
        "w_rq": bf(seg[0]), "w_rk": bf(seg[1]), "w_rv": bf(seg[2]), "w_rg": bf(seg[3]), "w_cq": bf(seg[4]),
        "w_ckv": bf(seg[5]), "w_kpe": bf(jnp.pad(seg[6], ((0, 0), (0, LANES - d_rope)))), "w_xq": bf(seg[7]),
        "w_gates": bf(seg[8]),
        "w_uq_n": bf(uq[:, :, :d_nope].reshape(q_lora, heads * d_nope)),
        "w_uq_p": bf(uq_p.reshape(q_lora, heads * LANES)),
        "w_uk_t": bf(jnp.swapaxes(w_uk[l], 1, 2)),
        "w_kn": bf(jnp.swapaxes(w_uk[l], 0, 1).reshape(kv_lora, heads * d_nope)),
        "w_vn": bf(jnp.swapaxes(w_uv[l], 0, 1).reshape(kv_lora, heads * d_v)),
        "w_uv_h": bf(w_uv[l]),
        "w_mem_k": bf(w_mem_k[l]), "w_mem_v": bf(w_mem_v[l]),
        "w_ret_o": bf(w_ret_o[l]), "w_mla_o": bf(w_mla_o[l]), "w_x_o": bf(w_x_o[l]), "w_out": bf(w_out[l]),
        "w_ffn_gate": bf(w_ffn_gate[l]), "w_ffn_up": bf(w_ffn_up[l]), "w_ffn_down": bf(w_ffn_down[l]),
    }


def _pad_tokens(a, b, tokens, rows=SAMPLE_TOK_PAD):
    a = a.reshape(b, tokens, a.shape[-1])
    return jnp.pad(a, ((0, 0), (0, rows - tokens), (0, 0)))


def kernel(x_prompt, x_sample, mem_prompt, cache_ckv, cache_kpe, page_table, state_ret, cache_mem_k, cache_mem_v,
           norm_mix_pre, norm_mix_post, norm_ffn_pre, norm_ffn_post, norm_mem, norm_q_lat, norm_kv_lat, w_in,
           w_uq, w_uk, w_uv, w_mem_k, w_mem_v, w_ret_o, w_mla_o, w_x_o, w_out, w_ffn_gate, w_ffn_up, w_ffn_down):
    depth = w_in.shape[0]
    batch, seq, d_model = x_prompt.shape
    db, tokens, _ = x_sample.shape
    ret_heads, ret_dk, ret_dv = state_ret.shape[2:]
    heads, kv_lora, d_nope = w_uk.shape[1:]
    d_rope = cache_kpe.shape[-1]
    d_v = w_uv.shape[-1]
    n_mem, x_heads, x_hd = cache_mem_k.shape[2:]
    q_lora = w_uq.shape[1]
    past_len = page_table.shape[1] * cache_ckv.shape[2]
    dims = dict(heads=heads, d_nope=d_nope, d_rope=d_rope, kv_lora=kv_lora, d_v=d_v,
                ret_heads=ret_heads, ret_dk=ret_dk, ret_dv=ret_dv)
    assert ret_dk == LANES and d_nope == LANES and d_rope <= LANES // 2 and tokens <= SAMPLE_TOK_PAD
    sizes = (ret_heads * ret_dk, ret_heads * ret_dk, ret_heads * ret_dv, ret_heads * ret_dv, q_lora, kv_lora,
             d_rope, x_heads * x_hd, w_in.shape[2] - (2 * ret_heads * ret_dk + 2 * ret_heads * ret_dv + q_lora
                                                       + kv_lora + d_rope + x_heads * x_hd))

    cache_kpe_t = jnp.swapaxes(cache_kpe, 2, 3)

    tabs_p = _rope_tables(jnp.arange(seq), ret_dk, d_rope)
    n_s = db * tokens
    tabs_s = _rope_tables(past_len + (jnp.arange(n_s) % tokens), ret_dk, d_rope)

    y_p = x_prompt.reshape(batch * seq, d_model)
    y_s = x_sample.reshape(n_s, d_model)
    outs = [[] for _ in range(8)]
    for l in range(depth):
        w = _layer_weights(l, dims, sizes, norm_mix_pre, norm_mix_post, norm_ffn_pre, norm_ffn_post, norm_mem,
                           norm_q_lat, norm_kv_lat, w_in, w_uq, w_uk, w_uv, w_mem_k, w_mem_v, w_ret_o, w_mla_o,
                           w_x_o, w_out, w_ffn_gate, w_ffn_up, w_ffn_down)

        mk_p, mv_p = _mem_kv(mem_prompt, w["g_mem"], w["w_mem_k"], w["w_mem_v"])
        rq, rk, rv, rg, qcat, xq, ckv_p, kpe_p, kcat, vn = _in_proj(y_p, w, tabs_p, dims, absorbed=False,
                                                                    tm=ROW_TILE)
        o_ret, ret_p = _ret_prompt(rq, rk, rv, rg, batch, seq, dims)
        o_mla = _mla_prompt(qcat, kcat, vn, batch, seq, dims)
        h_p = _merge(y_p, o_ret, o_mla, xq, w, dims, absorbed=False, tm=ROW_TILE, mem=(mk_p, mv_p, x_heads, seq))
        y_p = _ffn(h_p, w, tm=ROW_TILE)

        rq, rk, rv, rg, qcat, xq, ckv_s, kpe_s, qlat = _in_proj(y_s, w, tabs_s, dims, absorbed=True, tm=ROW_TILE)
        pad = lambda a: _pad_tokens(a, db, tokens)
        o_ret, ret_s = _ret_sample(pad(rq), pad(rk), pad(rv), pad(rg), state_ret[l].astype(F32), tokens, dims)
        o_ret = o_ret[:, :tokens].reshape(n_s, ret_heads * ret_dv)
        group = d_nope + LANES
        q_pe = qcat.reshape(db, tokens * heads, group)[:, :, d_nope:d_nope + d_rope]
        o_lat = _mla_sample(qlat.reshape(db, tokens * heads, kv_lora), q_pe, pad(ckv_s), pad(kpe_s),
                            cache_ckv, cache_kpe_t, l, page_table, tokens, dims)
        o_lat = o_lat.reshape(n_s, heads * kv_lora)
        mem_t = lambda c: jnp.transpose(c[l], (0, 2, 3, 1)).reshape(db, x_heads * x_hd, n_mem)
        o_x = _x_sample(_pad_tokens(xq.astype(F32), db, tokens, rows=X_SAMPLE_TOK_PAD), mem_t(cache_mem_k),
                        mem_t(cache_mem_v), x_heads)
        o_x = o_x[:, :tokens].reshape(n_s, x_heads * x_hd)
        h_s = _merge(y_s, o_ret, o_lat, o_x, w, dims, absorbed=True, tm=ROW_TILE)
        y_s = _ffn(h_s, w, tm=ROW_TILE)

        for lst, val in zip(outs, (ckv_p.reshape(batch, seq, kv_lora), kpe_p.reshape(batch, seq, d_rope),
                                   ckv_s.reshape(db, tokens, kv_lora), kpe_s.reshape(db, tokens, d_rope),
                                   ret_p.astype(x_prompt.dtype), ret_s.astype(state_ret.dtype),
                                   mk_p.reshape(batch, n_mem, x_heads, x_hd),
                                   mv_p.reshape(batch, n_mem, x_heads, x_hd))):
            lst.append(val)

    return (y_p.reshape(batch, seq, d_model), y_s.reshape(db, tokens, d_model)) + tuple(jnp.stack(o) for o in outs)
```

```python
import functools
import math

import numpy as np
import jax
import jax.numpy as jnp
from jax import lax
from jax.experimental import pallas as pl
from jax.experimental.pallas import tpu as pltpu

F32 = jnp.float32
BF16 = jnp.bfloat16

ROPE_BASE = 10000.0
RMS_EPS = 1e-6
LANES = 128
VMEM_LIMIT = 56 * 1024 * 1024
NEG_BIG = -1e30

ROW_TILE = 512
RET_CHUNK = 256
ATTN_Q_TILE = 512
ATTN_KV_TILE = 512
ATTN_HEADS_PER_STEP = 8
SAMPLE_TOK_PAD = 16
X_SAMPLE_TOK_PAD = 8
RET_SAMPLE_GROUP = 8
X_SAMPLE_GROUP = 8
DECODE_CHUNK = 4096
DMA_LOOP_UNROLL = 4


def _resident(shape):
    nd = len(shape)
    return pl.BlockSpec(shape, lambda *_: (0,) * nd, pipeline_mode=pl.Buffered(1))


def _params(n_axes):
    return pltpu.CompilerParams(dimension_semantics=("arbitrary",) * n_axes, vmem_limit_bytes=VMEM_LIMIT)


def _rms(x, g=None):
    y = x * lax.rsqrt(jnp.mean(x * x, axis=-1, keepdims=True) + RMS_EPS)
    return y if g is None else y * g


def _sigmoid(x):
    return 1.0 / (1.0 + jnp.exp(-x))


def _dot(a, b):
    return jnp.dot(a, b, preferred_element_type=F32)


def _dot_nt(a, b):
    return lax.dot_general(a, b, (((1,), (1,)), ((), ())), preferred_element_type=F32)


def _dot_tn(a, b):
    return lax.dot_general(a, b, (((0,), (0,)), ((), ())), preferred_element_type=F32)


def _rope_half_vreg(z, cos_t, sin_lo, sin_hi, quarter):
    return (z * cos_t + pltpu.roll(z, LANES - quarter, 1) * sin_lo + pltpu.roll(z, quarter, 1) * sin_hi)


def _in_proj_kernel(x_ref, g_ref, wrq, wrk, wrv, wrg, wcq, wckv, wkpe, wxq, gq_ref, gkv_ref, wuqn, wuqp,
                    cosr, sinr, cosp, sinlo, sinhi, *rest,
                    ret_heads, ret_dk, heads, d_nope, d_rope, kv_lora, absorbed, q_scale):
    if absorbed:
        wa, (rq_o, rk_o, rv_o, rg_o, qcat_o, xq_o, ckv_o, kpe_o, a_o) = rest[0], rest[1:]
    else:
        wa, wb, (rq_o, rk_o, rv_o, rg_o, qcat_o, xq_o, ckv_o, kpe_o, a_o, b_o) = rest[0], rest[1], rest[2:]
    u = _rms(x_ref[...], g_ref[...]).astype(BF16)
    cr, sr = cosr[...], sinr[...]
    cp, slo, shi = cosp[...], sinlo[...], sinhi[...]
    group = d_nope + LANES
    k_scale = ret_dk ** -0.5

    cq = _dot_nt(u, wcq[...])
    ckv = _dot_nt(u, wckv[...])
    kpe_raw = _dot_nt(u, wkpe[...])
    zq = _dot_nt(u, wrq[...])
    zk = _dot_nt(u, wrk[...])

    cqn = _rms(cq, gq_ref[...]).astype(BF16)
    ckvn = _rms(ckv, gkv_ref[...])
    ckv_o[...] = ckvn
    kpe = _rope_half_vreg(kpe_raw, cp, slo, shi, d_rope // 2)
    kpe_o[...] = kpe[:, :d_rope]

    qn = _dot(cqn, wuqn[...])
    qp = _dot(cqn, wuqp[...])
    if not absorbed:
        ckvb = ckvn.astype(BF16)
        kn = _dot(ckvb, wa[...])
        vn = _dot(ckvb, wb[...])

    for h in range(ret_heads):
        sl = slice(h * ret_dk, (h + 1) * ret_dk)
        q_h = zq[:, sl]
        k_h = zk[:, sl]
        rq_o[:, sl] = (q_h * cr + pltpu.roll(q_h, ret_dk // 2, 1) * sr).astype(BF16)
        rk_o[:, sl] = ((k_h * cr + pltpu.roll(k_h, ret_dk // 2, 1) * sr) * k_scale).astype(BF16)

    rv = _dot_nt(u, wrv[...])
    rg = _dot_nt(u, wrg[...])
    xq = _dot_nt(u, wxq[...])

    for h in range(heads):
        qn_h = qn[:, h * d_nope:(h + 1) * d_nope]
        qp_h = _rope_half_vreg(qp[:, h * LANES:(h + 1) * LANES], cp, slo, shi, d_rope // 2)
        if absorbed:
            a_o[:, h * kv_lora:(h + 1) * kv_lora] = _dot(qn_h.astype(BF16), wa[h]).astype(BF16)
        else:
            qn_h = qn_h * q_scale
            qp_h = qp_h * q_scale
        qcat_o[:, h * group:h * group + d_nope] = qn_h.astype(BF16)
        qcat_o[:, h * group + d_nope:(h + 1) * group] = qp_h.astype(BF16)
    if not absorbed:
        kpeb = kpe.astype(BF16)
        for h in range(heads):
            a_o[:, h * group:h * group + d_nope] = kn[:, h * d_nope:(h + 1) * d_nope].astype(BF16)
            a_o[:, h * group + d_nope:(h + 1) * group] = kpeb
        b_o[...] = vn.astype(BF16)

    rv_o[...] = rv.astype(BF16)
    rg_o[...] = rg.astype(BF16)
    xq_o[...] = xq.astype(BF16)


def _in_proj(x, w, tabs, dims, *, absorbed, tm):
    n, d_model = x.shape
    heads, d_nope, d_rope, kv_lora = dims["heads"], dims["d_nope"], dims["d_rope"], dims["kv_lora"]
    ret_heads, ret_dk, ret_dv = dims["ret_heads"], dims["ret_dk"], dims["ret_dv"]
    group = d_nope + LANES
    tab_rows = tabs["cosr"].shape[0]
    tab_tiles = tab_rows // tm
    row = lambda c: pl.BlockSpec((tm, c), lambda i: (i, 0))
    tab = lambda: pl.BlockSpec((tm, LANES), lambda i: (i % tab_tiles, 0))
    weights = [w["g_pre"], w["w_rq"], w["w_rk"], w["w_rv"], w["w_rg"], w["w_cq"], w["w_ckv"], w["w_kpe"],
               w["w_xq"], w["g_q"], w["g_kv"], w["w_uq_n"], w["w_uq_p"]]
    out_cols = [(ret_heads * ret_dk, BF16), (ret_heads * ret_dk, BF16), (ret_heads * ret_dv, BF16),
                (ret_heads * ret_dv, BF16), (heads * group, BF16), (w["w_xq"].shape[0], BF16), (kv_lora, F32),
                (d_rope, F32)]
    if absorbed:
        mode_weights = [w["w_uk_t"]]
        out_cols += [(heads * kv_lora, BF16)]
    else:
        mode_weights = [w["w_kn"], w["w_vn"]]
        out_cols += [(heads * group, BF16), (w["w_vn"].shape[1], BF16)]
    kern = functools.partial(_in_proj_kernel, ret_heads=ret_heads, ret_dk=ret_dk, heads=heads, d_nope=d_nope,
                             d_rope=d_rope, kv_lora=kv_lora, absorbed=absorbed,
                             q_scale=(d_nope + d_rope) ** -0.5 * math.log2(math.e))
    return pl.pallas_call(
        kern,
        grid=(n // tm,),
        in_specs=([row(d_model)] + [_resident(a.shape) for a in weights] + [tab() for _ in range(5)]
                  + [_resident(a.shape) for a in mode_weights]),
        out_specs=[row(c) for c, _ in out_cols],
        out_shape=[jax.ShapeDtypeStruct((n, c), dt) for c, dt in out_cols],
        compiler_params=_params(1),
        name="in_proj_sample" if absorbed else "in_proj_prompt",
    )(x, *weights, tabs["cosr"], tabs["sinr"], tabs["cosp"], tabs["sinlo"], tabs["sinhi"], *mode_weights)


def _ret_head(q, k, v, g, s, dec, qd, kd, g_l):
    inner = _dot_nt(q, k) * dec
    o = _dot(inner.astype(BF16), v) + _dot((q.astype(F32) * qd).astype(BF16), s.astype(BF16))
    s_new = s * g_l + _dot_tn((k.astype(F32) * kd).astype(BF16), v)
    gf = g.astype(F32)
    o = (gf * _sigmoid(gf)) * _rms(o)
    return o.astype(BF16), s_new


def _ret_prompt_kernel(rq, rk, rv, rg, dec, qd, kd, o_ref, s_out, s_scr, *, heads, dk, dv, g_l):
    c = pl.program_id(1)

    @pl.when(c == 0)
    def _():
        s_scr[...] = jnp.zeros(s_scr.shape, F32)

    for h in range(heads):
        o, s_new = _ret_head(rq[:, h * dk:(h + 1) * dk], rk[:, h * dk:(h + 1) * dk],
                             rv[:, h * dv:(h + 1) * dv], rg[:, h * dv:(h + 1) * dv],
                             s_scr[h], dec[h], qd[h], kd[h], g_l[h])
        o_ref[:, h * dv:(h + 1) * dv] = o
        s_scr[h] = s_new

    @pl.when(c == pl.num_programs(1) - 1)
    def _():
        s_out[0] = s_scr[...]


def _ret_consts(heads, length, dk, chunk_rows=None):
    rows = length if chunk_rows is None else chunk_rows
    lg = np.log1p(-np.exp2(-5.0 - np.arange(heads, dtype=np.float64)))
    i = np.arange(rows, dtype=np.float64)
    diff = i[:, None] - i[None, :]
    valid = (diff >= 0) & (i[:, None] < length) & (i[None, :] < length)
    dec = np.where(valid[None], np.exp(np.maximum(diff, 0.0)[None] * lg[:, None, None]), 0.0)
    qd = np.exp((i[None, :] + 1.0) * lg[:, None])
    kd = np.where(i[None, :] < length, np.exp((length - 1.0 - i[None, :]) * lg[:, None]), 0.0)
    qd = np.broadcast_to(qd[:, :, None], (heads, rows, dk))
    kd = np.broadcast_to(kd[:, :, None], (heads, rows, dk))
    g_l = tuple(float(v) for v in np.exp(length * lg))
    return (jnp.asarray(dec, F32), jnp.asarray(qd, F32), jnp.asarray(kd, F32), g_l)


def _ret_prompt(rq, rk, rv, rg, batch, seq, dims):
    heads, dk, dv = dims["ret_heads"], dims["ret_dk"], dims["ret_dv"]
    L = RET_CHUNK
    nc = seq // L
    dec, qd, kd, g_l = _ret_consts(heads, L, dk)
    row = lambda c: pl.BlockSpec((L, c), lambda b, i: (b * nc + i, 0))
    kern = functools.partial(_ret_prompt_kernel, heads=heads, dk=dk, dv=dv, g_l=g_l)
    return pl.pallas_call(
        kern,
        grid=(batch, nc),
        in_specs=[row(heads * dk), row(heads * dk), row(heads * dv), row(heads * dv),
                  _resident(dec.shape), _resident(qd.shape), _resident(kd.shape)],
        out_specs=[row(heads * dv), pl.BlockSpec((1, heads, dk, dv), lambda b, i: (b, 0, 0, 0))],
        out_shape=[jax.ShapeDtypeStruct((batch * seq, heads * dv), BF16),
                   jax.ShapeDtypeStruct((batch, heads, dk, dv), F32)],
        scratch_shapes=[pltpu.VMEM((heads, dk, dv), F32)],
        compiler_params=_params(2),
        name="retention_prompt",
    )(rq, rk, rv, rg, dec, qd, kd)


def _ret_sample_kernel(rq, rk, rv, rg, s0, dec, qd, kd, o_ref, s_out, *, group, heads, dk, dv, g_l):
    pairs = [(i, h) for i in range(group) for h in range(heads)]

    def state_side(i, h):
        q = rq[i, :, h * dk:(h + 1) * dk]
        k = rk[i, :, h * dk:(h + 1) * dk]
        v = rv[i, :, h * dv:(h + 1) * dv]
        s = s0[i, h]
        read = _dot((q.astype(F32) * qd[h]).astype(BF16), s.astype(BF16))
        s_out[i, h] = s * g_l[h] + _dot_tn((k.astype(F32) * kd[h]).astype(BF16), v)
        return read

    def output_side(i, h, read):
        q = rq[i, :, h * dk:(h + 1) * dk]
        k = rk[i, :, h * dk:(h + 1) * dk]
        v = rv[i, :, h * dv:(h + 1) * dv]
        o = _dot((_dot_nt(q, k) * dec[h]).astype(BF16), v) + read
        gf = rg[i, :, h * dv:(h + 1) * dv].astype(F32)
        o_ref[i, :, h * dv:(h + 1) * dv] = ((gf * _sigmoid(gf)) * _rms(o)).astype(BF16)

    read = state_side(*pairs[0])
    for n, (i, h) in enumerate(pairs):
        nxt = state_side(*pairs[n + 1]) if n + 1 < len(pairs) else None
        output_side(i, h, read)
        read = nxt


def _ret_sample(rq, rk, rv, rg, state, tokens, dims):
    heads, dk, dv = dims["ret_heads"], dims["ret_dk"], dims["ret_dv"]
    b, tp, _ = rq.shape
    G = RET_SAMPLE_GROUP
    dec, qd, kd, g_l = _ret_consts(heads, tokens, dk, chunk_rows=tp)
    blk = lambda c: pl.BlockSpec((G, tp, c), lambda i: (i, 0, 0))
    st = pl.BlockSpec((G, heads, dk, dv), lambda i: (i, 0, 0, 0))
    kern = functools.partial(_ret_sample_kernel, group=G, heads=heads, dk=dk, dv=dv, g_l=g_l)
    return pl.pallas_call(
        kern,
        grid=(b // G,),
        in_specs=[blk(heads * dk), blk(heads * dk), blk(heads * dv), blk(heads * dv), st,
                  _resident(dec.shape), _resident(qd.shape), _resident(kd.shape)],
        out_specs=[blk(heads * dv), st],
        out_shape=[jax.ShapeDtypeStruct((b, tp, heads * dv), BF16),
                   jax.ShapeDtypeStruct((b, heads, dk, dv), F32)],
        compiler_params=_params(1),
        name="retention_sample",
    )(rq, rk, rv, rg, state, dec, qd, kd)


def _mla_prompt_kernel(q_ref, k_ref, v_ref, o_ref, m_scr, l_scr, acc_scr, *, tq, tk, hp, group, d_v):
    qi = pl.program_id(2)
    ri = lax.broadcasted_iota(jnp.int32, (tq, tk), 0)
    ci = lax.broadcasted_iota(jnp.int32, (tq, tk), 1)
    causal_bias = jnp.where(ci <= ri, 0.0, NEG_BIG)

    def block(j, carry, first):
        start = pl.multiple_of(j * tk, tk)

        def scores(h):
            return _dot_nt(q_ref[:, h * group:(h + 1) * group], k_ref[pl.ds(start, tk), h * group:(h + 1) * group])

        s_next = scores(0)
        for h in range(hp):
            s = s_next
            if h + 1 < hp:
                s_next = scores(h + 1)
            vals = v_ref[pl.ds(start, tk), h * d_v:(h + 1) * d_v]
            if first:
                s = s + causal_bias
                m_new = jnp.broadcast_to(jnp.max(s, axis=-1, keepdims=True), (tq, LANES))
                p = jnp.exp2(s - jnp.tile(m_new, (1, tk // LANES)))
                l_scr[h] = jnp.broadcast_to(jnp.sum(p, axis=-1, keepdims=True), (tq, LANES))
                acc_scr[h] = _dot(p.astype(BF16), vals)
            else:
                m_prev = m_scr[h]
                m_new = jnp.maximum(m_prev, jnp.max(s, axis=-1, keepdims=True))
                a = jnp.exp2(m_prev - m_new)
                p = jnp.exp2(s - jnp.tile(m_new, (1, tk // LANES)))
                l_scr[h] = a * l_scr[h] + jnp.sum(p, axis=-1, keepdims=True)
                acc_scr[h] = a * acc_scr[h] + _dot(p.astype(BF16), vals)
            m_scr[h] = m_new
        return carry

    block(qi, 0, True)
    lax.fori_loop(0, qi, functools.partial(block, first=False), 0)
    for h in range(hp):
        o_ref[:, h * d_v:(h + 1) * d_v] = (acc_scr[h] / l_scr[h]).astype(BF16)


def _mla_prompt(qcat, kcat, v, batch, seq, dims):
    heads, d_nope, d_v = dims["heads"], dims["d_nope"], dims["d_v"]
    group = d_nope + LANES
    t = ATTN_Q_TILE
    hp = ATTN_HEADS_PER_STEP
    nq = seq // t
    assert ATTN_KV_TILE == t and d_v == LANES
    kern = functools.partial(_mla_prompt_kernel, tq=t, tk=ATTN_KV_TILE, hp=hp, group=group, d_v=d_v)
    return pl.pallas_call(
        kern,
        grid=(batch, heads // hp, nq),
        in_specs=[pl.BlockSpec((t, hp * group), lambda b, h, i: (b * nq + i, h)),
                  pl.BlockSpec((seq, hp * group), lambda b, h, i: (b, h)),
                  pl.BlockSpec((seq, hp * d_v), lambda b, h, i: (b, h))],
        out_specs=pl.BlockSpec((t, hp * d_v), lambda b, h, i: (b * nq + i, h)),
        out_shape=jax.ShapeDtypeStruct((batch * seq, heads * d_v), BF16),
        scratch_shapes=[pltpu.VMEM((hp, t, LANES), F32), pltpu.VMEM((hp, t, LANES), F32),
                        pltpu.VMEM((hp, t, d_v), F32)],
        compiler_params=_params(3),
        name="mla_prompt",
    )(qcat, kcat, v)


def _mla_sample_kernel(pt_ref, ql_ref, qp_ref, cn_ref, kn_ref, ckv_hbm, kpe_hbm, o_ref,
                       ckv_buf, kpe_buf, kbf, s_scr, sem,
                       *, layer, n_pages, page, chunk, tokens, heads, scale):
    b = pl.program_id(0)
    past = n_pages * page
    d_rope = qp_ref.shape[2]

    def page_copy(seq, slot, p, array):
        pg = pt_ref[seq, p]
        if array == 0:
            off = pl.multiple_of(p * page, page)
            return pltpu.make_async_copy(ckv_hbm.at[layer, pg], ckv_buf.at[slot, pl.ds(off, page), :],
                                         sem.at[0, slot])
        off = pl.multiple_of(p * d_rope, d_rope)
        return pltpu.make_async_copy(kpe_hbm.at[layer, pg], kpe_buf.at[slot, pl.ds(off, d_rope), :],
                                     sem.at[1, slot])

    def for_each_page(seq, slot, fn, arrays=(0, 1), unroll=DMA_LOOP_UNROLL):
        def body(p, carry):
            for a in arrays:
                fn(page_copy(seq, slot, p, a))
            return carry
        lax.fori_loop(0, n_pages, body, 0, unroll=unroll)

    @pl.when(b == 0)
    def _():
        for_each_page(0, 0, lambda cp: cp.start())

    @pl.when(b + 1 < pl.num_programs(0))
    def _():
        for_each_page(b + 1, (b + 1) % 2, lambda cp: cp.start())

    slot = b % 2
    for_each_page(b, slot, lambda cp: cp.wait(), arrays=(0,), unroll=True)
    for_each_page(b, slot, lambda cp: cp.wait(), arrays=(1,), unroll=True)

    ql = ql_ref[0]
    qp = qp_ref[0]
    ckv_s = ckv_buf.at[slot]
    kpe_s = kpe_buf.at[slot]
    n_chunks = past // chunk

    cn = cn_ref[0].astype(BF16)
    kn = kn_ref[0].astype(BF16)
    s_new = (_dot_nt(ql, cn) + _dot_nt(qp, kn)) * scale
    t = lax.broadcasted_iota(jnp.int32, s_new.shape, 0) // heads
    col = lax.broadcasted_iota(jnp.int32, s_new.shape, 1)
    s_new = jnp.where((col <= t) & (col < tokens), s_new, NEG_BIG)
    m = jnp.max(s_new, axis=-1, keepdims=True)
    p_new = jnp.exp(s_new - m)
    l = jnp.sum(p_new, axis=-1, keepdims=True)
    acc = _dot(p_new.astype(BF16), cn)

    def scores(c):
        sl = slice(c * chunk, (c + 1) * chunk)
        kc = ckv_s[sl, :].astype(BF16)
        kbf[sl, :] = kc
        pe = jnp.concatenate([kpe_s[p * d_rope:(p + 1) * d_rope, :]
                              for p in range(c * chunk // page, (c + 1) * chunk // page)], axis=1)
        s_scr[:, sl] = (_dot_nt(ql, kc) + _dot(qp, pe.astype(BF16))) * scale

    def values(c, m, l, acc):
        sl = slice(c * chunk, (c + 1) * chunk)
        s = s_scr[:, sl]
        m_new = jnp.maximum(m, jnp.max(s, axis=-1, keepdims=True))
        a = jnp.exp(m - m_new)
        p = jnp.exp(s - m_new)
        l = a * l + jnp.sum(p, axis=-1, keepdims=True)
        return m_new, l, a * acc + _dot(p.astype(BF16), kbf[sl, :])

    scores(0)
    for c in range(n_chunks):
        if c + 1 < n_chunks:
            scores(c + 1)
        m, l, acc = values(c, m, l, acc)
    o_ref[0] = acc / l


def _mla_sample(ql, qp, ckv_new, kpe_new, cache_ckv, cache_kpe_t, layer, page_table, tokens, dims):
    heads, d_nope, d_rope, kv_lora = dims["heads"], dims["d_nope"], dims["d_rope"], dims["kv_lora"]
    b, rows, _ = ql.shape
    n_pages = page_table.shape[1]
    page = cache_ckv.shape[2]
    past = n_pages * page
    tp = ckv_new.shape[1]
    per_b = lambda r, c: pl.BlockSpec((1, r, c), lambda i, pt: (i, 0, 0))
    hbm = pl.BlockSpec(memory_space=pl.ANY)
    kern = functools.partial(_mla_sample_kernel, layer=layer, n_pages=n_pages, page=page, chunk=DECODE_CHUNK,
                             tokens=tokens, heads=heads, scale=(d_nope + d_rope) ** -0.5)
    grid_spec = pltpu.PrefetchScalarGridSpec(
        num_scalar_prefetch=1,
        grid=(b,),
        in_specs=[per_b(rows, kv_lora), per_b(rows, d_rope), per_b(tp, kv_lora), per_b(tp, d_rope), hbm, hbm],
        out_specs=per_b(rows, kv_lora),
        scratch_shapes=[pltpu.VMEM((2, past, kv_lora), F32), pltpu.VMEM((2, n_pages * d_rope, page), F32),
                        pltpu.VMEM((past, kv_lora), BF16), pltpu.VMEM((rows, past), F32),
                        pltpu.SemaphoreType.DMA((2, 2))],
    )
    return pl.pallas_call(
        kern,
        grid_spec=grid_spec,
        out_shape=jax.ShapeDtypeStruct((b, rows, kv_lora), F32),
        compiler_params=_params(1),
        name="mla_sample",
    )(page_table, ql, qp, ckv_new, kpe_new, cache_ckv, cache_kpe_t)


def _mem_kv_kernel(m_ref, g_ref, wk, wv, k_o, v_o):
    mn = _rms(m_ref[0], g_ref[...]).astype(BF16)
    k_o[0] = _dot(mn, wk[...])
    v_o[0] = _dot(mn, wv[...])


def _mem_kv(mem, g, wk, wv):
    b, m, d = mem.shape
    c = wk.shape[1]
    return pl.pallas_call(
        _mem_kv_kernel,
        grid=(b,),
        in_specs=[pl.BlockSpec((1, m, d), lambda i: (i, 0, 0)), _resident(g.shape), _resident(wk.shape),
                  _resident(wv.shape)],
        out_specs=[pl.BlockSpec((1, m, c), lambda i: (i, 0, 0))] * 2,
        out_shape=[jax.ShapeDtypeStruct((b, m, c), F32)] * 2,
        compiler_params=_params(1),
        name="mem_kv",
    )(mem, g, wk, wv)


def _x_attend_heads(q, mk, mv, heads, hd):
    r = q.shape[0]
    lane = lax.broadcasted_iota(jnp.int32, q.shape, 1)
    sels = [(lane >= h * hd) & (lane < (h + 1) * hd) for h in range(heads)]
    q_heads = jnp.concatenate([jnp.where(sel, q, jnp.zeros_like(q)) for sel in sels], axis=0)
    s = _dot_nt(q_heads, mk) * (hd ** -0.5)
    p = jnp.exp(s - jnp.max(s, axis=-1, keepdims=True))
    p = p / jnp.sum(p, axis=-1, keepdims=True)
    pv = _dot(p.astype(BF16), mv)
    out = jnp.zeros(q.shape, F32)
    for h, sel in enumerate(sels):
        out = out + jnp.where(sel, pv[h * r:(h + 1) * r], 0.0)
    return out


def _x_sample_kernel(q_ref, kt_ref, vt_ref, o_ref, *, group, heads, hd):
    tp = q_ref.shape[1]
    lane = lax.broadcasted_iota(jnp.int32, q_ref.shape[1:], 1)
    sels = [(lane >= h * hd) & (lane < (h + 1) * hd) for h in range(heads)]
    def scores(g):
        q_heads = jnp.concatenate([jnp.where(sel, q_ref[g], 0.0) for sel in sels], axis=0).astype(BF16)
        return _dot(q_heads, kt_ref[g].astype(BF16)) * (hd ** -0.5)

    s_next = scores(0)
    for g in range(group):
        q = q_ref[g]
        s = s_next
        if g + 1 < group:
            s_next = scores(g + 1)
        p = jnp.exp(s - jnp.max(s, axis=-1, keepdims=True))
        p = p / jnp.sum(p, axis=-1, keepdims=True)
        r = _dot_nt(p.astype(BF16), vt_ref[g].astype(BF16))
        out = jnp.zeros(q.shape, F32)
        for h, sel in enumerate(sels):
            out = out + jnp.where(sel, r[h * tp:(h + 1) * tp], 0.0)
        o_ref[g] = out


def _x_sample(xq, mkt, mvt, heads):
    b, tp, c = xq.shape
    m = mkt.shape[2]
    G = X_SAMPLE_GROUP
    kern = functools.partial(_x_sample_kernel, group=G, heads=heads, hd=c // heads)
    return pl.pallas_call(
        kern,
        grid=(b // G,),
        in_specs=[pl.BlockSpec((G, tp, c), lambda i: (i, 0, 0)),
                  pl.BlockSpec((G, c, m), lambda i: (i, 0, 0)),
                  pl.BlockSpec((G, c, m), lambda i: (i, 0, 0))],
        out_specs=pl.BlockSpec((G, tp, c), lambda i: (i, 0, 0)),
        out_shape=jax.ShapeDtypeStruct((b, tp, c), F32),
        compiler_params=_params(1),
        name="x_attend_sample",
    )(xq, mkt, mvt)


def _merge_kernel(x_ref, gpre, wgate, oret, wret, omla, wmla, wuv, ox, wx, wout, gpost, *rest,
                  heads, kv_lora, d_v, absorbed, x_heads):
    h_o = rest[-1]
    if x_heads:
        mk, mv = rest[:2]
        o_x = _x_attend_heads(ox[...], mk[0].astype(BF16), mv[0].astype(BF16), x_heads, ox.shape[1] // x_heads)
    else:
        o_x = ox[...]
    x = x_ref[...]
    d = x.shape[1]
    u = _rms(x, gpre[...]).astype(BF16)
    a_ret = _dot(oret[...], wret[...])
    if absorbed:
        a_mla = jnp.zeros(x.shape, F32)
        for h in range(heads):
            o_h = _dot(omla[:, h * kv_lora:(h + 1) * kv_lora].astype(BF16), wuv[h]).astype(BF16)
            a_mla = a_mla + _dot(o_h, wmla[h * d_v:(h + 1) * d_v, :])
    else:
        a_mla = _dot(omla[...], wmla[...])
    a_x = _dot(o_x.astype(BF16), wx[...])
    mixed = jnp.zeros(x.shape, F32)
    for i, a in enumerate((a_ret, a_mla, a_x)):
        mixed = mixed + _sigmoid(_dot_nt(u, wgate[i * d:(i + 1) * d, :])) * a
    h_o[...] = x + _rms(_dot(mixed.astype(BF16), wout[...]), gpost[...])


def _merge(x, oret, omla, ox, w, dims, *, absorbed, tm, mem=None):
    n, d = x.shape
    row = lambda a: pl.BlockSpec((tm, a.shape[1]), lambda i: (i, 0))
    res = lambda a: _resident(a.shape)
    args = [x, w["g_pre"], w["w_gates"], oret, w["w_ret_o"], omla, w["w_mla_o"], w["w_uv_h"], ox, w["w_x_o"],
            w["w_out"], w["g_post"]]
    specs = [row(x), res(w["g_pre"]), res(w["w_gates"]), row(oret), res(w["w_ret_o"]), row(omla),
             res(w["w_mla_o"]), res(w["w_uv_h"]), row(ox), res(w["w_x_o"]), res(w["w_out"]), res(w["g_post"])]
    x_heads = 0
    if mem is not None:
        mk, mv, x_heads, seq_rows = mem
        tiles_per_seq = seq_rows // tm
        per_seq = pl.BlockSpec((1,) + mk.shape[1:], lambda i: (i // tiles_per_seq, 0, 0))
        args += [mk, mv]
        specs += [per_seq, per_seq]
    kern = functools.partial(_merge_kernel, heads=dims["heads"], kv_lora=dims["kv_lora"], d_v=dims["d_v"],
                             absorbed=absorbed, x_heads=x_heads)
    return pl.pallas_call(
        kern,
        grid=(n // tm,),
        in_specs=specs,
        out_specs=pl.BlockSpec((tm, d), lambda i: (i, 0)),
        out_shape=jax.ShapeDtypeStruct((n, d), F32),
        compiler_params=_params(1),
        name="merge_sample" if absorbed else "merge_prompt",
    )(*args)


def _ffn_kernel(h_ref, gpre, wg, wu, wd, gpost, y_o):
    h = h_ref[...]
    f = _rms(h, gpre[...]).astype(BF16)
    gate = _dot(f, wg[...])
    act = (gate * _sigmoid(gate) * _dot(f, wu[...])).astype(BF16)
    y_o[...] = h + _rms(_dot(act, wd[...]), gpost[...])


def _ffn(h, w, *, tm):
    n, d = h.shape
    res = lambda a: _resident(a.shape)
    return pl.pallas_call(
        _ffn_kernel,
        grid=(n // tm,),
        in_specs=[pl.BlockSpec((tm, d), lambda i: (i, 0)), res(w["g_ffn_pre"]), res(w["w_ffn_gate"]),
                  res(w["w_ffn_up"]), res(w["w_ffn_down"]), res(w["g_ffn_post"])],
        out_specs=pl.BlockSpec((tm, d), lambda i: (i, 0)),
        out_shape=jax.ShapeDtypeStruct((n, d), F32),
        compiler_params=_params(1),
        name="ffn",
    )(h, w["g_ffn_pre"], w["w_ffn_gate"], w["w_ffn_up"], w["w_ffn_down"], w["g_ffn_post"])


def _rope_tables(pos, ret_dk, d_rope):
    posf = pos.astype(F32)[:, None]

    def angles(half):
        inv = ROPE_BASE ** (-jnp.arange(half, dtype=F32) / half)
        ang = posf * inv[None, :]
        return jnp.cos(ang), jnp.sin(ang)

    cr, sr = angles(ret_dk // 2)
    cp, sp = angles(d_rope // 2)
    z = jnp.zeros_like(cp)
    pad = jnp.zeros((pos.shape[0], LANES - d_rope), F32)
    return {
        "cosr": jnp.concatenate([cr, cr], axis=1),
        "sinr": jnp.concatenate([-sr, sr], axis=1),
        "cosp": jnp.concatenate([cp, cp, pad], axis=1),
        "sinlo": jnp.concatenate([-sp, z, pad], axis=1),
        "sinhi": jnp.concatenate([z, sp, pad], axis=1),
    }


def _layer_weights(l, dims, sizes, norm_mix_pre, norm_mix_post, norm_ffn_pre, norm_ffn_post, norm_mem, norm_q_lat,
                   norm_kv_lat, w_in, w_uq, w_uk, w_uv, w_mem_k, w_mem_v, w_ret_o, w_mla_o, w_x_o, w_out,
                   w_ffn_gate, w_ffn_up, w_ffn_down):
    heads, d_nope, d_rope, kv_lora, d_v = (dims[k] for k in ("heads", "d_nope", "d_rope", "kv_lora", "d_v"))
    bf = lambda a: a.astype(BF16)
    gain = lambda a: a[l].astype(F32)[None, :]
    offs = np.concatenate([[0], np.cumsum(sizes)])
    w_in_t = bf(jnp.swapaxes(w_in[l], 0, 1))
    seg = [w_in_t[offs[i]:offs[i + 1], :] for i in range(len(sizes))]
    q_lora = w_uq.shape[1]
    uq = w_uq[l].reshape(q_lora, heads, d_nope + d_rope)
    uq_p = jnp.pad(uq[:, :, d_nope:], ((0, 0), (0, 0), (0, LANES - d_rope)))
    return {
        "g_pre": gain(norm_mix_pre), "g_post": gain(norm_mix_post), "g_ffn_pre": gain(norm_ffn_pre),
        "g_ffn_post": gain(norm_ffn_post), "g_mem": gain(norm_mem), "g_q": gain(norm_q_lat),
        "g_kv": gain(norm_kv_lat),
        "w_rq": seg[0], "w_rk": seg[1], "w_rv": seg[2], "w_rg": seg[3], "w_cq": seg[4],
        "w_ckv": seg[5], "w_kpe": jnp.pad(seg[6], ((0, LANES - d_rope), (0, 0))), "w_xq": seg[7],
        "w_gates": seg[8],
        "w_uq_n": bf(uq[:, :, :d_nope].reshape(q_lora, heads * d_nope)),
        "w_uq_p": bf(uq_p.reshape(q_lora, heads * LANES)),
        "w_uk_t": bf(jnp.swapaxes(w_uk[l], 1, 2)),
        "w_kn": bf(jnp.swapaxes(w_uk[l], 0, 1).reshape(kv_lora, heads * d_nope)),
        "w_vn": bf(jnp.swapaxes(w_uv[l], 0, 1).reshape(kv_lora, heads * d_v)),
        "w_uv_h": bf(w_uv[l]),
        "w_mem_k": bf(w_mem_k[l]), "w_mem_v": bf(w_mem_v[l]),
        "w_ret_o": bf(w_ret_o[l]), "w_mla_o": bf(w_mla_o[l]), "w_x_o": bf(w_x_o[l]), "w_out": bf(w_out[l]),
        "w_ffn_gate": bf(w_ffn_gate[l]), "w_ffn_up": bf(w_ffn_up[l]), "w_ffn_down": bf(w_ffn_down[l]),
    }


def _pad_tokens(a, b, tokens, rows=SAMPLE_TOK_PAD):
    a = a.reshape(b, tokens, a.shape[-1])
    return jnp.pad(a, ((0, 0), (0, rows - tokens), (0, 0)))


def kernel(x_prompt, x_sample, mem_prompt, cache_ckv, cache_kpe, page_table, state_ret, cache_mem_k, cache_mem_v,
           norm_mix_pre, norm_mix_post, norm_ffn_pre, norm_ffn_post, norm_mem, norm_q_lat, norm_kv_lat, w_in,
           w_uq, w_uk, w_uv, w_mem_k, w_mem_v, w_ret_o, w_mla_o, w_x_o, w_out, w_ffn_gate, w_ffn_up, w_ffn_down):
    depth = w_in.shape[0]
    batch, seq, d_model = x_prompt.shape
    db, tokens, _ = x_sample.shape
    ret_heads, ret_dk, ret_dv = state_ret.shape[2:]
    heads, kv_lora, d_nope = w_uk.shape[1:]
    d_rope = cache_kpe.shape[-1]
    d_v = w_uv.shape[-1]
    n_mem, x_heads, x_hd = cache_mem_k.shape[2:]
    q_lora = w_uq.shape[1]
    past_len = page_table.shape[1] * cache_ckv.shape[2]
    dims = dict(heads=heads, d_nope=d_nope, d_rope=d_rope, kv_lora=kv_lora, d_v=d_v,
                ret_heads=ret_heads, ret_dk=ret_dk, ret_dv=ret_dv)
    assert ret_dk == LANES and d_nope == LANES and d_rope <= LANES // 2 and tokens <= SAMPLE_TOK_PAD
    sizes = (ret_heads * ret_dk, ret_heads * ret_dk, ret_heads * ret_dv, ret_heads * ret_dv, q_lora, kv_lora,
             d_rope, x_heads * x_hd, w_in.shape[2] - (2 * ret_heads * ret_dk + 2 * ret_heads * ret_dv + q_lora
                                                       + kv_lora + d_rope + x_heads * x_hd))

    cache_kpe_t = jnp.swapaxes(cache_kpe, 2, 3)

    tabs_p = _rope_tables(jnp.arange(seq), ret_dk, d_rope)
    n_s = db * tokens
    tabs_s = _rope_tables(past_len + (jnp.arange(n_s) % tokens), ret_dk, d_rope)

    y_p = x_prompt.reshape(batch * seq, d_model)
    y_s = x_sample.reshape(n_s, d_model)
    outs = [[] for _ in range(8)]
    for l in range(depth):
        w = _layer_weights(l, dims, sizes, norm_mix_pre, norm_mix_post, norm_ffn_pre, norm_ffn_post, norm_mem,
                           norm_q_lat, norm_kv_lat, w_in, w_uq, w_uk, w_uv, w_mem_k, w_mem_v, w_ret_o, w_mla_o,
                           w_x_o, w_out, w_ffn_gate, w_ffn_up, w_ffn_down)

        mk_p, mv_p = _mem_kv(mem_prompt, w["g_mem"], w["w_mem_k"], w["w_mem_v"])
        rq, rk, rv, rg, qcat, xq, ckv_p, kpe_p, kcat, vn = _in_proj(y_p, w, tabs_p, dims, absorbed=False,
                                                                    tm=ROW_TILE)
        o_ret, ret_p = _ret_prompt(rq, rk, rv, rg, batch, seq, dims)
        o_mla = _mla_prompt(qcat, kcat, vn, batch, seq, dims)
        h_p = _merge(y_p, o_ret, o_mla, xq, w, dims, absorbed=False, tm=ROW_TILE, mem=(mk_p, mv_p, x_heads, seq))
        y_p = _ffn(h_p, w, tm=ROW_TILE)

        rq, rk, rv, rg, qcat, xq, ckv_s, kpe_s, qlat = _in_proj(y_s, w, tabs_s, dims, absorbed=True, tm=ROW_TILE)
        pad = lambda a: _pad_tokens(a, db, tokens)
        o_ret, ret_s = _ret_sample(pad(rq), pad(rk), pad(rv), pad(rg), state_ret[l].astype(F32), tokens, dims)
        o_ret = o_ret[:, :tokens].reshape(n_s, ret_heads * ret_dv)
        group = d_nope + LANES
        q_pe = qcat.reshape(db, tokens * heads, group)[:, :, d_nope:d_nope + d_rope]
        o_lat = _mla_sample(qlat.reshape(db, tokens * heads, kv_lora), q_pe, pad(ckv_s), pad(kpe_s),
                            cache_ckv, cache_kpe_t, l, page_table, tokens, dims)
        o_lat = o_lat.reshape(n_s, heads * kv_lora)
        mem_t = lambda c: jnp.transpose(c[l], (0, 2, 3, 1)).reshape(db, x_heads * x_hd, n_mem)
        o_x = _x_sample(_pad_tokens(xq.astype(F32), db, tokens, rows=X_SAMPLE_TOK_PAD), mem_t(cache_mem_k),
                        mem_t(cache_mem_v), x_heads)
        o_x = o_x[:, :tokens].reshape(n_s, x_heads * x_hd)
        h_s = _merge(y_s, o_ret, o_lat, o_x, w, dims, absorbed=True, tm=ROW_TILE)
        y_s = _ffn(h_s, w, tm=ROW_TILE)

        for lst, val in zip(outs, (ckv_p.reshape(batch, seq, kv_lora), kpe_p.reshape(batch, seq, d_rope),
                                   ckv_s.reshape(db, tokens, kv_lora), kpe_s.reshape(db, tokens, d_rope),
                                   ret_p.astype(x_prompt.dtype), ret_s.astype(state_ret.dtype),
                                   mk_p.reshape(batch, n_mem, x_heads, x_hd),
                                   mv_p.reshape(batch, n_mem, x_heads, x_hd))):
            lst.append(val)

    return (y_p.reshape(batch, seq, d_model), y_s.reshape(db, tokens, d_model)) + tuple(jnp.stack(o) for o in outs)
```

```python
import functools
import math

import numpy as np
import jax
import jax.numpy as jnp
from jax import lax
from jax.experimental import pallas as pl
from jax.experimental.pallas import tpu as pltpu

F32 = jnp.float32
BF16 = jnp.bfloat16

ROPE_BASE = 10000.0
RMS_EPS = 1e-6
LANES = 128
VMEM_LIMIT = 56 * 1024 * 1024
NEG_BIG = -1e30

ROW_TILE = 512
RET_CHUNK = 256
RET_SEQS_PER_STEP = 4
ATTN_Q_TILE = 512
ATTN_KV_TILE = 512
ATTN_HEADS_PER_STEP = 8
SAMPLE_TOK_PAD = 16
X_SAMPLE_TOK_PAD = 8
RET_SAMPLE_GROUP = 8
X_SAMPLE_GROUP = 16
DECODE_CHUNK = 4096
DMA_LOOP_UNROLL = 4


def _resident(shape):
    nd = len(shape)
    return pl.BlockSpec(shape, lambda *_: (0,) * nd, pipeline_mode=pl.Buffered(1))


def _params(n_axes):
    return pltpu.CompilerParams(dimension_semantics=("arbitrary",) * n_axes, vmem_limit_bytes=VMEM_LIMIT)


def _rms(x, g=None):
    y = x * lax.rsqrt(jnp.mean(x * x, axis=-1, keepdims=True) + RMS_EPS)
    return y if g is None else y * g


def _sigmoid(x):
    return 1.0 / (1.0 + jnp.exp(-x))


def _dot(a, b):
    return jnp.dot(a, b, preferred_element_type=F32)


def _dot_nt(a, b):
    return lax.dot_general(a, b, (((1,), (1,)), ((), ())), preferred_element_type=F32)


def _dot_tn(a, b):
    return lax.dot_general(a, b, (((0,), (0,)), ((), ())), preferred_element_type=F32)


def _rope_half_vreg(z, cos_t, sin_lo, sin_hi, quarter):
    return (z * cos_t + pltpu.roll(z, LANES - quarter, 1) * sin_lo + pltpu.roll(z, quarter, 1) * sin_hi)


def _in_proj_kernel(x_ref, g_ref, wrq, wrk, wrv, wrg, wcq, wckv, wkpe, wxq, gq_ref, gkv_ref, wuqn, wuqp,
                    cosr, sinr, cosp, sinlo, sinhi, *rest,
                    ret_heads, ret_dk, heads, d_nope, d_rope, kv_lora, absorbed, q_scale):
    if absorbed:
        wa, (rq_o, rk_o, rv_o, rg_o, qcat_o, xq_o, ckv_o, kpe_o, a_o) = rest[0], rest[1:]
    else:
        wa, wb, (rq_o, rk_o, rv_o, rg_o, qcat_o, xq_o, ckv_o, kpe_o, a_o, b_o) = rest[0], rest[1], rest[2:]
    u = _rms(x_ref[...], g_ref[...]).astype(BF16)
    cr, sr = cosr[...], sinr[...]
    cp, slo, shi = cosp[...], sinlo[...], sinhi[...]
    group = d_nope + LANES
    k_scale = ret_dk ** -0.5

    cq = _dot_nt(u, wcq[...])
    ckv = _dot_nt(u, wckv[...])
    kpe_raw = _dot_nt(u, wkpe[...])
    zq = _dot_nt(u, wrq[...])
    zk = _dot_nt(u, wrk[...])

    cqn = _rms(cq, gq_ref[...]).astype(BF16)
    ckvn = _rms(ckv, gkv_ref[...])
    ckv_o[...] = ckvn
    kpe = _rope_half_vreg(kpe_raw, cp, slo, shi, d_rope // 2)
    kpe_o[...] = kpe[:, :d_rope]

    qn = _dot(cqn, wuqn[...])
    qp = _dot(cqn, wuqp[...])
    if not absorbed:
        ckvb = ckvn.astype(BF16)
        kn = _dot(ckvb, wa[...])
        vn = _dot(ckvb, wb[...])

    for h in range(ret_heads):
        sl = slice(h * ret_dk, (h + 1) * ret_dk)
        q_h = zq[:, sl]
        k_h = zk[:, sl]
        rq_o[:, sl] = (q_h * cr + pltpu.roll(q_h, ret_dk // 2, 1) * sr).astype(BF16)
        rk_o[:, sl] = ((k_h * cr + pltpu.roll(k_h, ret_dk // 2, 1) * sr) * k_scale).astype(BF16)

    rv = _dot_nt(u, wrv[...])
    rg = _dot_nt(u, wrg[...])
    xq = _dot_nt(u, wxq[...])

    for h in range(heads):
        qn_h = qn[:, h * d_nope:(h + 1) * d_nope]
        qp_h = _rope_half_vreg(qp[:, h * LANES:(h + 1) * LANES], cp, slo, shi, d_rope // 2)
        if absorbed:
            a_o[:, h * kv_lora:(h + 1) * kv_lora] = _dot(qn_h.astype(BF16), wa[h]).astype(BF16)
        else:
            qn_h = qn_h * q_scale
            qp_h = qp_h * q_scale
        qcat_o[:, h * group:h * group + d_nope] = qn_h.astype(BF16)
        qcat_o[:, h * group + d_nope:(h + 1) * group] = qp_h.astype(BF16)
    if not absorbed:
        kpeb = kpe.astype(BF16)
        for h in range(heads):
            a_o[:, h * group:h * group + d_nope] = kn[:, h * d_nope:(h + 1) * d_nope].astype(BF16)
            a_o[:, h * group + d_nope:(h + 1) * group] = kpeb
        b_o[...] = vn.astype(BF16)

    rv_o[...] = rv.astype(BF16)
    rg_o[...] = rg.astype(BF16)
    xq_o[...] = xq.astype(BF16)


def _in_proj(x, w, tabs, dims, *, absorbed, tm):
    n, d_model = x.shape
    heads, d_nope, d_rope, kv_lora = dims["heads"], dims["d_nope"], dims["d_rope"], dims["kv_lora"]
    ret_heads, ret_dk, ret_dv = dims["ret_heads"], dims["ret_dk"], dims["ret_dv"]
    group = d_nope + LANES
    tab_rows = tabs["cosr"].shape[0]
    tab_tiles = tab_rows // tm
    row = lambda c: pl.BlockSpec((tm, c), lambda i: (i, 0))
    tab = lambda: pl.BlockSpec((tm, LANES), lambda i: (i % tab_tiles, 0))
    weights = [w["g_pre"], w["w_rq"], w["w_rk"], w["w_rv"], w["w_rg"], w["w_cq"], w["w_ckv"], w["w_kpe"],
               w["w_xq"], w["g_q"], w["g_kv"], w["w_uq_n"], w["w_uq_p"]]
    out_cols = [(ret_heads * ret_dk, BF16), (ret_heads * ret_dk, BF16), (ret_heads * ret_dv, BF16),
                (ret_heads * ret_dv, BF16), (heads * group, BF16), (w["w_xq"].shape[0], BF16), (kv_lora, F32),
                (d_rope, F32)]
    if absorbed:
        mode_weights = [w["w_uk_t"]]
        out_cols += [(heads * kv_lora, BF16)]
    else:
        mode_weights = [w["w_kn"], w["w_vn"]]
        out_cols += [(heads * group, BF16), (w["w_vn"].shape[1], BF16)]
    kern = functools.partial(_in_proj_kernel, ret_heads=ret_heads, ret_dk=ret_dk, heads=heads, d_nope=d_nope,
                             d_rope=d_rope, kv_lora=kv_lora, absorbed=absorbed,
                             q_scale=(d_nope + d_rope) ** -0.5 * math.log2(math.e))
    return pl.pallas_call(
        kern,
        grid=(n // tm,),
        in_specs=([row(d_model)] + [_resident(a.shape) for a in weights] + [tab() for _ in range(5)]
                  + [_resident(a.shape) for a in mode_weights]),
        out_specs=[row(c) for c, _ in out_cols],
        out_shape=[jax.ShapeDtypeStruct((n, c), dt) for c, dt in out_cols],
        compiler_params=_params(1),
        name="in_proj_sample" if absorbed else "in_proj_prompt",
    )(x, *weights, tabs["cosr"], tabs["sinr"], tabs["cosp"], tabs["sinlo"], tabs["sinhi"], *mode_weights)


def _ret_head(q, k, v, g, s, dec, qd, kd, g_l):
    inner = _dot_nt(q, k) * dec
    o = _dot(inner.astype(BF16), v) + _dot((q.astype(F32) * qd).astype(BF16), s.astype(BF16))
    s_new = s * g_l + _dot_tn((k.astype(F32) * kd).astype(BF16), v)
    gf = g.astype(F32)
    o = (gf * _sigmoid(gf)) * _rms(o)
    return o.astype(BF16), s_new


def _ret_prompt_kernel(rq, rk, rv, rg, dec, qd, kd, o_ref, s_out, s_scr, *, heads, dk, dv, g_l):
    c = pl.program_id(1)

    @pl.when(c == 0)
    def _():
        s_scr[...] = jnp.zeros(s_scr.shape, F32)

    for b in range(rq.shape[0]):
        for h in range(heads):
            o, s_new = _ret_head(rq[b, :, h * dk:(h + 1) * dk], rk[b, :, h * dk:(h + 1) * dk],
                                 rv[b, :, h * dv:(h + 1) * dv], rg[b, :, h * dv:(h + 1) * dv],
                                 s_scr[b, h], dec[h], qd[h], kd[h], g_l[h])
            o_ref[b, :, h * dv:(h + 1) * dv] = o
            s_scr[b, h] = s_new

    @pl.when(c == pl.num_programs(1) - 1)
    def _():
        s_out[...] = s_scr[...]


def _ret_consts(heads, length, dk, chunk_rows=None):
    rows = length if chunk_rows is None else chunk_rows
    lg = np.log1p(-np.exp2(-5.0 - np.arange(heads, dtype=np.float64)))
    i = np.arange(rows, dtype=np.float64)
    diff = i[:, None] - i[None, :]
    valid = (diff >= 0) & (i[:, None] < length) & (i[None, :] < length)
    dec = np.where(valid[None], np.exp(np.maximum(diff, 0.0)[None] * lg[:, None, None]), 0.0)
    qd = np.exp((i[None, :] + 1.0) * lg[:, None])
    kd = np.where(i[None, :] < length, np.exp((length - 1.0 - i[None, :]) * lg[:, None]), 0.0)
    qd = np.broadcast_to(qd[:, :, None], (heads, rows, dk))
    kd = np.broadcast_to(kd[:, :, None], (heads, rows, dk))
    g_l = tuple(float(v) for v in np.exp(length * lg))
    return (jnp.asarray(dec, F32), jnp.asarray(qd, F32), jnp.asarray(kd, F32), g_l)


def _ret_prompt(rq, rk, rv, rg, batch, seq, dims):
    heads, dk, dv = dims["ret_heads"], dims["ret_dk"], dims["ret_dv"]
    L = RET_CHUNK
    nc = seq // L
    dec, qd, kd, g_l = _ret_consts(heads, L, dk)
    nb = RET_SEQS_PER_STEP
    row = lambda c: pl.BlockSpec((nb, L, c), lambda b, i: (b, i, 0))
    per_seq = lambda a: a.reshape(batch, seq, a.shape[-1])
    kern = functools.partial(_ret_prompt_kernel, heads=heads, dk=dk, dv=dv, g_l=g_l)
    o, state = pl.pallas_call(
        kern,
        grid=(batch // nb, nc),
        in_specs=[row(heads * dk), row(heads * dk), row(heads * dv), row(heads * dv),
                  _resident(dec.shape), _resident(qd.shape), _resident(kd.shape)],
        out_specs=[row(heads * dv), pl.BlockSpec((nb, heads, dk, dv), lambda b, i: (b, 0, 0, 0))],
        out_shape=[jax.ShapeDtypeStruct((batch, seq, heads * dv), BF16),
                   jax.ShapeDtypeStruct((batch, heads, dk, dv), F32)],
        scratch_shapes=[pltpu.VMEM((nb, heads, dk, dv), F32)],
        compiler_params=_params(2),
        name="retention_prompt",
    )(per_seq(rq), per_seq(rk), per_seq(rv), per_seq(rg), dec, qd, kd)
    return o.reshape(batch * seq, heads * dv), state


def _ret_sample_kernel(rq, rk, rv, rg, s0, dec, qd, kd, o_ref, s_out, *, group, heads, dk, dv, g_l):
    pairs = [(i, h) for i in range(group) for h in range(heads)]

    def state_side(i, h):
        q = rq[i, :, h * dk:(h + 1) * dk]
        k = rk[i, :, h * dk:(h + 1) * dk]
        v = rv[i, :, h * dv:(h + 1) * dv]
        s = s0[i, h]
        read = _dot((q.astype(F32) * qd[h]).astype(BF16), s.astype(BF16))
        s_out[i, h] = s * g_l[h] + _dot_tn((k.astype(F32) * kd[h]).astype(BF16), v)
        return read

    def output_side(i, h, read):
        q = rq[i, :, h * dk:(h + 1) * dk]
        k = rk[i, :, h * dk:(h + 1) * dk]
        v = rv[i, :, h * dv:(h + 1) * dv]
        o = _dot((_dot_nt(q, k) * dec[h]).astype(BF16), v) + read
        gf = rg[i, :, h * dv:(h + 1) * dv].astype(F32)
        o_ref[i, :, h * dv:(h + 1) * dv] = ((gf * _sigmoid(gf)) * _rms(o)).astype(BF16)

    read = state_side(*pairs[0])
    for n, (i, h) in enumerate(pairs):
        nxt = state_side(*pairs[n + 1]) if n + 1 < len(pairs) else None
        output_side(i, h, read)
        read = nxt


def _ret_sample(rq, rk, rv, rg, state, tokens, dims):
    heads, dk, dv = dims["ret_heads"], dims["ret_dk"], dims["ret_dv"]
    b, tp, _ = rq.shape
    G = RET_SAMPLE_GROUP
    dec, qd, kd, g_l = _ret_consts(heads, tokens, dk, chunk_rows=tp)
    blk = lambda c: pl.BlockSpec((G, tp, c), lambda i: (i, 0, 0))
    st = pl.BlockSpec((G, heads, dk, dv), lambda i: (i, 0, 0, 0))
    kern = functools.partial(_ret_sample_kernel, group=G, heads=heads, dk=dk, dv=dv, g_l=g_l)
    return pl.pallas_call(
        kern,
        grid=(b // G,),
        in_specs=[blk(heads * dk), blk(heads * dk), blk(heads * dv), blk(heads * dv), st,
                  _resident(dec.shape), _resident(qd.shape), _resident(kd.shape)],
        out_specs=[blk(heads * dv), st],
        out_shape=[jax.ShapeDtypeStruct((b, tp, heads * dv), BF16),
                   jax.ShapeDtypeStruct((b, heads, dk, dv), F32)],
        compiler_params=_params(1),
        name="retention_sample",
    )(rq, rk, rv, rg, state, dec, qd, kd)


def _mla_prompt_kernel(q_ref, k_ref, v_ref, o_ref, m_scr, l_scr, acc_scr, *, tq, tk, hp, group, d_v):
    qi = pl.program_id(2)
    ri = lax.broadcasted_iota(jnp.int32, (tq, tk), 0)
    ci = lax.broadcasted_iota(jnp.int32, (tq, tk), 1)
    causal_bias = jnp.where(ci <= ri, 0.0, NEG_BIG)

    def block(j, carry, first):
        start = pl.multiple_of(j * tk, tk)

        def scores(h):
            return _dot_nt(q_ref[:, h * group:(h + 1) * group], k_ref[pl.ds(start, tk), h * group:(h + 1) * group])

        s_next = scores(0)
        for h in range(hp):
            s = s_next
            if h + 1 < hp:
                s_next = scores(h + 1)
            vals = v_ref[pl.ds(start, tk), h * d_v:(h + 1) * d_v]
            if first:
                s = s + causal_bias
                m_new = jnp.broadcast_to(jnp.max(s, axis=-1, keepdims=True), (tq, LANES))
                p = jnp.exp2(s - jnp.tile(m_new, (1, tk // LANES)))
                l_scr[h] = jnp.broadcast_to(jnp.sum(p, axis=-1, keepdims=True), (tq, LANES))
                acc_scr[h] = _dot(p.astype(BF16), vals)
            else:
                m_prev = m_scr[h]
                m_new = jnp.maximum(m_prev, jnp.max(s, axis=-1, keepdims=True))
                a = jnp.exp2(m_prev - m_new)
                p = jnp.exp2(s - jnp.tile(m_new, (1, tk // LANES)))
                l_scr[h] = a * l_scr[h] + jnp.sum(p, axis=-1, keepdims=True)
                acc_scr[h] = a * acc_scr[h] + _dot(p.astype(BF16), vals)
            m_scr[h] = m_new
        return carry

    block(qi, 0, True)
    lax.fori_loop(0, qi, functools.partial(block, first=False), 0)
    for h in range(hp):
        o_ref[:, h * d_v:(h + 1) * d_v] = (acc_scr[h] / l_scr[h]).astype(BF16)


def _mla_prompt(qcat, kcat, v, batch, seq, dims):
    heads, d_nope, d_v = dims["heads"], dims["d_nope"], dims["d_v"]
    group = d_nope + LANES
    t = ATTN_Q_TILE
    hp = ATTN_HEADS_PER_STEP
    nq = seq // t
    assert ATTN_KV_TILE == t and d_v == LANES
    kern = functools.partial(_mla_prompt_kernel, tq=t, tk=ATTN_KV_TILE, hp=hp, group=group, d_v=d_v)
    return pl.pallas_call(
        kern,
        grid=(batch, heads // hp, nq),
        in_specs=[pl.BlockSpec((t, hp * group), lambda b, h, i: (b * nq + i, h)),
                  pl.BlockSpec((seq, hp * group), lambda b, h, i: (b, h)),
                  pl.BlockSpec((seq, hp * d_v), lambda b, h, i: (b, h))],
        out_specs=pl.BlockSpec((t, hp * d_v), lambda b, h, i: (b * nq + i, h)),
        out_shape=jax.ShapeDtypeStruct((batch * seq, heads * d_v), BF16),
        scratch_shapes=[pltpu.VMEM((hp, t, LANES), F32), pltpu.VMEM((hp, t, LANES), F32),
                        pltpu.VMEM((hp, t, d_v), F32)],
        compiler_params=_params(3),
        name="mla_prompt",
    )(qcat, kcat, v)


def _mla_sample_kernel(pt_ref, ql_ref, qp_ref, cn_ref, kn_ref, ckv_hbm, kpe_hbm, o_ref,
                       ckv_buf, kpe_buf, kbf, s_scr, sem,
                       *, layer, n_pages, page, chunk, tokens, heads, scale):
    b = pl.program_id(0)
    past = n_pages * page
    d_rope = qp_ref.shape[2]

    def page_copy(seq, slot, p, array):
        pg = pt_ref[seq, p]
        if array == 0:
            off = pl.multiple_of(p * page, page)
            return pltpu.make_async_copy(ckv_hbm.at[layer, pg], ckv_buf.at[slot, pl.ds(off, page), :],
                                         sem.at[0, slot])
        off = pl.multiple_of(p * d_rope, d_rope)
        return pltpu.make_async_copy(kpe_hbm.at[layer, pg], kpe_buf.at[slot, pl.ds(off, d_rope), :],
                                     sem.at[1, slot])

    def for_each_page(seq, slot, fn, arrays=(0, 1), unroll=DMA_LOOP_UNROLL):
        def body(p, carry):
            for a in arrays:
                fn(page_copy(seq, slot, p, a))
            return carry
        lax.fori_loop(0, n_pages, body, 0, unroll=unroll)

    @pl.when(b == 0)
    def _():
        for_each_page(0, 0, lambda cp: cp.start())

    @pl.when(b + 1 < pl.num_programs(0))
    def _():
        for_each_page(b + 1, (b + 1) % 2, lambda cp: cp.start())

    slot = b % 2
    for_each_page(b, slot, lambda cp: cp.wait(), arrays=(0,), unroll=True)
    for_each_page(b, slot, lambda cp: cp.wait(), arrays=(1,), unroll=True)

    ql = ql_ref[0]
    qp = qp_ref[0]
    ckv_s = ckv_buf.at[slot]
    kpe_s = kpe_buf.at[slot]
    n_chunks = past // chunk

    cn = cn_ref[0].astype(BF16)
    kn = kn_ref[0].astype(BF16)
    s_new = (_dot_nt(ql, cn) + _dot_nt(qp, kn)) * scale
    t = lax.broadcasted_iota(jnp.int32, s_new.shape, 0) // heads
    col = lax.broadcasted_iota(jnp.int32, s_new.shape, 1)
    s_new = jnp.where((col <= t) & (col < tokens), s_new, NEG_BIG)
    m = jnp.max(s_new, axis=-1, keepdims=True)
    p_new = jnp.exp(s_new - m)
    l = jnp.sum(p_new, axis=-1, keepdims=True)
    acc = _dot(p_new.astype(BF16), cn)

    def scores(c):
        sl = slice(c * chunk, (c + 1) * chunk)
        kc = ckv_s[sl, :].astype(BF16)
        kbf[sl, :] = kc
        pe = jnp.concatenate([kpe_s[p * d_rope:(p + 1) * d_rope, :]
                              for p in range(c * chunk // page, (c + 1) * chunk // page)], axis=1)
        s_scr[:, sl] = (_dot_nt(ql, kc) + _dot(qp, pe.astype(BF16))) * scale

    def values(c, m, l, acc):
        sl = slice(c * chunk, (c + 1) * chunk)
        s = s_scr[:, sl]
        m_new = jnp.maximum(m, jnp.max(s, axis=-1, keepdims=True))
        a = jnp.exp(m - m_new)
        p = jnp.exp(s - m_new)
        l = a * l + jnp.sum(p, axis=-1, keepdims=True)
        return m_new, l, a * acc + _dot(p.astype(BF16), kbf[sl, :])

    scores(0)
    for c in range(n_chunks):
        if c + 1 < n_chunks:
            scores(c + 1)
        m, l, acc = values(c, m, l, acc)
    o_ref[0] = acc / l


def _mla_sample(ql, qp, ckv_new, kpe_new, cache_ckv, cache_kpe_t, layer, page_table, tokens, dims):
    heads, d_nope, d_rope, kv_lora = dims["heads"], dims["d_nope"], dims["d_rope"], dims["kv_lora"]
    b, rows, _ = ql.shape
    n_pages = page_table.shape[1]
    page = cache_ckv.shape[2]
    past = n_pages * page
    tp = ckv_new.shape[1]
    per_b = lambda r, c: pl.BlockSpec((1, r, c), lambda i, pt: (i, 0, 0))
    hbm = pl.BlockSpec(memory_space=pl.ANY)
    kern = functools.partial(_mla_sample_kernel, layer=layer, n_pages=n_pages, page=page, chunk=DECODE_CHUNK,
                             tokens=tokens, heads=heads, scale=(d_nope + d_rope) ** -0.5)
    grid_spec = pltpu.PrefetchScalarGridSpec(
        num_scalar_prefetch=1,
        grid=(b,),
        in_specs=[per_b(rows, kv_lora), per_b(rows, d_rope), per_b(tp, kv_lora), per_b(tp, d_rope), hbm, hbm],
        out_specs=per_b(rows, kv_lora),
        scratch_shapes=[pltpu.VMEM((2, past, kv_lora), F32), pltpu.VMEM((2, n_pages * d_rope, page), F32),
                        pltpu.VMEM((past, kv_lora), BF16), pltpu.VMEM((rows, past), F32),
                        pltpu.SemaphoreType.DMA((2, 2))],
    )
    return pl.pallas_call(
        kern,
        grid_spec=grid_spec,
        out_shape=jax.ShapeDtypeStruct((b, rows, kv_lora), F32),
        compiler_params=_params(1),
        name="mla_sample",
    )(page_table, ql, qp, ckv_new, kpe_new, cache_ckv, cache_kpe_t)


def _mem_kv_kernel(m_ref, g_ref, wk, wv, k_o, v_o):
    mn = _rms(m_ref[0], g_ref[...]).astype(BF16)
    k_o[0] = _dot(mn, wk[...])
    v_o[0] = _dot(mn, wv[...])


def _mem_kv(mem, g, wk, wv):
    b, m, d = mem.shape
    c = wk.shape[1]
    return pl.pallas_call(
        _mem_kv_kernel,
        grid=(b,),
        in_specs=[pl.BlockSpec((1, m, d), lambda i: (i, 0, 0)), _resident(g.shape), _resident(wk.shape),
                  _resident(wv.shape)],
        out_specs=[pl.BlockSpec((1, m, c), lambda i: (i, 0, 0))] * 2,
        out_shape=[jax.ShapeDtypeStruct((b, m, c), F32)] * 2,
        compiler_params=_params(1),
        name="mem_kv",
    )(mem, g, wk, wv)


def _x_attend_heads(q, mk, mv, heads, hd):
    r = q.shape[0]
    lane = lax.broadcasted_iota(jnp.int32, q.shape, 1)
    sels = [(lane >= h * hd) & (lane < (h + 1) * hd) for h in range(heads)]
    q_heads = jnp.concatenate([jnp.where(sel, q, jnp.zeros_like(q)) for sel in sels], axis=0)
    s = _dot_nt(q_heads, mk) * (hd ** -0.5)
    p = jnp.exp(s - jnp.max(s, axis=-1, keepdims=True))
    p = p / jnp.sum(p, axis=-1, keepdims=True)
    pv = _dot(p.astype(BF16), mv)
    out = jnp.zeros(q.shape, F32)
    for h, sel in enumerate(sels):
        out = out + jnp.where(sel, pv[h * r:(h + 1) * r], 0.0)
    return out


def _x_sample_kernel(q_ref, kt_ref, vt_ref, o_ref, *, group, heads, hd):
    tp = q_ref.shape[1]
    lane = lax.broadcasted_iota(jnp.int32, q_ref.shape[1:], 1)
    sels = [(lane >= h * hd) & (lane < (h + 1) * hd) for h in range(heads)]
    def scores(g):
        q_heads = jnp.concatenate([jnp.where(sel, q_ref[g], 0.0) for sel in sels], axis=0).astype(BF16)
        return _dot(q_heads, kt_ref[g].astype(BF16)) * (hd ** -0.5)

    s_next = scores(0)
    for g in range(group):
        q = q_ref[g]
        s = s_next
        if g + 1 < group:
            s_next = scores(g + 1)
        p = jnp.exp(s - jnp.max(s, axis=-1, keepdims=True))
        p = p / jnp.sum(p, axis=-1, keepdims=True)
        r = _dot_nt(p.astype(BF16), vt_ref[g].astype(BF16))
        out = jnp.zeros(q.shape, F32)
        for h, sel in enumerate(sels):
            out = out + jnp.where(sel, r[h * tp:(h + 1) * tp], 0.0)
        o_ref[g] = out


def _x_sample(xq, mkt, mvt, heads):
    b, tp, c = xq.shape
    m = mkt.shape[2]
    G = X_SAMPLE_GROUP
    kern = functools.partial(_x_sample_kernel, group=G, heads=heads, hd=c // heads)
    return pl.pallas_call(
        kern,
        grid=(b // G,),
        in_specs=[pl.BlockSpec((G, tp, c), lambda i: (i, 0, 0)),
                  pl.BlockSpec((G, c, m), lambda i: (i, 0, 0)),
                  pl.BlockSpec((G, c, m), lambda i: (i, 0, 0))],
        out_specs=pl.BlockSpec((G, tp, c), lambda i: (i, 0, 0)),
        out_shape=jax.ShapeDtypeStruct((b, tp, c), F32),
        compiler_params=_params(1),
        name="x_attend_sample",
    )(xq, mkt, mvt)


def _merge_kernel(x_ref, gpre, wgate, oret, wret, omla, wmla, wuv, ox, wx, wout, gpost, *rest,
                  heads, kv_lora, d_v, absorbed, x_heads):
    h_o = rest[-1]
    if x_heads:
        mk, mv = rest[:2]
        o_x = _x_attend_heads(ox[...], mk[0].astype(BF16), mv[0].astype(BF16), x_heads, ox.shape[1] // x_heads)
    else:
        o_x = ox[...]
    x = x_ref[...]
    d = x.shape[1]
    u = _rms(x, gpre[...]).astype(BF16)
    a_ret = _dot(oret[...], wret[...])
    if absorbed:
        a_mla = jnp.zeros(x.shape, F32)
        for h in range(heads):
            o_h = _dot(omla[:, h * kv_lora:(h + 1) * kv_lora].astype(BF16), wuv[h]).astype(BF16)
            a_mla = a_mla + _dot(o_h, wmla[h * d_v:(h + 1) * d_v, :])
    else:
        a_mla = _dot(omla[...], wmla[...])
    a_x = _dot(o_x.astype(BF16), wx[...])
    mixed = jnp.zeros(x.shape, F32)
    for i, a in enumerate((a_ret, a_mla, a_x)):
        mixed = mixed + _sigmoid(_dot_nt(u, wgate[i * d:(i + 1) * d, :])) * a
    h_o[...] = x + _rms(_dot(mixed.astype(BF16), wout[...]), gpost[...])


def _merge(x, oret, omla, ox, w, dims, *, absorbed, tm, mem=None):
    n, d = x.shape
    row = lambda a: pl.BlockSpec((tm, a.shape[1]), lambda i: (i, 0))
    res = lambda a: _resident(a.shape)
    args = [x, w["g_pre"], w["w_gates"], oret, w["w_ret_o"], omla, w["w_mla_o"], w["w_uv_h"], ox, w["w_x_o"],
            w["w_out"], w["g_post"]]
    specs = [row(x), res(w["g_pre"]), res(w["w_gates"]), row(oret), res(w["w_ret_o"]), row(omla),
             res(w["w_mla_o"]), res(w["w_uv_h"]), row(ox), res(w["w_x_o"]), res(w["w_out"]), res(w["g_post"])]
    x_heads = 0
    if mem is not None:
        mk, mv, x_heads, seq_rows = mem
        tiles_per_seq = seq_rows // tm
        per_seq = pl.BlockSpec((1,) + mk.shape[1:], lambda i: (i // tiles_per_seq, 0, 0))
        args += [mk, mv]
        specs += [per_seq, per_seq]
    kern = functools.partial(_merge_kernel, heads=dims["heads"], kv_lora=dims["kv_lora"], d_v=dims["d_v"],
                             absorbed=absorbed, x_heads=x_heads)
    return pl.pallas_call(
        kern,
        grid=(n // tm,),
        in_specs=specs,
        out_specs=pl.BlockSpec((tm, d), lambda i: (i, 0)),
        out_shape=jax.ShapeDtypeStruct((n, d), F32),
        compiler_params=_params(1),
        name="merge_sample" if absorbed else "merge_prompt",
    )(*args)


def _ffn_kernel(h_ref, gpre, wg, wu, wd, gpost, y_o):
    h = h_ref[...]
    f = _rms(h, gpre[...]).astype(BF16)
    gate = _dot(f, wg[...])
    act = (gate * _sigmoid(gate) * _dot(f, wu[...])).astype(BF16)
    y_o[...] = h + _rms(_dot(act, wd[...]), gpost[...])


def _ffn(h, w, *, tm):
    n, d = h.shape
    res = lambda a: _resident(a.shape)
    return pl.pallas_call(
        _ffn_kernel,
        grid=(n // tm,),
        in_specs=[pl.BlockSpec((tm, d), lambda i: (i, 0)), res(w["g_ffn_pre"]), res(w["w_ffn_gate"]),
                  res(w["w_ffn_up"]), res(w["w_ffn_down"]), res(w["g_ffn_post"])],
        out_specs=pl.BlockSpec((tm, d), lambda i: (i, 0)),
        out_shape=jax.ShapeDtypeStruct((n, d), F32),
        compiler_params=_params(1),
        name="ffn",
    )(h, w["g_ffn_pre"], w["w_ffn_gate"], w["w_ffn_up"], w["w_ffn_down"], w["g_ffn_post"])


def _rope_tables(pos, ret_dk, d_rope):
    posf = pos.astype(F32)[:, None]

    def angles(half):
        inv = ROPE_BASE ** (-jnp.arange(half, dtype=F32) / half)
        ang = posf * inv[None, :]
        return jnp.cos(ang), jnp.sin(ang)

    cr, sr = angles(ret_dk // 2)
    cp, sp = angles(d_rope // 2)
    z = jnp.zeros_like(cp)
    pad = jnp.zeros((pos.shape[0], LANES - d_rope), F32)
    return {
        "cosr": jnp.concatenate([cr, cr], axis=1),
        "sinr": jnp.concatenate([-sr, sr], axis=1),
        "cosp": jnp.concatenate([cp, cp, pad], axis=1),
        "sinlo": jnp.concatenate([-sp, z, pad], axis=1),
        "sinhi": jnp.concatenate([z, sp, pad], axis=1),
    }


def _layer_weights(l, dims, sizes, norm_mix_pre, norm_mix_post, norm_ffn_pre, norm_ffn_post, norm_mem, norm_q_lat,
                   norm_kv_lat, w_in, w_uq, w_uk, w_uv, w_mem_k, w_mem_v, w_ret_o, w_mla_o, w_x_o, w_out,
                   w_ffn_gate, w_ffn_up, w_ffn_down):
    heads, d_nope, d_rope, kv_lora, d_v = (dims[k] for k in ("heads", "d_nope", "d_rope", "kv_lora", "d_v"))
    bf = lambda a: a.astype(BF16)
    gain = lambda a: a[l].astype(F32)[None, :]
    offs = np.concatenate([[0], np.cumsum(sizes)])
    w_in_t = bf(jnp.swapaxes(w_in[l], 0, 1))
    seg = [w_in_t[offs[i]:offs[i + 1], :] for i in range(len(sizes))]
    q_lora = w_uq.shape[1]
    uq = w_uq[l].reshape(q_lora, heads, d_nope + d_rope)
    uq_p = jnp.pad(uq[:, :, d_nope:], ((0, 0), (0, 0), (0, LANES - d_rope)))
    return {
        "g_pre": gain(norm_mix_pre), "g_post": gain(norm_mix_post), "g_ffn_pre": gain(norm_ffn_pre),
        "g_ffn_post": gain(norm_ffn_post), "g_mem": gain(norm_mem), "g_q": gain(norm_q_lat),
        "g_kv": gain(norm_kv_lat),
        "w_rq": seg[0], "w_rk": seg[1], "w_rv": seg[2], "w_rg": seg[3], "w_cq": seg[4],
        "w_ckv": seg[5], "w_kpe": jnp.pad(seg[6], ((0, LANES - d_rope), (0, 0))), "w_xq": seg[7],
        "w_gates": seg[8],
        "w_uq_n": bf(uq[:, :, :d_nope].reshape(q_lora, heads * d_nope)),
        "w_uq_p": bf(uq_p.reshape(q_lora, heads * LANES)),
        "w_uk_t": bf(jnp.swapaxes(w_uk[l], 1, 2)),
        "w_kn": bf(jnp.swapaxes(w_uk[l], 0, 1).reshape(kv_lora, heads * d_nope)),
        "w_vn": bf(jnp.swapaxes(w_uv[l], 0, 1).reshape(kv_lora, heads * d_v)),
        "w_uv_h": bf(w_uv[l]),
        "w_mem_k": bf(w_mem_k[l]), "w_mem_v": bf(w_mem_v[l]),
        "w_ret_o": bf(w_ret_o[l]), "w_mla_o": bf(w_mla_o[l]), "w_x_o": bf(w_x_o[l]), "w_out": bf(w_out[l]),
        "w_ffn_gate": bf(w_ffn_gate[l]), "w_ffn_up": bf(w_ffn_up[l]), "w_ffn_down": bf(w_ffn_down[l]),
    }


def _pad_tokens(a, b, tokens, rows=SAMPLE_TOK_PAD):
    a = a.reshape(b, tokens, a.shape[-1])
    return jnp.pad(a, ((0, 0), (0, rows - tokens), (0, 0)))


def kernel(x_prompt, x_sample, mem_prompt, cache_ckv, cache_kpe, page_table, state_ret, cache_mem_k, cache_mem_v,
           norm_mix_pre, norm_mix_post, norm_ffn_pre, norm_ffn_post, norm_mem, norm_q_lat, norm_kv_lat, w_in,
           w_uq, w_uk, w_uv, w_mem_k, w_mem_v, w_ret_o, w_mla_o, w_x_o, w_out, w_ffn_gate, w_ffn_up, w_ffn_down):
    depth = w_in.shape[0]
    batch, seq, d_model = x_prompt.shape
    db, tokens, _ = x_sample.shape
    ret_heads, ret_dk, ret_dv = state_ret.shape[2:]
    heads, kv_lora, d_nope = w_uk.shape[1:]
    d_rope = cache_kpe.shape[-1]
    d_v = w_uv.shape[-1]
    n_mem, x_heads, x_hd = cache_mem_k.shape[2:]
    q_lora = w_uq.shape[1]
    past_len = page_table.shape[1] * cache_ckv.shape[2]
    dims = dict(heads=heads, d_nope=d_nope, d_rope=d_rope, kv_lora=kv_lora, d_v=d_v,
                ret_heads=ret_heads, ret_dk=ret_dk, ret_dv=ret_dv)
    assert ret_dk == LANES and d_nope == LANES and d_rope <= LANES // 2 and tokens <= SAMPLE_TOK_PAD
    sizes = (ret_heads * ret_dk, ret_heads * ret_dk, ret_heads * ret_dv, ret_heads * ret_dv, q_lora, kv_lora,
             d_rope, x_heads * x_hd, w_in.shape[2] - (2 * ret_heads * ret_dk + 2 * ret_heads * ret_dv + q_lora
                                                       + kv_lora + d_rope + x_heads * x_hd))

    cache_kpe_t = jnp.swapaxes(cache_kpe, 2, 3)

    tabs_p = _rope_tables(jnp.arange(seq), ret_dk, d_rope)
    n_s = db * tokens
    tabs_s = _rope_tables(past_len + (jnp.arange(n_s) % tokens), ret_dk, d_rope)

    y_p = x_prompt.reshape(batch * seq, d_model)
    y_s = x_sample.reshape(n_s, d_model)
    outs = [[] for _ in range(8)]
    for l in range(depth):
        w = _layer_weights(l, dims, sizes, norm_mix_pre, norm_mix_post, norm_ffn_pre, norm_ffn_post, norm_mem,
                           norm_q_lat, norm_kv_lat, w_in, w_uq, w_uk, w_uv, w_mem_k, w_mem_v, w_ret_o, w_mla_o,
                           w_x_o, w_out, w_ffn_gate, w_ffn_up, w_ffn_down)

        mk_p, mv_p = _mem_kv(mem_prompt, w["g_mem"], w["w_mem_k"], w["w_mem_v"])
        rq, rk, rv, rg, qcat, xq, ckv_p, kpe_p, kcat, vn = _in_proj(y_p, w, tabs_p, dims, absorbed=False,
                                                                    tm=ROW_TILE)
        o_ret, ret_p = _ret_prompt(rq, rk, rv, rg, batch, seq, dims)
        o_mla = _mla_prompt(qcat, kcat, vn, batch, seq, dims)
        h_p = _merge(y_p, o_ret, o_mla, xq, w, dims, absorbed=False, tm=ROW_TILE, mem=(mk_p, mv_p, x_heads, seq))
        y_p = _ffn(h_p, w, tm=ROW_TILE)

        rq, rk, rv, rg, qcat, xq, ckv_s, kpe_s, qlat = _in_proj(y_s, w, tabs_s, dims, absorbed=True, tm=ROW_TILE)
        pad = lambda a: _pad_tokens(a, db, tokens)
        o_ret, ret_s = _ret_sample(pad(rq), pad(rk), pad(rv), pad(rg), state_ret[l].astype(F32), tokens, dims)
        o_ret = o_ret[:, :tokens].reshape(n_s, ret_heads * ret_dv)
        group = d_nope + LANES
        q_pe = qcat.reshape(db, tokens * heads, group)[:, :, d_nope:d_nope + d_rope]
        o_lat = _mla_sample(qlat.reshape(db, tokens * heads, kv_lora), q_pe, pad(ckv_s), pad(kpe_s),
                            cache_ckv, cache_kpe_t, l, page_table, tokens, dims)
        o_lat = o_lat.reshape(n_s, heads * kv_lora)
        mem_t = lambda c: jnp.transpose(c[l], (0, 2, 3, 1)).reshape(db, x_heads * x_hd, n_mem)
        o_x = _x_sample(_pad_tokens(xq.astype(F32), db, tokens, rows=X_SAMPLE_TOK_PAD), mem_t(cache_mem_k),
                        mem_t(cache_mem_v), x_heads)
        o_x = o_x[:, :tokens].reshape(n_s, x_heads * x_hd)
        h_s = _merge(y_s, o_ret, o_lat, o_x, w, dims, absorbed=True, tm=ROW_TILE)
        y_s = _ffn(h_s, w, tm=ROW_TILE)

        for lst, val in zip(outs, (ckv_p.reshape(batch, seq, kv_lora), kpe_p.reshape(batch, seq, d_rope),
                                   ckv_s.reshape(db, tokens, kv_lora), kpe_s.reshape(db, tokens, d_rope),
                                   ret_p.astype(x_prompt.dtype), ret_s.astype(state_ret.dtype),
                                   mk_p.reshape(batch, n_mem, x_heads, x_hd),
                                   mv_p.reshape(batch, n_mem, x_heads, x_hd))):
            lst.append(val)

    return (y_p.reshape(batch, seq, d_model), y_s.reshape(db, tokens, d_model)) + tuple(jnp.stack(o) for o in outs)
```

```python
import functools
import math

import numpy as np
import jax
import jax.numpy as jnp
from jax import lax
from jax.experimental import pallas as pl
from jax.experimental.pallas import tpu as pltpu

F32 = jnp.float32
BF16 = jnp.bfloat16

ROPE_BASE = 10000.0
RMS_EPS = 1e-6
LANES = 128
VMEM_LIMIT = 56 * 1024 * 1024
NEG_BIG = -1e30

ROW_TILE = 512
RET_CHUNK = 256
RET_SEQS_PER_STEP = 4
ATTN_Q_TILE = 512
ATTN_KV_TILE = 512
ATTN_HEADS_PER_STEP = 8
SAMPLE_TOK_PAD = 16
X_SAMPLE_TOK_PAD = 8
RET_SAMPLE_GROUP = 16
X_SAMPLE_GROUP = 16
DECODE_CHUNK = 4096
DMA_LOOP_UNROLL = 4


def _resident(shape):
    nd = len(shape)
    return pl.BlockSpec(shape, lambda *_: (0,) * nd, pipeline_mode=pl.Buffered(1))


def _params(n_axes):
    return pltpu.CompilerParams(dimension_semantics=("arbitrary",) * n_axes, vmem_limit_bytes=VMEM_LIMIT)


def _rms(x, g=None):
    y = x * lax.rsqrt(jnp.mean(x * x, axis=-1, keepdims=True) + RMS_EPS)
    return y if g is None else y * g


def _sigmoid(x):
    return 1.0 / (1.0 + jnp.exp(-x))


def _dot(a, b):
    return jnp.dot(a, b, preferred_element_type=F32)


def _dot_nt(a, b):
    return lax.dot_general(a, b, (((1,), (1,)), ((), ())), preferred_element_type=F32)


def _dot_tn(a, b):
    return lax.dot_general(a, b, (((0,), (0,)), ((), ())), preferred_element_type=F32)


def _rope_half_vreg(z, cos_t, sin_lo, sin_hi, quarter):
    return (z * cos_t + pltpu.roll(z, LANES - quarter, 1) * sin_lo + pltpu.roll(z, quarter, 1) * sin_hi)


def _in_proj_kernel(x_ref, g_ref, wrq, wrk, wrv, wrg, wcq, wckv, wkpe, wxq, gq_ref, gkv_ref, wuqn, wuqp,
                    cosr, sinr, cosp, sinlo, sinhi, *rest,
                    ret_heads, ret_dk, heads, d_nope, d_rope, kv_lora, absorbed, q_scale):
    if absorbed:
        wa, (rq_o, rk_o, rv_o, rg_o, qcat_o, xq_o, ckv_o, kpe_o, a_o) = rest[0], rest[1:]
    else:
        wa, wb, (rq_o, rk_o, rv_o, rg_o, qcat_o, xq_o, ckv_o, kpe_o, a_o, b_o) = rest[0], rest[1], rest[2:]
    u = _rms(x_ref[...], g_ref[...]).astype(BF16)
    cr, sr = cosr[...], sinr[...]
    cp, slo, shi = cosp[...], sinlo[...], sinhi[...]
    group = d_nope + LANES
    k_scale = ret_dk ** -0.5

    cq = _dot_nt(u, wcq[...])
    ckv = _dot_nt(u, wckv[...])
    kpe_raw = _dot_nt(u, wkpe[...])
    zq = _dot_nt(u, wrq[...])
    zk = _dot_nt(u, wrk[...])

    cqn = _rms(cq, gq_ref[...]).astype(BF16)
    ckvn = _rms(ckv, gkv_ref[...])
    ckv_o[...] = ckvn
    kpe = _rope_half_vreg(kpe_raw, cp, slo, shi, d_rope // 2)
    kpe_o[...] = kpe[:, :d_rope]

    qn = _dot(cqn, wuqn[...])
    qp = _dot(cqn, wuqp[...])
    if not absorbed:
        ckvb = ckvn.astype(BF16)
        kn = _dot(ckvb, wa[...])
        vn = _dot(ckvb, wb[...])

    for h in range(ret_heads):
        sl = slice(h * ret_dk, (h + 1) * ret_dk)
        q_h = zq[:, sl]
        k_h = zk[:, sl]
        rq_o[:, sl] = (q_h * cr + pltpu.roll(q_h, ret_dk // 2, 1) * sr).astype(BF16)
        rk_o[:, sl] = ((k_h * cr + pltpu.roll(k_h, ret_dk // 2, 1) * sr) * k_scale).astype(BF16)

    rv = _dot_nt(u, wrv[...])
    rg = _dot_nt(u, wrg[...])
    xq = _dot_nt(u, wxq[...])

    for h in range(heads):
        qn_h = qn[:, h * d_nope:(h + 1) * d_nope]
        qp_h = _rope_half_vreg(qp[:, h * LANES:(h + 1) * LANES], cp, slo, shi, d_rope // 2)
        if absorbed:
            a_o[:, h * kv_lora:(h + 1) * kv_lora] = _dot(qn_h.astype(BF16), wa[h]).astype(BF16)
        else:
            qn_h = qn_h * q_scale
            qp_h = qp_h * q_scale
        qcat_o[:, h * group:h * group + d_nope] = qn_h.astype(BF16)
        qcat_o[:, h * group + d_nope:(h + 1) * group] = qp_h.astype(BF16)
    if not absorbed:
        kpeb = kpe.astype(BF16)
        for h in range(heads):
            a_o[:, h * group:h * group + d_nope] = kn[:, h * d_nope:(h + 1) * d_nope].astype(BF16)
            a_o[:, h * group + d_nope:(h + 1) * group] = kpeb
        b_o[...] = vn.astype(BF16)

    rv_o[...] = rv.astype(BF16)
    rg_o[...] = rg.astype(BF16)
    xq_o[...] = xq.astype(BF16)


def _in_proj(x, w, tabs, dims, *, absorbed, tm):
    n, d_model = x.shape
    heads, d_nope, d_rope, kv_lora = dims["heads"], dims["d_nope"], dims["d_rope"], dims["kv_lora"]
    ret_heads, ret_dk, ret_dv = dims["ret_heads"], dims["ret_dk"], dims["ret_dv"]
    group = d_nope + LANES
    tab_rows = tabs["cosr"].shape[0]
    tab_tiles = tab_rows // tm
    row = lambda c: pl.BlockSpec((tm, c), lambda i: (i, 0))
    tab = lambda: pl.BlockSpec((tm, LANES), lambda i: (i % tab_tiles, 0))
    weights = [w["g_pre"], w["w_rq"], w["w_rk"], w["w_rv"], w["w_rg"], w["w_cq"], w["w_ckv"], w["w_kpe"],
               w["w_xq"], w["g_q"], w["g_kv"], w["w_uq_n"], w["w_uq_p"]]
    out_cols = [(ret_heads * ret_dk, BF16), (ret_heads * ret_dk, BF16), (ret_heads * ret_dv, BF16),
                (ret_heads * ret_dv, BF16), (heads * group, BF16), (w["w_xq"].shape[0], BF16), (kv_lora, F32),
                (d_rope, F32)]
    if absorbed:
        mode_weights = [w["w_uk_t"]]
        out_cols += [(heads * kv_lora, BF16)]
    else:
        mode_weights = [w["w_kn"], w["w_vn"]]
        out_cols += [(heads * group, BF16), (w["w_vn"].shape[1], BF16)]
    kern = functools.partial(_in_proj_kernel, ret_heads=ret_heads, ret_dk=ret_dk, heads=heads, d_nope=d_nope,
                             d_rope=d_rope, kv_lora=kv_lora, absorbed=absorbed,
                             q_scale=(d_nope + d_rope) ** -0.5 * math.log2(math.e))
    return pl.pallas_call(
        kern,
        grid=(n // tm,),
        in_specs=([row(d_model)] + [_resident(a.shape) for a in weights] + [tab() for _ in range(5)]
                  + [_resident(a.shape) for a in mode_weights]),
        out_specs=[row(c) for c, _ in out_cols],
        out_shape=[jax.ShapeDtypeStruct((n, c), dt) for c, dt in out_cols],
        compiler_params=_params(1),
        name="in_proj_sample" if absorbed else "in_proj_prompt",
    )(x, *weights, tabs["cosr"], tabs["sinr"], tabs["cosp"], tabs["sinlo"], tabs["sinhi"], *mode_weights)


def _ret_head(q, k, v, g, s, dec, qd, kd, g_l):
    inner = _dot_nt(q, k) * dec
    o = _dot(inner.astype(BF16), v) + _dot((q.astype(F32) * qd).astype(BF16), s.astype(BF16))
    s_new = s * g_l + _dot_tn((k.astype(F32) * kd).astype(BF16), v)
    gf = g.astype(F32)
    o = (gf * _sigmoid(gf)) * _rms(o)
    return o.astype(BF16), s_new


def _ret_prompt_kernel(rq, rk, rv, rg, dec, qd, kd, o_ref, s_out, s_scr, *, heads, dk, dv, g_l):
    c = pl.program_id(1)

    @pl.when(c == 0)
    def _():
        s_scr[...] = jnp.zeros(s_scr.shape, F32)

    for b in range(rq.shape[0]):
        for h in range(heads):
            o, s_new = _ret_head(rq[b, :, h * dk:(h + 1) * dk], rk[b, :, h * dk:(h + 1) * dk],
                                 rv[b, :, h * dv:(h + 1) * dv], rg[b, :, h * dv:(h + 1) * dv],
                                 s_scr[b, h], dec[h], qd[h], kd[h], g_l[h])
            o_ref[b, :, h * dv:(h + 1) * dv] = o
            s_scr[b, h] = s_new

    @pl.when(c == pl.num_programs(1) - 1)
    def _():
        s_out[...] = s_scr[...]


def _ret_consts(heads, length, dk, chunk_rows=None):
    rows = length if chunk_rows is None else chunk_rows
    lg = np.log1p(-np.exp2(-5.0 - np.arange(heads, dtype=np.float64)))
    i = np.arange(rows, dtype=np.float64)
    diff = i[:, None] - i[None, :]
    valid = (diff >= 0) & (i[:, None] < length) & (i[None, :] < length)
    dec = np.where(valid[None], np.exp(np.maximum(diff, 0.0)[None] * lg[:, None, None]), 0.0)
    qd = np.exp((i[None, :] + 1.0) * lg[:, None])
    kd = np.where(i[None, :] < length, np.exp((length - 1.0 - i[None, :]) * lg[:, None]), 0.0)
    qd = np.broadcast_to(qd[:, :, None], (heads, rows, dk))
    kd = np.broadcast_to(kd[:, :, None], (heads, rows, dk))
    g_l = tuple(float(v) for v in np.exp(length * lg))
    return (jnp.asarray(dec, F32), jnp.asarray(qd, F32), jnp.asarray(kd, F32), g_l)


def _ret_prompt(rq, rk, rv, rg, batch, seq, dims):
    heads, dk, dv = dims["ret_heads"], dims["ret_dk"], dims["ret_dv"]
    L = RET_CHUNK
    nc = seq // L
    dec, qd, kd, g_l = _ret_consts(heads, L, dk)
    nb = RET_SEQS_PER_STEP
    row = lambda c: pl.BlockSpec((nb, L, c), lambda b, i: (b, i, 0))
    per_seq = lambda a: a.reshape(batch, seq, a.shape[-1])
    kern = functools.partial(_ret_prompt_kernel, heads=heads, dk=dk, dv=dv, g_l=g_l)
    o, state = pl.pallas_call(
        kern,
        grid=(batch // nb, nc),
        in_specs=[row(heads * dk), row(heads * dk), row(heads * dv), row(heads * dv),
                  _resident(dec.shape), _resident(qd.shape), _resident(kd.shape)],
        out_specs=[row(heads * dv), pl.BlockSpec((nb, heads, dk, dv), lambda b, i: (b, 0, 0, 0))],
        out_shape=[jax.ShapeDtypeStruct((batch, seq, heads * dv), BF16),
                   jax.ShapeDtypeStruct((batch, heads, dk, dv), F32)],
        scratch_shapes=[pltpu.VMEM((nb, heads, dk, dv), F32)],
        compiler_params=_params(2),
        name="retention_prompt",
    )(per_seq(rq), per_seq(rk), per_seq(rv), per_seq(rg), dec, qd, kd)
    return o.reshape(batch * seq, heads * dv), state


def _ret_sample_kernel(rq, rk, rv, rg, s0, dec, qd, kd, o_ref, s_out, *, group, heads, dk, dv, g_l):
    pairs = [(i, h) for i in range(group) for h in range(heads)]

    def state_side(i, h):
        q = rq[i, :, h * dk:(h + 1) * dk]
        k = rk[i, :, h * dk:(h + 1) * dk]
        v = rv[i, :, h * dv:(h + 1) * dv]
        s = s0[i, h]
        read = _dot((q.astype(F32) * qd[h]).astype(BF16), s.astype(BF16))
        s_out[i, h] = s * g_l[h] + _dot_tn((k.astype(F32) * kd[h]).astype(BF16), v)
        return read

    def output_side(i, h, read):
        q = rq[i, :, h * dk:(h + 1) * dk]
        k = rk[i, :, h * dk:(h + 1) * dk]
        v = rv[i, :, h * dv:(h + 1) * dv]
        o = _dot((_dot_nt(q, k) * dec[h]).astype(BF16), v) + read
        gf = rg[i, :, h * dv:(h + 1) * dv].astype(F32)
        o_ref[i, :, h * dv:(h + 1) * dv] = ((gf * _sigmoid(gf)) * _rms(o)).astype(BF16)

    read = state_side(*pairs[0])
    for n, (i, h) in enumerate(pairs):
        nxt = state_side(*pairs[n + 1]) if n + 1 < len(pairs) else None
        output_side(i, h, read)
        read = nxt


def _ret_sample(rq, rk, rv, rg, state, tokens, dims):
    heads, dk, dv = dims["ret_heads"], dims["ret_dk"], dims["ret_dv"]
    b, tp, _ = rq.shape
    G = RET_SAMPLE_GROUP
    dec, qd, kd, g_l = _ret_consts(heads, tokens, dk, chunk_rows=tp)
    blk = lambda c: pl.BlockSpec((G, tp, c), lambda i: (i, 0, 0))
    st = pl.BlockSpec((G, heads, dk, dv), lambda i: (i, 0, 0, 0))
    kern = functools.partial(_ret_sample_kernel, group=G, heads=heads, dk=dk, dv=dv, g_l=g_l)
    return pl.pallas_call(
        kern,
        grid=(b // G,),
        in_specs=[blk(heads * dk), blk(heads * dk), blk(heads * dv), blk(heads * dv), st,
                  _resident(dec.shape), _resident(qd.shape), _resident(kd.shape)],
        out_specs=[blk(heads * dv), st],
        out_shape=[jax.ShapeDtypeStruct((b, tp, heads * dv), BF16),
                   jax.ShapeDtypeStruct((b, heads, dk, dv), F32)],
        compiler_params=_params(1),
        name="retention_sample",
    )(rq, rk, rv, rg, state, dec, qd, kd)


def _mla_prompt_kernel(q_ref, k_ref, v_ref, o_ref, m_scr, l_scr, acc_scr, *, tq, tk, hp, group, d_v):
    qi = pl.program_id(2)
    ri = lax.broadcasted_iota(jnp.int32, (tq, tk), 0)
    ci = lax.broadcasted_iota(jnp.int32, (tq, tk), 1)
    causal_bias = jnp.where(ci <= ri, 0.0, NEG_BIG)

    def block(j, carry, first):
        start = pl.multiple_of(j * tk, tk)

        def scores(h):
            return _dot_nt(q_ref[:, h * group:(h + 1) * group], k_ref[pl.ds(start, tk), h * group:(h + 1) * group])

        s_next = scores(0)
        for h in range(hp):
            s = s_next
            if h + 1 < hp:
                s_next = scores(h + 1)
            vals = v_ref[pl.ds(start, tk), h * d_v:(h + 1) * d_v]
            if first:
                s = s + causal_bias
                m_new = jnp.broadcast_to(jnp.max(s, axis=-1, keepdims=True), (tq, LANES))
                p = jnp.exp2(s - jnp.tile(m_new, (1, tk // LANES)))
                l_scr[h] = jnp.broadcast_to(jnp.sum(p, axis=-1, keepdims=True), (tq, LANES))
                acc_scr[h] = _dot(p.astype(BF16), vals)
            else:
                m_prev = m_scr[h]
                m_new = jnp.maximum(m_prev, jnp.max(s, axis=-1, keepdims=True))
                a = jnp.exp2(m_prev - m_new)
                p = jnp.exp2(s - jnp.tile(m_new, (1, tk // LANES)))
                l_scr[h] = a * l_scr[h] + jnp.sum(p, axis=-1, keepdims=True)
                acc_scr[h] = a * acc_scr[h] + _dot(p.astype(BF16), vals)
            m_scr[h] = m_new
        return carry

    block(qi, 0, True)
    lax.fori_loop(0, qi, functools.partial(block, first=False), 0)
    for h in range(hp):
        o_ref[:, h * d_v:(h + 1) * d_v] = (acc_scr[h] / l_scr[h]).astype(BF16)


def _mla_prompt(qcat, kcat, v, batch, seq, dims):
    heads, d_nope, d_v = dims["heads"], dims["d_nope"], dims["d_v"]
    group = d_nope + LANES
    t = ATTN_Q_TILE
    hp = ATTN_HEADS_PER_STEP
    nq = seq // t
    assert ATTN_KV_TILE == t and d_v == LANES
    kern = functools.partial(_mla_prompt_kernel, tq=t, tk=ATTN_KV_TILE, hp=hp, group=group, d_v=d_v)
    return pl.pallas_call(
        kern,
        grid=(batch, heads // hp, nq),
        in_specs=[pl.BlockSpec((t, hp * group), lambda b, h, i: (b * nq + i, h)),
                  pl.BlockSpec((seq, hp * group), lambda b, h, i: (b, h)),
                  pl.BlockSpec((seq, hp * d_v), lambda b, h, i: (b, h))],
        out_specs=pl.BlockSpec((t, hp * d_v), lambda b, h, i: (b * nq + i, h)),
        out_shape=jax.ShapeDtypeStruct((batch * seq, heads * d_v), BF16),
        scratch_shapes=[pltpu.VMEM((hp, t, LANES), F32), pltpu.VMEM((hp, t, LANES), F32),
                        pltpu.VMEM((hp, t, d_v), F32)],
        compiler_params=_params(3),
        name="mla_prompt",
    )(qcat, kcat, v)


def _mla_sample_kernel(pt_ref, ql_ref, qp_ref, cn_ref, kn_ref, ckv_hbm, kpe_hbm, o_ref,
                       ckv_buf, kpe_buf, kbf, s_scr, sem,
                       *, layer, n_pages, page, chunk, tokens, heads, scale):
    b = pl.program_id(0)
    past = n_pages * page
    d_rope = qp_ref.shape[2]

    def page_copy(seq, slot, p, array):
        pg = pt_ref[seq, p]
        if array == 0:
            off = pl.multiple_of(p * page, page)
            return pltpu.make_async_copy(ckv_hbm.at[layer, pg], ckv_buf.at[slot, pl.ds(off, page), :],
                                         sem.at[0, slot])
        off = pl.multiple_of(p * d_rope, d_rope)
        return pltpu.make_async_copy(kpe_hbm.at[layer, pg], kpe_buf.at[slot, pl.ds(off, d_rope), :],
                                     sem.at[1, slot])

    def for_each_page(seq, slot, fn, arrays=(0, 1), unroll=DMA_LOOP_UNROLL):
        def body(p, carry):
            for a in arrays:
                fn(page_copy(seq, slot, p, a))
            return carry
        lax.fori_loop(0, n_pages, body, 0, unroll=unroll)

    @pl.when(b == 0)
    def _():
        for_each_page(0, 0, lambda cp: cp.start())

    @pl.when(b + 1 < pl.num_programs(0))
    def _():
        for_each_page(b + 1, (b + 1) % 2, lambda cp: cp.start())

    slot = b % 2
    for_each_page(b, slot, lambda cp: cp.wait(), arrays=(0,), unroll=True)
    for_each_page(b, slot, lambda cp: cp.wait(), arrays=(1,), unroll=True)

    ql = ql_ref[0]
    qp = qp_ref[0]
    ckv_s = ckv_buf.at[slot]
    kpe_s = kpe_buf.at[slot]
    n_chunks = past // chunk

    cn = cn_ref[0].astype(BF16)
    kn = kn_ref[0].astype(BF16)
    s_new = (_dot_nt(ql, cn) + _dot_nt(qp, kn)) * scale
    t = lax.broadcasted_iota(jnp.int32, s_new.shape, 0) // heads
    col = lax.broadcasted_iota(jnp.int32, s_new.shape, 1)
    s_new = jnp.where((col <= t) & (col < tokens), s_new, NEG_BIG)
    m = jnp.max(s_new, axis=-1, keepdims=True)
    p_new = jnp.exp(s_new - m)
    l = jnp.sum(p_new, axis=-1, keepdims=True)
    acc = _dot(p_new.astype(BF16), cn)

    def scores(c):
        sl = slice(c * chunk, (c + 1) * chunk)
        kc = ckv_s[sl, :].astype(BF16)
        kbf[sl, :] = kc
        pe = jnp.concatenate([kpe_s[p * d_rope:(p + 1) * d_rope, :]
                              for p in range(c * chunk // page, (c + 1) * chunk // page)], axis=1)
        s_scr[:, sl] = (_dot_nt(ql, kc) + _dot(qp, pe.astype(BF16))) * scale

    def values(c, m, l, acc):
        sl = slice(c * chunk, (c + 1) * chunk)
        s = s_scr[:, sl]
        m_new = jnp.maximum(m, jnp.max(s, axis=-1, keepdims=True))
        a = jnp.exp(m - m_new)
        p = jnp.exp(s - m_new)
        l = a * l + jnp.sum(p, axis=-1, keepdims=True)
        return m_new, l, a * acc + _dot(p.astype(BF16), kbf[sl, :])

    scores(0)
    for c in range(n_chunks):
        if c + 1 < n_chunks:
            scores(c + 1)
        m, l, acc = values(c, m, l, acc)
    o_ref[0] = acc / l


def _mla_sample(ql, qp, ckv_new, kpe_new, cache_ckv, cache_kpe_t, layer, page_table, tokens, dims):
    heads, d_nope, d_rope, kv_lora = dims["heads"], dims["d_nope"], dims["d_rope"], dims["kv_lora"]
    b, rows, _ = ql.shape
    n_pages = page_table.shape[1]
    page = cache_ckv.shape[2]
    past = n_pages * page
    tp = ckv_new.shape[1]
    per_b = lambda r, c: pl.BlockSpec((1, r, c), lambda i, pt: (i, 0, 0))
    hbm = pl.BlockSpec(memory_space=pl.ANY)
    kern = functools.partial(_mla_sample_kernel, layer=layer, n_pages=n_pages, page=page, chunk=DECODE_CHUNK,
                             tokens=tokens, heads=heads, scale=(d_nope + d_rope) ** -0.5)
    grid_spec = pltpu.PrefetchScalarGridSpec(
        num_scalar_prefetch=1,
        grid=(b,),
        in_specs=[per_b(rows, kv_lora), per_b(rows, d_rope), per_b(tp, kv_lora), per_b(tp, d_rope), hbm, hbm],
        out_specs=per_b(rows, kv_lora),
        scratch_shapes=[pltpu.VMEM((2, past, kv_lora), F32), pltpu.VMEM((2, n_pages * d_rope, page), F32),
                        pltpu.VMEM((past, kv_lora), BF16), pltpu.VMEM((rows, past), F32),
                        pltpu.SemaphoreType.DMA((2, 2))],
    )
    return pl.pallas_call(
        kern,
        grid_spec=grid_spec,
        out_shape=jax.ShapeDtypeStruct((b, rows, kv_lora), F32),
        compiler_params=_params(1),
        name="mla_sample",
    )(page_table, ql, qp, ckv_new, kpe_new, cache_ckv, cache_kpe_t)


def _mem_kv_kernel(m_ref, g_ref, wk, wv, k_o, v_o):
    mn = _rms(m_ref[0], g_ref[...]).astype(BF16)
    k_o[0] = _dot(mn, wk[...])
    v_o[0] = _dot(mn, wv[...])


def _mem_kv(mem, g, wk, wv):
    b, m, d = mem.shape
    c = wk.shape[1]
    return pl.pallas_call(
        _mem_kv_kernel,
        grid=(b,),
        in_specs=[pl.BlockSpec((1, m, d), lambda i: (i, 0, 0)), _resident(g.shape), _resident(wk.shape),
                  _resident(wv.shape)],
        out_specs=[pl.BlockSpec((1, m, c), lambda i: (i, 0, 0))] * 2,
        out_shape=[jax.ShapeDtypeStruct((b, m, c), F32)] * 2,
        compiler_params=_params(1),
        name="mem_kv",
    )(mem, g, wk, wv)


def _x_attend_heads(q, mk, mv, heads, hd):
    r = q.shape[0]
    lane = lax.broadcasted_iota(jnp.int32, q.shape, 1)
    sels = [(lane >= h * hd) & (lane < (h + 1) * hd) for h in range(heads)]
    q_heads = jnp.concatenate([jnp.where(sel, q, jnp.zeros_like(q)) for sel in sels], axis=0)
    s = _dot_nt(q_heads, mk) * (hd ** -0.5)
    p = jnp.exp(s - jnp.max(s, axis=-1, keepdims=True))
    p = p / jnp.sum(p, axis=-1, keepdims=True)
    pv = _dot(p.astype(BF16), mv)
    out = jnp.zeros(q.shape, F32)
    for h, sel in enumerate(sels):
        out = out + jnp.where(sel, pv[h * r:(h + 1) * r], 0.0)
    return out


def _x_sample_kernel(q_ref, kt_ref, vt_ref, o_ref, *, group, heads, hd):
    tp = q_ref.shape[1]
    lane = lax.broadcasted_iota(jnp.int32, q_ref.shape[1:], 1)
    sels = [(lane >= h * hd) & (lane < (h + 1) * hd) for h in range(heads)]
    def scores(g):
        q_heads = jnp.concatenate([jnp.where(sel, q_ref[g], 0.0) for sel in sels], axis=0).astype(BF16)
        return _dot(q_heads, kt_ref[g].astype(BF16)) * (hd ** -0.5)

    s_next = scores(0)
    for g in range(group):
        q = q_ref[g]
        s = s_next
        if g + 1 < group:
            s_next = scores(g + 1)
        p = jnp.exp(s - jnp.max(s, axis=-1, keepdims=True))
        p = p / jnp.sum(p, axis=-1, keepdims=True)
        r = _dot_nt(p.astype(BF16), vt_ref[g].astype(BF16))
        out = jnp.zeros(q.shape, F32)
        for h, sel in enumerate(sels):
            out = out + jnp.where(sel, r[h * tp:(h + 1) * tp], 0.0)
        o_ref[g] = out


def _x_sample(xq, mkt, mvt, heads):
    b, tp, c = xq.shape
    m = mkt.shape[2]
    G = X_SAMPLE_GROUP
    kern = functools.partial(_x_sample_kernel, group=G, heads=heads, hd=c // heads)
    return pl.pallas_call(
        kern,
        grid=(b // G,),
        in_specs=[pl.BlockSpec((G, tp, c), lambda i: (i, 0, 0)),
                  pl.BlockSpec((G, c, m), lambda i: (i, 0, 0)),
                  pl.BlockSpec((G, c, m), lambda i: (i, 0, 0))],
        out_specs=pl.BlockSpec((G, tp, c), lambda i: (i, 0, 0)),
        out_shape=jax.ShapeDtypeStruct((b, tp, c), F32),
        compiler_params=_params(1),
        name="x_attend_sample",
    )(xq, mkt, mvt)


def _merge_kernel(x_ref, gpre, wgate, oret, wret, omla, wmla, wuv, ox, wx, wout, gpost, *rest,
                  heads, kv_lora, d_v, absorbed, x_heads):
    h_o = rest[-1]
    if x_heads:
        mk, mv = rest[:2]
        o_x = _x_attend_heads(ox[...], mk[0].astype(BF16), mv[0].astype(BF16), x_heads, ox.shape[1] // x_heads)
    else:
        o_x = ox[...]
    x = x_ref[...]
    d = x.shape[1]
    u = _rms(x, gpre[...]).astype(BF16)
    a_ret = _dot(oret[...], wret[...])
    if absorbed:
        a_mla = jnp.zeros(x.shape, F32)
        for h in range(heads):
            o_h = _dot(omla[:, h * kv_lora:(h + 1) * kv_lora].astype(BF16), wuv[h]).astype(BF16)
            a_mla = a_mla + _dot(o_h, wmla[h * d_v:(h + 1) * d_v, :])
    else:
        a_mla = _dot(omla[...], wmla[...])
    a_x = _dot(o_x.astype(BF16), wx[...])
    mixed = jnp.zeros(x.shape, F32)
    for i, a in enumerate((a_ret, a_mla, a_x)):
        mixed = mixed + _sigmoid(_dot_nt(u, wgate[i * d:(i + 1) * d, :])) * a
    h_o[...] = x + _rms(_dot(mixed.astype(BF16), wout[...]), gpost[...])


def _merge(x, oret, omla, ox, w, dims, *, absorbed, tm, mem=None):
    n, d = x.shape
    row = lambda a: pl.BlockSpec((tm, a.shape[1]), lambda i: (i, 0))
    res = lambda a: _resident(a.shape)
    args = [x, w["g_pre"], w["w_gates"], oret, w["w_ret_o"], omla, w["w_mla_o"], w["w_uv_h"], ox, w["w_x_o"],
            w["w_out"], w["g_post"]]
    specs = [row(x), res(w["g_pre"]), res(w["w_gates"]), row(oret), res(w["w_ret_o"]), row(omla),
             res(w["w_mla_o"]), res(w["w_uv_h"]), row(ox), res(w["w_x_o"]), res(w["w_out"]), res(w["g_post"])]
    x_heads = 0
    if mem is not None:
        mk, mv, x_heads, seq_rows = mem
        tiles_per_seq = seq_rows // tm
        per_seq = pl.BlockSpec((1,) + mk.shape[1:], lambda i: (i // tiles_per_seq, 0, 0))
        args += [mk, mv]
        specs += [per_seq, per_seq]
    kern = functools.partial(_merge_kernel, heads=dims["heads"], kv_lora=dims["kv_lora"], d_v=dims["d_v"],
                             absorbed=absorbed, x_heads=x_heads)
    return pl.pallas_call(
        kern,
        grid=(n // tm,),
        in_specs=specs,
        out_specs=pl.BlockSpec((tm, d), lambda i: (i, 0)),
        out_shape=jax.ShapeDtypeStruct((n, d), F32),
        compiler_params=_params(1),
        name="merge_sample" if absorbed else "merge_prompt",
    )(*args)


def _ffn_kernel(h_ref, gpre, wg, wu, wd, gpost, y_o):
    h = h_ref[...]
    f = _rms(h, gpre[...]).astype(BF16)
    gate = _dot(f, wg[...])
    act = (gate * _sigmoid(gate) * _dot(f, wu[...])).astype(BF16)
    y_o[...] = h + _rms(_dot(act, wd[...]), gpost[...])


def _ffn(h, w, *, tm):
    n, d = h.shape
    res = lambda a: _resident(a.shape)
    return pl.pallas_call(
        _ffn_kernel,
        grid=(n // tm,),
        in_specs=[pl.BlockSpec((tm, d), lambda i: (i, 0)), res(w["g_ffn_pre"]), res(w["w_ffn_gate"]),
                  res(w["w_ffn_up"]), res(w["w_ffn_down"]), res(w["g_ffn_post"])],
        out_specs=pl.BlockSpec((tm, d), lambda i: (i, 0)),
        out_shape=jax.ShapeDtypeStruct((n, d), F32),
        compiler_params=_params(1),
        name="ffn",
    )(h, w["g_ffn_pre"], w["w_ffn_gate"], w["w_ffn_up"], w["w_ffn_down"], w["g_ffn_post"])


def _rope_tables(pos, ret_dk, d_rope):
    posf = pos.astype(F32)[:, None]

    def angles(half):
        inv = ROPE_BASE ** (-jnp.arange(half, dtype=F32) / half)
        ang = posf * inv[None, :]
        return jnp.cos(ang), jnp.sin(ang)

    cr, sr = angles(ret_dk // 2)
    cp, sp = angles(d_rope // 2)
    z = jnp.zeros_like(cp)
    pad = jnp.zeros((pos.shape[0], LANES - d_rope), F32)
    return {
        "cosr": jnp.concatenate([cr, cr], axis=1),
        "sinr": jnp.concatenate([-sr, sr], axis=1),
        "cosp": jnp.concatenate([cp, cp, pad], axis=1),
        "sinlo": jnp.concatenate([-sp, z, pad], axis=1),
        "sinhi": jnp.concatenate([z, sp, pad], axis=1),
    }


def _layer_weights(l, dims, sizes, norm_mix_pre, norm_mix_post, norm_ffn_pre, norm_ffn_post, norm_mem, norm_q_lat,
                   norm_kv_lat, w_in, w_uq, w_uk, w_uv, w_mem_k, w_mem_v, w_ret_o, w_mla_o, w_x_o, w_out,
                   w_ffn_gate, w_ffn_up, w_ffn_down):
    heads, d_nope, d_rope, kv_lora, d_v = (dims[k] for k in ("heads", "d_nope", "d_rope", "kv_lora", "d_v"))
    bf = lambda a: a.astype(BF16)
    gain = lambda a: a[l].astype(F32)[None, :]
    offs = np.concatenate([[0], np.cumsum(sizes)])
    w_in_t = bf(jnp.swapaxes(w_in[l], 0, 1))
    seg = [w_in_t[offs[i]:offs[i + 1], :] for i in range(len(sizes))]
    q_lora = w_uq.shape[1]
    uq = w_uq[l].reshape(q_lora, heads, d_nope + d_rope)
    uq_p = jnp.pad(uq[:, :, d_nope:], ((0, 0), (0, 0), (0, LANES - d_rope)))
    return {
        "g_pre": gain(norm_mix_pre), "g_post": gain(norm_mix_post), "g_ffn_pre": gain(norm_ffn_pre),
        "g_ffn_post": gain(norm_ffn_post), "g_mem": gain(norm_mem), "g_q": gain(norm_q_lat),
        "g_kv": gain(norm_kv_lat),
        "w_rq": seg[0], "w_rk": seg[1], "w_rv": seg[2], "w_rg": seg[3], "w_cq": seg[4],
        "w_ckv": seg[5], "w_kpe": jnp.pad(seg[6], ((0, LANES - d_rope), (0, 0))), "w_xq": seg[7],
        "w_gates": seg[8],
        "w_uq_n": bf(uq[:, :, :d_nope].reshape(q_lora, heads * d_nope)),
        "w_uq_p": bf(uq_p.reshape(q_lora, heads * LANES)),
        "w_uk_t": bf(jnp.swapaxes(w_uk[l], 1, 2)),
        "w_kn": bf(jnp.swapaxes(w_uk[l], 0, 1).reshape(kv_lora, heads * d_nope)),
        "w_vn": bf(jnp.swapaxes(w_uv[l], 0, 1).reshape(kv_lora, heads * d_v)),
        "w_uv_h": bf(w_uv[l]),
        "w_mem_k": bf(w_mem_k[l]), "w_mem_v": bf(w_mem_v[l]),
        "w_ret_o": bf(w_ret_o[l]), "w_mla_o": bf(w_mla_o[l]), "w_x_o": bf(w_x_o[l]), "w_out": bf(w_out[l]),
        "w_ffn_gate": bf(w_ffn_gate[l]), "w_ffn_up": bf(w_ffn_up[l]), "w_ffn_down": bf(w_ffn_down[l]),
    }


def _pad_tokens(a, b, tokens, rows=SAMPLE_TOK_PAD):
    a = a.reshape(b, tokens, a.shape[-1])
    return jnp.pad(a, ((0, 0), (0, rows - tokens), (0, 0)))


def kernel(x_prompt, x_sample, mem_prompt, cache_ckv, cache_kpe, page_table, state_ret, cache_mem_k, cache_mem_v,
           norm_mix_pre, norm_mix_post, norm_ffn_pre, norm_ffn_post, norm_mem, norm_q_lat, norm_kv_lat, w_in,
           w_uq, w_uk, w_uv, w_mem_k, w_mem_v, w_ret_o, w_mla_o, w_x_o, w_out, w_ffn_gate, w_ffn_up, w_ffn_down):
    depth = w_in.shape[0]
    batch, seq, d_model = x_prompt.shape
    db, tokens, _ = x_sample.shape
    ret_heads, ret_dk, ret_dv = state_ret.shape[2:]
    heads, kv_lora, d_nope = w_uk.shape[1:]
    d_rope = cache_kpe.shape[-1]
    d_v = w_uv.shape[-1]
    n_mem, x_heads, x_hd = cache_mem_k.shape[2:]
    q_lora = w_uq.shape[1]
    past_len = page_table.shape[1] * cache_ckv.shape[2]
    dims = dict(heads=heads, d_nope=d_nope, d_rope=d_rope, kv_lora=kv_lora, d_v=d_v,
                ret_heads=ret_heads, ret_dk=ret_dk, ret_dv=ret_dv)
    assert ret_dk == LANES and d_nope == LANES and d_rope <= LANES // 2
    assert tokens <= min(SAMPLE_TOK_PAD, X_SAMPLE_TOK_PAD)
    sizes = (ret_heads * ret_dk, ret_heads * ret_dk, ret_heads * ret_dv, ret_heads * ret_dv, q_lora, kv_lora,
             d_rope, x_heads * x_hd, w_in.shape[2] - (2 * ret_heads * ret_dk + 2 * ret_heads * ret_dv + q_lora
                                                       + kv_lora + d_rope + x_heads * x_hd))

    cache_kpe_t = jnp.swapaxes(cache_kpe, 2, 3)

    tabs_p = _rope_tables(jnp.arange(seq), ret_dk, d_rope)
    n_s = db * tokens
    tabs_s = _rope_tables(past_len + (jnp.arange(n_s) % tokens), ret_dk, d_rope)

    y_p = x_prompt.reshape(batch * seq, d_model)
    y_s = x_sample.reshape(n_s, d_model)
    outs = [[] for _ in range(8)]
    for l in range(depth):
        w = _layer_weights(l, dims, sizes, norm_mix_pre, norm_mix_post, norm_ffn_pre, norm_ffn_post, norm_mem,
                           norm_q_lat, norm_kv_lat, w_in, w_uq, w_uk, w_uv, w_mem_k, w_mem_v, w_ret_o, w_mla_o,
                           w_x_o, w_out, w_ffn_gate, w_ffn_up, w_ffn_down)

        mk_p, mv_p = _mem_kv(mem_prompt, w["g_mem"], w["w_mem_k"], w["w_mem_v"])
        rq, rk, rv, rg, qcat, xq, ckv_p, kpe_p, kcat, vn = _in_proj(y_p, w, tabs_p, dims, absorbed=False,
                                                                    tm=ROW_TILE)
        o_ret, ret_p = _ret_prompt(rq, rk, rv, rg, batch, seq, dims)
        o_mla = _mla_prompt(qcat, kcat, vn, batch, seq, dims)
        h_p = _merge(y_p, o_ret, o_mla, xq, w, dims, absorbed=False, tm=ROW_TILE, mem=(mk_p, mv_p, x_heads, seq))
        y_p = _ffn(h_p, w, tm=ROW_TILE)

        rq, rk, rv, rg, qcat, xq, ckv_s, kpe_s, qlat = _in_proj(y_s, w, tabs_s, dims, absorbed=True, tm=ROW_TILE)
        pad = lambda a: _pad_tokens(a, db, tokens)
        o_ret, ret_s = _ret_sample(pad(rq), pad(rk), pad(rv), pad(rg), state_ret[l].astype(F32), tokens, dims)
        o_ret = o_ret[:, :tokens].reshape(n_s, ret_heads * ret_dv)
        group = d_nope + LANES
        q_pe = qcat.reshape(db, tokens * heads, group)[:, :, d_nope:d_nope + d_rope]
        o_lat = _mla_sample(qlat.reshape(db, tokens * heads, kv_lora), q_pe, pad(ckv_s), pad(kpe_s),
                            cache_ckv, cache_kpe_t, l, page_table, tokens, dims)
        o_lat = o_lat.reshape(n_s, heads * kv_lora)
        mem_t = lambda c: jnp.transpose(c[l], (0, 2, 3, 1)).reshape(db, x_heads * x_hd, n_mem)
        o_x = _x_sample(_pad_tokens(xq.astype(F32), db, tokens, rows=X_SAMPLE_TOK_PAD), mem_t(cache_mem_k),
                        mem_t(cache_mem_v), x_heads)
        o_x = o_x[:, :tokens].reshape(n_s, x_heads * x_hd)
        h_s = _merge(y_s, o_ret, o_lat, o_x, w, dims, absorbed=True, tm=ROW_TILE)
        y_s = _ffn(h_s, w, tm=ROW_TILE)

        for lst, val in zip(outs, (ckv_p.reshape(batch, seq, kv_lora), kpe_p.reshape(batch, seq, d_rope),
                                   ckv_s.reshape(db, tokens, kv_lora), kpe_s.reshape(db, tokens, d_rope),
                                   ret_p.astype(x_prompt.dtype), ret_s.astype(state_ret.dtype),
                                   mk_p.reshape(batch, n_mem, x_heads, x_hd),
                                   mv_p.reshape(batch, n_mem, x_heads, x_hd))):
            lst.append(val)

    return (y_p.reshape(batch, seq, d_model), y_s.reshape(db, tokens, d_model)) + tuple(jnp.stack(o) for o in outs)
```

```python
import functools
import math

import numpy as np
import jax
import jax.numpy as jnp
from jax import lax
from jax.experimental import pallas as pl
from jax.experimental.pallas import tpu as pltpu

F32 = jnp.float32
BF16 = jnp.bfloat16

ROPE_BASE = 10000.0
RMS_EPS = 1e-6
LANES = 128
VMEM_LIMIT = 56 * 1024 * 1024
NEG_BIG = -1e30

ROW_TILE = 512
RET_CHUNK = 256
RET_SEQS_PER_STEP = 4
ATTN_Q_TILE = 512
ATTN_KV_TILE = 512
ATTN_HEADS_PER_STEP = 8
SAMPLE_TOK_PAD = 16
X_SAMPLE_TOK_PAD = 8
RET_SAMPLE_GROUP = 16
X_SAMPLE_GROUP = 16
DECODE_CHUNK = 4096
DMA_LOOP_UNROLL = 4


def _resident(shape):
    nd = len(shape)
    return pl.BlockSpec(shape, lambda *_: (0,) * nd, pipeline_mode=pl.Buffered(1))


def _params(n_axes):
    return pltpu.CompilerParams(dimension_semantics=("arbitrary",) * n_axes, vmem_limit_bytes=VMEM_LIMIT)


def _rms(x, g=None):
    y = x * lax.rsqrt(jnp.mean(x * x, axis=-1, keepdims=True) + RMS_EPS)
    return y if g is None else y * g


def _sigmoid(x):
    return 1.0 / (1.0 + jnp.exp(-x))


def _dot(a, b):
    return jnp.dot(a, b, preferred_element_type=F32)


def _dot_nt(a, b):
    return lax.dot_general(a, b, (((1,), (1,)), ((), ())), preferred_element_type=F32)


def _dot_tn(a, b):
    return lax.dot_general(a, b, (((0,), (0,)), ((), ())), preferred_element_type=F32)


def _rope_half_vreg(z, cos_t, sin_lo, sin_hi, quarter):
    return (z * cos_t + pltpu.roll(z, LANES - quarter, 1) * sin_lo + pltpu.roll(z, quarter, 1) * sin_hi)


def _in_proj_kernel(x_ref, g_ref, wrq, wrk, wrv, wrg, wcq, wckv, wkpe, wxq, gq_ref, gkv_ref, wuqn, wuqp,
                    cosr, sinr, cosp, sinlo, sinhi, *rest,
                    ret_heads, ret_dk, heads, d_nope, d_rope, kv_lora, absorbed, q_scale):
    if absorbed:
        wa, (rq_o, rk_o, rv_o, rg_o, qcat_o, xq_o, ckv_o, kpe_o, a_o) = rest[0], rest[1:]
    else:
        wa, wb, (rq_o, rk_o, rv_o, rg_o, qcat_o, xq_o, ckv_o, kpe_o, a_o, b_o) = rest[0], rest[1], rest[2:]
    u = _rms(x_ref[...], g_ref[...]).astype(BF16)
    cr, sr = cosr[...], sinr[...]
    cp, slo, shi = cosp[...], sinlo[...], sinhi[...]
    group = d_nope + LANES
    k_scale = ret_dk ** -0.5

    cq = _dot_nt(u, wcq[...])
    ckv = _dot_nt(u, wckv[...])
    kpe_raw = _dot_nt(u, wkpe[...])
    zq = _dot_nt(u, wrq[...])
    zk = _dot_nt(u, wrk[...])

    cqn = _rms(cq, gq_ref[...]).astype(BF16)
    ckvn = _rms(ckv, gkv_ref[...])
    ckv_o[...] = ckvn
    kpe = _rope_half_vreg(kpe_raw, cp, slo, shi, d_rope // 2)
    if absorbed:
        kpe_o[...] = kpe[:, :d_rope]
    else:
        kpe_o[0] = kpe.T[:d_rope, :]

    qn = _dot(cqn, wuqn[...])
    qp = _dot(cqn, wuqp[...])
    if not absorbed:
        ckvb = ckvn.astype(BF16)
        kn = _dot(ckvb, wa[...])
        vn = _dot(ckvb, wb[...])

    for h in range(ret_heads):
        sl = slice(h * ret_dk, (h + 1) * ret_dk)
        q_h = zq[:, sl]
        k_h = zk[:, sl]
        rq_o[:, sl] = (q_h * cr + pltpu.roll(q_h, ret_dk // 2, 1) * sr).astype(BF16)
        rk_o[:, sl] = ((k_h * cr + pltpu.roll(k_h, ret_dk // 2, 1) * sr) * k_scale).astype(BF16)

    rv = _dot_nt(u, wrv[...])
    rg = _dot_nt(u, wrg[...])
    xq = _dot_nt(u, wxq[...])

    for h in range(heads):
        qn_h = qn[:, h * d_nope:(h + 1) * d_nope]
        qp_h = _rope_half_vreg(qp[:, h * LANES:(h + 1) * LANES], cp, slo, shi, d_rope // 2)
        if absorbed:
            a_o[:, h * kv_lora:(h + 1) * kv_lora] = _dot(qn_h.astype(BF16), wa[h]).astype(BF16)
        else:
            qn_h = qn_h * q_scale
            qp_h = qp_h * q_scale
        qcat_o[:, h * group:h * group + d_nope] = qn_h.astype(BF16)
        qcat_o[:, h * group + d_nope:(h + 1) * group] = qp_h.astype(BF16)
    if not absorbed:
        kpeb = kpe.astype(BF16)
        for h in range(heads):
            a_o[:, h * group:h * group + d_nope] = kn[:, h * d_nope:(h + 1) * d_nope].astype(BF16)
            a_o[:, h * group + d_nope:(h + 1) * group] = kpeb
        b_o[...] = vn.astype(BF16)

    rv_o[...] = rv.astype(BF16)
    rg_o[...] = rg.astype(BF16)
    xq_o[...] = xq.astype(BF16)


def _in_proj(x, w, tabs, dims, *, absorbed, tm):
    n, d_model = x.shape
    heads, d_nope, d_rope, kv_lora = dims["heads"], dims["d_nope"], dims["d_rope"], dims["kv_lora"]
    ret_heads, ret_dk, ret_dv = dims["ret_heads"], dims["ret_dk"], dims["ret_dv"]
    group = d_nope + LANES
    tab_rows = tabs["cosr"].shape[0]
    tab_tiles = tab_rows // tm
    row = lambda c: pl.BlockSpec((tm, c), lambda i: (i, 0))
    tab = lambda: pl.BlockSpec((tm, LANES), lambda i: (i % tab_tiles, 0))
    weights = [w["g_pre"], w["w_rq"], w["w_rk"], w["w_rv"], w["w_rg"], w["w_cq"], w["w_ckv"], w["w_kpe"],
               w["w_xq"], w["g_q"], w["g_kv"], w["w_uq_n"], w["w_uq_p"]]
    out_cols = [(ret_heads * ret_dk, BF16), (ret_heads * ret_dk, BF16), (ret_heads * ret_dv, BF16),
                (ret_heads * ret_dv, BF16), (heads * group, BF16), (w["w_xq"].shape[0], BF16), (kv_lora, F32),
                (d_rope, F32)]
    if absorbed:
        mode_weights = [w["w_uk_t"]]
        out_cols += [(heads * kv_lora, BF16)]
    else:
        mode_weights = [w["w_kn"], w["w_vn"]]
        out_cols += [(heads * group, BF16), (w["w_vn"].shape[1], BF16)]
    kern = functools.partial(_in_proj_kernel, ret_heads=ret_heads, ret_dk=ret_dk, heads=heads, d_nope=d_nope,
                             d_rope=d_rope, kv_lora=kv_lora, absorbed=absorbed,
                             q_scale=(d_nope + d_rope) ** -0.5 * math.log2(math.e))
    out_specs = [row(c) for c, _ in out_cols]
    out_shape = [jax.ShapeDtypeStruct((n, c), dt) for c, dt in out_cols]
    if not absorbed:
        kpe_slot = 7
        out_specs[kpe_slot] = pl.BlockSpec((1, d_rope, tm), lambda i: (i // tab_tiles, 0, i % tab_tiles))
        out_shape[kpe_slot] = jax.ShapeDtypeStruct((n // tab_rows, d_rope, tab_rows), F32)
    return pl.pallas_call(
        kern,
        grid=(n // tm,),
        in_specs=([row(d_model)] + [_resident(a.shape) for a in weights] + [tab() for _ in range(5)]
                  + [_resident(a.shape) for a in mode_weights]),
        out_specs=out_specs,
        out_shape=out_shape,
        compiler_params=_params(1),
        name="in_proj_sample" if absorbed else "in_proj_prompt",
    )(x, *weights, tabs["cosr"], tabs["sinr"], tabs["cosp"], tabs["sinlo"], tabs["sinhi"], *mode_weights)


def _ret_head(q, k, v, g, s, dec, qd, kd, g_l):
    inner = _dot_nt(q, k) * dec
    o = _dot(inner.astype(BF16), v) + _dot((q.astype(F32) * qd).astype(BF16), s.astype(BF16))
    s_new = s * g_l + _dot_tn((k.astype(F32) * kd).astype(BF16), v)
    gf = g.astype(F32)
    o = (gf * _sigmoid(gf)) * _rms(o)
    return o.astype(BF16), s_new


def _ret_prompt_kernel(rq, rk, rv, rg, dec, qd, kd, o_ref, s_out, s_scr, *, heads, dk, dv, g_l):
    c = pl.program_id(1)

    @pl.when(c == 0)
    def _():
        s_scr[...] = jnp.zeros(s_scr.shape, F32)

    for b in range(rq.shape[0]):
        for h in range(heads):
            o, s_new = _ret_head(rq[b, :, h * dk:(h + 1) * dk], rk[b, :, h * dk:(h + 1) * dk],
                                 rv[b, :, h * dv:(h + 1) * dv], rg[b, :, h * dv:(h + 1) * dv],
                                 s_scr[b, h], dec[h], qd[h], kd[h], g_l[h])
            o_ref[b, :, h * dv:(h + 1) * dv] = o
            s_scr[b, h] = s_new

    @pl.when(c == pl.num_programs(1) - 1)
    def _():
        s_out[...] = s_scr[...]


def _ret_consts(heads, length, dk, chunk_rows=None):
    rows = length if chunk_rows is None else chunk_rows
    lg = np.log1p(-np.exp2(-5.0 - np.arange(heads, dtype=np.float64)))
    i = np.arange(rows, dtype=np.float64)
    diff = i[:, None] - i[None, :]
    valid = (diff >= 0) & (i[:, None] < length) & (i[None, :] < length)
    dec = np.where(valid[None], np.exp(np.maximum(diff, 0.0)[None] * lg[:, None, None]), 0.0)
    qd = np.exp((i[None, :] + 1.0) * lg[:, None])
    kd = np.where(i[None, :] < length, np.exp((length - 1.0 - i[None, :]) * lg[:, None]), 0.0)
    qd = np.broadcast_to(qd[:, :, None], (heads, rows, dk))
    kd = np.broadcast_to(kd[:, :, None], (heads, rows, dk))
    g_l = tuple(float(v) for v in np.exp(length * lg))
    return (jnp.asarray(dec, F32), jnp.asarray(qd, F32), jnp.asarray(kd, F32), g_l)


def _ret_prompt(rq, rk, rv, rg, batch, seq, dims):
    heads, dk, dv = dims["ret_heads"], dims["ret_dk"], dims["ret_dv"]
    L = RET_CHUNK
    nc = seq // L
    dec, qd, kd, g_l = _ret_consts(heads, L, dk)
    nb = RET_SEQS_PER_STEP
    row = lambda c: pl.BlockSpec((nb, L, c), lambda b, i: (b, i, 0))
    per_seq = lambda a: a.reshape(batch, seq, a.shape[-1])
    kern = functools.partial(_ret_prompt_kernel, heads=heads, dk=dk, dv=dv, g_l=g_l)
    o, state = pl.pallas_call(
        kern,
        grid=(batch // nb, nc),
        in_specs=[row(heads * dk), row(heads * dk), row(heads * dv), row(heads * dv),
                  _resident(dec.shape), _resident(qd.shape), _resident(kd.shape)],
        out_specs=[row(heads * dv), pl.BlockSpec((nb, heads, dk, dv), lambda b, i: (b, 0, 0, 0))],
        out_shape=[jax.ShapeDtypeStruct((batch, seq, heads * dv), BF16),
                   jax.ShapeDtypeStruct((batch, heads, dk, dv), F32)],
        scratch_shapes=[pltpu.VMEM((nb, heads, dk, dv), F32)],
        compiler_params=_params(2),
        name="retention_prompt",
    )(per_seq(rq), per_seq(rk), per_seq(rv), per_seq(rg), dec, qd, kd)
    return o.reshape(batch * seq, heads * dv), state


def _ret_sample_kernel(rq, rk, rv, rg, s0, dec, qd, kd, o_ref, s_out, *, group, heads, dk, dv, g_l):
    pairs = [(i, h) for i in range(group) for h in range(heads)]

    def state_side(i, h):
        q = rq[i, :, h * dk:(h + 1) * dk]
        k = rk[i, :, h * dk:(h + 1) * dk]
        v = rv[i, :, h * dv:(h + 1) * dv]
        s = s0[i, h]
        read = _dot((q.astype(F32) * qd[h]).astype(BF16), s.astype(BF16))
        s_out[i, h] = s * g_l[h] + _dot_tn((k.astype(F32) * kd[h]).astype(BF16), v)
        return read

    def output_side(i, h, read):
        q = rq[i, :, h * dk:(h + 1) * dk]
        k = rk[i, :, h * dk:(h + 1) * dk]
        v = rv[i, :, h * dv:(h + 1) * dv]
        o = _dot((_dot_nt(q, k) * dec[h]).astype(BF16), v) + read
        gf = rg[i, :, h * dv:(h + 1) * dv].astype(F32)
        o_ref[i, :, h * dv:(h + 1) * dv] = ((gf * _sigmoid(gf)) * _rms(o)).astype(BF16)

    read = state_side(*pairs[0])
    for n, (i, h) in enumerate(pairs):
        nxt = state_side(*pairs[n + 1]) if n + 1 < len(pairs) else None
        output_side(i, h, read)
        read = nxt


def _ret_sample(rq, rk, rv, rg, state, tokens, dims):
    heads, dk, dv = dims["ret_heads"], dims["ret_dk"], dims["ret_dv"]
    b, tp, _ = rq.shape
    G = RET_SAMPLE_GROUP
    dec, qd, kd, g_l = _ret_consts(heads, tokens, dk, chunk_rows=tp)
    blk = lambda c: pl.BlockSpec((G, tp, c), lambda i: (i, 0, 0))
    st = pl.BlockSpec((G, heads, dk, dv), lambda i: (i, 0, 0, 0))
    kern = functools.partial(_ret_sample_kernel, group=G, heads=heads, dk=dk, dv=dv, g_l=g_l)
    return pl.pallas_call(
        kern,
        grid=(b // G,),
        in_specs=[blk(heads * dk), blk(heads * dk), blk(heads * dv), blk(heads * dv), st,
                  _resident(dec.shape), _resident(qd.shape), _resident(kd.shape)],
        out_specs=[blk(heads * dv), st],
        out_shape=[jax.ShapeDtypeStruct((b, tp, heads * dv), BF16),
                   jax.ShapeDtypeStruct((b, heads, dk, dv), F32)],
        compiler_params=_params(1),
        name="retention_sample",
    )(rq, rk, rv, rg, state, dec, qd, kd)


def _mla_prompt_kernel(q_ref, k_ref, v_ref, o_ref, m_scr, l_scr, acc_scr, *, tq, tk, hp, group, d_v):
    qi = pl.program_id(2)
    ri = lax.broadcasted_iota(jnp.int32, (tq, tk), 0)
    ci = lax.broadcasted_iota(jnp.int32, (tq, tk), 1)
    causal_bias = jnp.where(ci <= ri, 0.0, NEG_BIG)

    def block(j, carry, first):
        start = pl.multiple_of(j * tk, tk)

        def scores(h):
            return _dot_nt(q_ref[:, h * group:(h + 1) * group], k_ref[pl.ds(start, tk), h * group:(h + 1) * group])

        s_next = scores(0)
        for h in range(hp):
            s = s_next
            if h + 1 < hp:
                s_next = scores(h + 1)
            vals = v_ref[pl.ds(start, tk), h * d_v:(h + 1) * d_v]
            if first:
                s = s + causal_bias
                m_new = jnp.broadcast_to(jnp.max(s, axis=-1, keepdims=True), (tq, LANES))
                p = jnp.exp2(s - jnp.tile(m_new, (1, tk // LANES)))
                l_scr[h] = jnp.broadcast_to(jnp.sum(p, axis=-1, keepdims=True), (tq, LANES))
                acc_scr[h] = _dot(p.astype(BF16), vals)
            else:
                m_prev = m_scr[h]
                m_new = jnp.maximum(m_prev, jnp.max(s, axis=-1, keepdims=True))
                a = jnp.exp2(m_prev - m_new)
                p = jnp.exp2(s - jnp.tile(m_new, (1, tk // LANES)))
                l_scr[h] = a * l_scr[h] + jnp.sum(p, axis=-1, keepdims=True)
                acc_scr[h] = a * acc_scr[h] + _dot(p.astype(BF16), vals)
            m_scr[h] = m_new
        return carry

    block(qi, 0, True)
    lax.fori_loop(0, qi, functools.partial(block, first=False), 0)
    for h in range(hp):
        o_ref[:, h * d_v:(h + 1) * d_v] = (acc_scr[h] / l_scr[h]).astype(BF16)


def _mla_prompt(qcat, kcat, v, batch, seq, dims):
    heads, d_nope, d_v = dims["heads"], dims["d_nope"], dims["d_v"]
    group = d_nope + LANES
    t = ATTN_Q_TILE
    hp = ATTN_HEADS_PER_STEP
    nq = seq // t
    assert ATTN_KV_TILE == t and d_v == LANES
    kern = functools.partial(_mla_prompt_kernel, tq=t, tk=ATTN_KV_TILE, hp=hp, group=group, d_v=d_v)
    return pl.pallas_call(
        kern,
        grid=(batch, heads // hp, nq),
        in_specs=[pl.BlockSpec((t, hp * group), lambda b, h, i: (b * nq + i, h)),
                  pl.BlockSpec((seq, hp * group), lambda b, h, i: (b, h)),
                  pl.BlockSpec((seq, hp * d_v), lambda b, h, i: (b, h))],
        out_specs=pl.BlockSpec((t, hp * d_v), lambda b, h, i: (b * nq + i, h)),
        out_shape=jax.ShapeDtypeStruct((batch * seq, heads * d_v), BF16),
        scratch_shapes=[pltpu.VMEM((hp, t, LANES), F32), pltpu.VMEM((hp, t, LANES), F32),
                        pltpu.VMEM((hp, t, d_v), F32)],
        compiler_params=_params(3),
        name="mla_prompt",
    )(qcat, kcat, v)


def _mla_sample_kernel(pt_ref, ql_ref, qp_ref, cn_ref, kn_ref, ckv_hbm, kpe_hbm, o_ref,
                       ckv_buf, kpe_buf, kbf, s_scr, sem,
                       *, layer, n_pages, page, chunk, tokens, heads, scale):
    b = pl.program_id(0)
    past = n_pages * page
    d_rope = qp_ref.shape[2]

    def page_copy(seq, slot, p, array):
        pg = pt_ref[seq, p]
        if array == 0:
            off = pl.multiple_of(p * page, page)
            return pltpu.make_async_copy(ckv_hbm.at[layer, pg], ckv_buf.at[slot, pl.ds(off, page), :],
                                         sem.at[0, slot])
        off = pl.multiple_of(p * d_rope, d_rope)
        return pltpu.make_async_copy(kpe_hbm.at[layer, pg], kpe_buf.at[slot, pl.ds(off, d_rope), :],
                                     sem.at[1, slot])

    def for_each_page(seq, slot, fn, arrays=(0, 1), unroll=DMA_LOOP_UNROLL):
        def body(p, carry):
            for a in arrays:
                fn(page_copy(seq, slot, p, a))
            return carry
        lax.fori_loop(0, n_pages, body, 0, unroll=unroll)

    @pl.when(b == 0)
    def _():
        for_each_page(0, 0, lambda cp: cp.start())

    @pl.when(b + 1 < pl.num_programs(0))
    def _():
        for_each_page(b + 1, (b + 1) % 2, lambda cp: cp.start())

    slot = b % 2
    for_each_page(b, slot, lambda cp: cp.wait(), arrays=(0,), unroll=True)
    for_each_page(b, slot, lambda cp: cp.wait(), arrays=(1,), unroll=True)

    ql = ql_ref[0]
    qp = qp_ref[0]
    ckv_s = ckv_buf.at[slot]
    kpe_s = kpe_buf.at[slot]
    n_chunks = past // chunk

    cn = cn_ref[0].astype(BF16)
    kn = kn_ref[0].astype(BF16)
    s_new = (_dot_nt(ql, cn) + _dot_nt(qp, kn)) * scale
    t = lax.broadcasted_iota(jnp.int32, s_new.shape, 0) // heads
    col = lax.broadcasted_iota(jnp.int32, s_new.shape, 1)
    s_new = jnp.where((col <= t) & (col < tokens), s_new, NEG_BIG)
    m = jnp.max(s_new, axis=-1, keepdims=True)
    p_new = jnp.exp(s_new - m)
    l = jnp.sum(p_new, axis=-1, keepdims=True)
    acc = _dot(p_new.astype(BF16), cn)

    def scores(c):
        sl = slice(c * chunk, (c + 1) * chunk)
        kc = ckv_s[sl, :].astype(BF16)
        kbf[sl, :] = kc
        pe = jnp.concatenate([kpe_s[p * d_rope:(p + 1) * d_rope, :]
                              for p in range(c * chunk // page, (c + 1) * chunk // page)], axis=1)
        s_scr[:, sl] = (_dot_nt(ql, kc) + _dot(qp, pe.astype(BF16))) * scale

    def values(c, m, l, acc):
        sl = slice(c * chunk, (c + 1) * chunk)
        s = s_scr[:, sl]
        m_new = jnp.maximum(m, jnp.max(s, axis=-1, keepdims=True))
        a = jnp.exp(m - m_new)
        p = jnp.exp(s - m_new)
        l = a * l + jnp.sum(p, axis=-1, keepdims=True)
        return m_new, l, a * acc + _dot(p.astype(BF16), kbf[sl, :])

    scores(0)
    for c in range(n_chunks):
        if c + 1 < n_chunks:
            scores(c + 1)
        m, l, acc = values(c, m, l, acc)
    o_ref[0] = acc / l


def _mla_sample(ql, qp, ckv_new, kpe_new, cache_ckv, cache_kpe_t, layer, page_table, tokens, dims):
    heads, d_nope, d_rope, kv_lora = dims["heads"], dims["d_nope"], dims["d_rope"], dims["kv_lora"]
    b, rows, _ = ql.shape
    n_pages = page_table.shape[1]
    page = cache_ckv.shape[2]
    past = n_pages * page
    tp = ckv_new.shape[1]
    per_b = lambda r, c: pl.BlockSpec((1, r, c), lambda i, pt: (i, 0, 0))
    hbm = pl.BlockSpec(memory_space=pl.ANY)
    kern = functools.partial(_mla_sample_kernel, layer=layer, n_pages=n_pages, page=page, chunk=DECODE_CHUNK,
                             tokens=tokens, heads=heads, scale=(d_nope + d_rope) ** -0.5)
    grid_spec = pltpu.PrefetchScalarGridSpec(
        num_scalar_prefetch=1,
        grid=(b,),
        in_specs=[per_b(rows, kv_lora), per_b(rows, d_rope), per_b(tp, kv_lora), per_b(tp, d_rope), hbm, hbm],
        out_specs=per_b(rows, kv_lora),
        scratch_shapes=[pltpu.VMEM((2, past, kv_lora), F32), pltpu.VMEM((2, n_pages * d_rope, page), F32),
                        pltpu.VMEM((past, kv_lora), BF16), pltpu.VMEM((rows, past), F32),
                        pltpu.SemaphoreType.DMA((2, 2))],
    )
    return pl.pallas_call(
        kern,
        grid_spec=grid_spec,
        out_shape=jax.ShapeDtypeStruct((b, rows, kv_lora), F32),
        compiler_params=_params(1),
        name="mla_sample",
    )(page_table, ql, qp, ckv_new, kpe_new, cache_ckv, cache_kpe_t)


def _mem_kv_kernel(m_ref, g_ref, wk, wv, k_o, v_o):
    mn = _rms(m_ref[0], g_ref[...]).astype(BF16)
    k_o[0] = _dot(mn, wk[...])
    v_o[0] = _dot(mn, wv[...])


def _mem_kv(mem, g, wk, wv):
    b, m, d = mem.shape
    c = wk.shape[1]
    return pl.pallas_call(
        _mem_kv_kernel,
        grid=(b,),
        in_specs=[pl.BlockSpec((1, m, d), lambda i: (i, 0, 0)), _resident(g.shape), _resident(wk.shape),
                  _resident(wv.shape)],
        out_specs=[pl.BlockSpec((1, m, c), lambda i: (i, 0, 0))] * 2,
        out_shape=[jax.ShapeDtypeStruct((b, m, c), F32)] * 2,
        compiler_params=_params(1),
        name="mem_kv",
    )(mem, g, wk, wv)


def _x_attend_heads(q, mk, mv, heads, hd):
    r = q.shape[0]
    lane = lax.broadcasted_iota(jnp.int32, q.shape, 1)
    sels = [(lane >= h * hd) & (lane < (h + 1) * hd) for h in range(heads)]
    q_heads = jnp.concatenate([jnp.where(sel, q, jnp.zeros_like(q)) for sel in sels], axis=0)
    s = _dot_nt(q_heads, mk) * (hd ** -0.5)
    p = jnp.exp(s - jnp.max(s, axis=-1, keepdims=True))
    p = p / jnp.sum(p, axis=-1, keepdims=True)
    pv = _dot(p.astype(BF16), mv)
    out = jnp.zeros(q.shape, F32)
    for h, sel in enumerate(sels):
        out = out + jnp.where(sel, pv[h * r:(h + 1) * r], 0.0)
    return out


def _x_sample_kernel(q_ref, kt_ref, vt_ref, o_ref, *, group, heads, hd):
    tp = q_ref.shape[1]
    lane = lax.broadcasted_iota(jnp.int32, q_ref.shape[1:], 1)
    sels = [(lane >= h * hd) & (lane < (h + 1) * hd) for h in range(heads)]
    def scores(g):
        q_heads = jnp.concatenate([jnp.where(sel, q_ref[g], 0.0) for sel in sels], axis=0).astype(BF16)
        return _dot(q_heads, kt_ref[g].astype(BF16)) * (hd ** -0.5)

    s_next = scores(0)
    for g in range(group):
        q = q_ref[g]
        s = s_next
        if g + 1 < group:
            s_next = scores(g + 1)
        p = jnp.exp(s - jnp.max(s, axis=-1, keepdims=True))
        p = p / jnp.sum(p, axis=-1, keepdims=True)
        r = _dot_nt(p.astype(BF16), vt_ref[g].astype(BF16))
        out = jnp.zeros(q.shape, F32)
        for h, sel in enumerate(sels):
            out = out + jnp.where(sel, r[h * tp:(h + 1) * tp], 0.0)
        o_ref[g] = out


def _x_sample(xq, mkt, mvt, heads):
    b, tp, c = xq.shape
    m = mkt.shape[2]
    G = X_SAMPLE_GROUP
    kern = functools.partial(_x_sample_kernel, group=G, heads=heads, hd=c // heads)
    return pl.pallas_call(
        kern,
        grid=(b // G,),
        in_specs=[pl.BlockSpec((G, tp, c), lambda i: (i, 0, 0)),
                  pl.BlockSpec((G, c, m), lambda i: (i, 0, 0)),
                  pl.BlockSpec((G, c, m), lambda i: (i, 0, 0))],
        out_specs=pl.BlockSpec((G, tp, c), lambda i: (i, 0, 0)),
        out_shape=jax.ShapeDtypeStruct((b, tp, c), F32),
        compiler_params=_params(1),
        name="x_attend_sample",
    )(xq, mkt, mvt)


def _merge_kernel(x_ref, gpre, wgate, oret, wret, omla, wmla, wuv, ox, wx, wout, gpost, *rest,
                  heads, kv_lora, d_v, absorbed, x_heads):
    h_o = rest[-1]
    if x_heads:
        mk, mv = rest[:2]
        o_x = _x_attend_heads(ox[...], mk[0].astype(BF16), mv[0].astype(BF16), x_heads, ox.shape[1] // x_heads)
    else:
        o_x = ox[...]
    x = x_ref[...]
    d = x.shape[1]
    u = _rms(x, gpre[...]).astype(BF16)
    a_ret = _dot(oret[...], wret[...])
    if absorbed:
        a_mla = jnp.zeros(x.shape, F32)
        for h in range(heads):
            o_h = _dot(omla[:, h * kv_lora:(h + 1) * kv_lora].astype(BF16), wuv[h]).astype(BF16)
            a_mla = a_mla + _dot(o_h, wmla[h * d_v:(h + 1) * d_v, :])
    else:
        a_mla = _dot(omla[...], wmla[...])
    a_x = _dot(o_x.astype(BF16), wx[...])
    mixed = jnp.zeros(x.shape, F32)
    for i, a in enumerate((a_ret, a_mla, a_x)):
        mixed = mixed + _sigmoid(_dot_nt(u, wgate[i * d:(i + 1) * d, :])) * a
    h_o[...] = x + _rms(_dot(mixed.astype(BF16), wout[...]), gpost[...])


def _merge(x, oret, omla, ox, w, dims, *, absorbed, tm, mem=None):
    n, d = x.shape
    row = lambda a: pl.BlockSpec((tm, a.shape[1]), lambda i: (i, 0))
    res = lambda a: _resident(a.shape)
    args = [x, w["g_pre"], w["w_gates"], oret, w["w_ret_o"], omla, w["w_mla_o"], w["w_uv_h"], ox, w["w_x_o"],
            w["w_out"], w["g_post"]]
    specs = [row(x), res(w["g_pre"]), res(w["w_gates"]), row(oret), res(w["w_ret_o"]), row(omla),
             res(w["w_mla_o"]), res(w["w_uv_h"]), row(ox), res(w["w_x_o"]), res(w["w_out"]), res(w["g_post"])]
    x_heads = 0
    if mem is not None:
        mk, mv, x_heads, seq_rows = mem
        tiles_per_seq = seq_rows // tm
        per_seq = pl.BlockSpec((1,) + mk.shape[1:], lambda i: (i // tiles_per_seq, 0, 0))
        args += [mk, mv]
        specs += [per_seq, per_seq]
    kern = functools.partial(_merge_kernel, heads=dims["heads"], kv_lora=dims["kv_lora"], d_v=dims["d_v"],
                             absorbed=absorbed, x_heads=x_heads)
    return pl.pallas_call(
        kern,
        grid=(n // tm,),
        in_specs=specs,
        out_specs=pl.BlockSpec((tm, d), lambda i: (i, 0)),
        out_shape=jax.ShapeDtypeStruct((n, d), F32),
        compiler_params=_params(1),
        name="merge_sample" if absorbed else "merge_prompt",
    )(*args)


def _ffn_kernel(h_ref, gpre, wg, wu, wd, gpost, y_o):
    h = h_ref[...]
    f = _rms(h, gpre[...]).astype(BF16)
    gate = _dot(f, wg[...])
    act = (gate * _sigmoid(gate) * _dot(f, wu[...])).astype(BF16)
    y_o[...] = h + _rms(_dot(act, wd[...]), gpost[...])


def _ffn(h, w, *, tm):
    n, d = h.shape
    res = lambda a: _resident(a.shape)
    return pl.pallas_call(
        _ffn_kernel,
        grid=(n // tm,),
        in_specs=[pl.BlockSpec((tm, d), lambda i: (i, 0)), res(w["g_ffn_pre"]), res(w["w_ffn_gate"]),
                  res(w["w_ffn_up"]), res(w["w_ffn_down"]), res(w["g_ffn_post"])],
        out_specs=pl.BlockSpec((tm, d), lambda i: (i, 0)),
        out_shape=jax.ShapeDtypeStruct((n, d), F32),
        compiler_params=_params(1),
        name="ffn",
    )(h, w["g_ffn_pre"], w["w_ffn_gate"], w["w_ffn_up"], w["w_ffn_down"], w["g_ffn_post"])


def _rope_tables(pos, ret_dk, d_rope):
    posf = pos.astype(F32)[:, None]

    def angles(half):
        inv = ROPE_BASE ** (-jnp.arange(half, dtype=F32) / half)
        ang = posf * inv[None, :]
        return jnp.cos(ang), jnp.sin(ang)

    cr, sr = angles(ret_dk // 2)
    cp, sp = angles(d_rope // 2)
    z = jnp.zeros_like(cp)
    pad = jnp.zeros((pos.shape[0], LANES - d_rope), F32)
    return {
        "cosr": jnp.concatenate([cr, cr], axis=1),
        "sinr": jnp.concatenate([-sr, sr], axis=1),
        "cosp": jnp.concatenate([cp, cp, pad], axis=1),
        "sinlo": jnp.concatenate([-sp, z, pad], axis=1),
        "sinhi": jnp.concatenate([z, sp, pad], axis=1),
    }


def _layer_weights(l, dims, sizes, norm_mix_pre, norm_mix_post, norm_ffn_pre, norm_ffn_post, norm_mem, norm_q_lat,
                   norm_kv_lat, w_in, w_uq, w_uk, w_uv, w_mem_k, w_mem_v, w_ret_o, w_mla_o, w_x_o, w_out,
                   w_ffn_gate, w_ffn_up, w_ffn_down):
    heads, d_nope, d_rope, kv_lora, d_v = (dims[k] for k in ("heads", "d_nope", "d_rope", "kv_lora", "d_v"))
    bf = lambda a: a.astype(BF16)
    gain = lambda a: a[l].astype(F32)[None, :]
    offs = np.concatenate([[0], np.cumsum(sizes)])
    w_in_t = bf(jnp.swapaxes(w_in[l], 0, 1))
    seg = [w_in_t[offs[i]:offs[i + 1], :] for i in range(len(sizes))]
    q_lora = w_uq.shape[1]
    uq = w_uq[l].reshape(q_lora, heads, d_nope + d_rope)
    uq_p = jnp.pad(uq[:, :, d_nope:], ((0, 0), (0, 0), (0, LANES - d_rope)))
    return {
        "g_pre": gain(norm_mix_pre), "g_post": gain(norm_mix_post), "g_ffn_pre": gain(norm_ffn_pre),
        "g_ffn_post": gain(norm_ffn_post), "g_mem": gain(norm_mem), "g_q": gain(norm_q_lat),
        "g_kv": gain(norm_kv_lat),
        "w_rq": seg[0], "w_rk": seg[1], "w_rv": seg[2], "w_rg": seg[3], "w_cq": seg[4],
        "w_ckv": seg[5], "w_kpe": jnp.pad(seg[6], ((0, LANES - d_rope), (0, 0))), "w_xq": seg[7],
        "w_gates": seg[8],
        "w_uq_n": bf(uq[:, :, :d_nope].reshape(q_lora, heads * d_nope)),
        "w_uq_p": bf(uq_p.reshape(q_lora, heads * LANES)),
        "w_uk_t": bf(jnp.swapaxes(w_uk[l], 1, 2)),
        "w_kn": bf(jnp.swapaxes(w_uk[l], 0, 1).reshape(kv_lora, heads * d_nope)),
        "w_vn": bf(jnp.swapaxes(w_uv[l], 0, 1).reshape(kv_lora, heads * d_v)),
        "w_uv_h": bf(w_uv[l]),
        "w_mem_k": bf(w_mem_k[l]), "w_mem_v": bf(w_mem_v[l]),
        "w_ret_o": bf(w_ret_o[l]), "w_mla_o": bf(w_mla_o[l]), "w_x_o": bf(w_x_o[l]), "w_out": bf(w_out[l]),
        "w_ffn_gate": bf(w_ffn_gate[l]), "w_ffn_up": bf(w_ffn_up[l]), "w_ffn_down": bf(w_ffn_down[l]),
    }


def _pad_tokens(a, b, tokens, rows=SAMPLE_TOK_PAD):
    a = a.reshape(b, tokens, a.shape[-1])
    return jnp.pad(a, ((0, 0), (0, rows - tokens), (0, 0)))


def kernel(x_prompt, x_sample, mem_prompt, cache_ckv, cache_kpe, page_table, state_ret, cache_mem_k, cache_mem_v,
           norm_mix_pre, norm_mix_post, norm_ffn_pre, norm_ffn_post, norm_mem, norm_q_lat, norm_kv_lat, w_in,
           w_uq, w_uk, w_uv, w_mem_k, w_mem_v, w_ret_o, w_mla_o, w_x_o, w_out, w_ffn_gate, w_ffn_up, w_ffn_down):
    depth = w_in.shape[0]
    batch, seq, d_model = x_prompt.shape
    db, tokens, _ = x_sample.shape
    ret_heads, ret_dk, ret_dv = state_ret.shape[2:]
    heads, kv_lora, d_nope = w_uk.shape[1:]
    d_rope = cache_kpe.shape[-1]
    d_v = w_uv.shape[-1]
    n_mem, x_heads, x_hd = cache_mem_k.shape[2:]
    q_lora = w_uq.shape[1]
    past_len = page_table.shape[1] * cache_ckv.shape[2]
    dims = dict(heads=heads, d_nope=d_nope, d_rope=d_rope, kv_lora=kv_lora, d_v=d_v,
                ret_heads=ret_heads, ret_dk=ret_dk, ret_dv=ret_dv)
    assert ret_dk == LANES and d_nope == LANES and d_rope <= LANES // 2
    assert tokens <= min(SAMPLE_TOK_PAD, X_SAMPLE_TOK_PAD)
    sizes = (ret_heads * ret_dk, ret_heads * ret_dk, ret_heads * ret_dv, ret_heads * ret_dv, q_lora, kv_lora,
             d_rope, x_heads * x_hd, w_in.shape[2] - (2 * ret_heads * ret_dk + 2 * ret_heads * ret_dv + q_lora
                                                       + kv_lora + d_rope + x_heads * x_hd))

    cache_kpe_t = jnp.swapaxes(cache_kpe, 2, 3)

    tabs_p = _rope_tables(jnp.arange(seq), ret_dk, d_rope)
    n_s = db * tokens
    tabs_s = _rope_tables(past_len + (jnp.arange(n_s) % tokens), ret_dk, d_rope)

    y_p = x_prompt.reshape(batch * seq, d_model)
    y_s = x_sample.reshape(n_s, d_model)
    outs = [[] for _ in range(8)]
    for l in range(depth):
        w = _layer_weights(l, dims, sizes, norm_mix_pre, norm_mix_post, norm_ffn_pre, norm_ffn_post, norm_mem,
                           norm_q_lat, norm_kv_lat, w_in, w_uq, w_uk, w_uv, w_mem_k, w_mem_v, w_ret_o, w_mla_o,
                           w_x_o, w_out, w_ffn_gate, w_ffn_up, w_ffn_down)

        mk_p, mv_p = _mem_kv(mem_prompt, w["g_mem"], w["w_mem_k"], w["w_mem_v"])
        rq, rk, rv, rg, qcat, xq, ckv_p, kpe_p, kcat, vn = _in_proj(y_p, w, tabs_p, dims, absorbed=False,
                                                                    tm=ROW_TILE)
        o_ret, ret_p = _ret_prompt(rq, rk, rv, rg, batch, seq, dims)
        o_mla = _mla_prompt(qcat, kcat, vn, batch, seq, dims)
        h_p = _merge(y_p, o_ret, o_mla, xq, w, dims, absorbed=False, tm=ROW_TILE, mem=(mk_p, mv_p, x_heads, seq))
        y_p = _ffn(h_p, w, tm=ROW_TILE)

        rq, rk, rv, rg, qcat, xq, ckv_s, kpe_s, qlat = _in_proj(y_s, w, tabs_s, dims, absorbed=True, tm=ROW_TILE)
        pad = lambda a: _pad_tokens(a, db, tokens)
        o_ret, ret_s = _ret_sample(pad(rq), pad(rk), pad(rv), pad(rg), state_ret[l].astype(F32), tokens, dims)
        o_ret = o_ret[:, :tokens].reshape(n_s, ret_heads * ret_dv)
        group = d_nope + LANES
        q_pe = qcat.reshape(db, tokens * heads, group)[:, :, d_nope:d_nope + d_rope]
        o_lat = _mla_sample(qlat.reshape(db, tokens * heads, kv_lora), q_pe, pad(ckv_s), pad(kpe_s),
                            cache_ckv, cache_kpe_t, l, page_table, tokens, dims)
        o_lat = o_lat.reshape(n_s, heads * kv_lora)
        mem_t = lambda c: jnp.transpose(c[l], (0, 2, 3, 1)).reshape(db, x_heads * x_hd, n_mem)
        o_x = _x_sample(_pad_tokens(xq.astype(F32), db, tokens, rows=X_SAMPLE_TOK_PAD), mem_t(cache_mem_k),
                        mem_t(cache_mem_v), x_heads)
        o_x = o_x[:, :tokens].reshape(n_s, x_heads * x_hd)
        h_s = _merge(y_s, o_ret, o_lat, o_x, w, dims, absorbed=True, tm=ROW_TILE)
        y_s = _ffn(h_s, w, tm=ROW_TILE)

        for lst, val in zip(outs, (ckv_p.reshape(batch, seq, kv_lora), jnp.swapaxes(kpe_p, 1, 2),
                                   ckv_s.reshape(db, tokens, kv_lora), kpe_s.reshape(db, tokens, d_rope),
                                   ret_p.astype(x_prompt.dtype), ret_s.astype(state_ret.dtype),
                                   mk_p.reshape(batch, n_mem, x_heads, x_hd),
                                   mv_p.reshape(batch, n_mem, x_heads, x_hd))):
            lst.append(val)

    return (y_p.reshape(batch, seq, d_model), y_s.reshape(db, tokens, d_model)) + tuple(jnp.stack(o) for o in outs)
```

```python
import functools
import math

import numpy as np
import jax
import jax.numpy as jnp
from jax import lax
from jax.experimental import pallas as pl
from jax.experimental.pallas import tpu as pltpu

F32 = jnp.float32
BF16 = jnp.bfloat16

ROPE_BASE = 10000.0
RMS_EPS = 1e-6
LANES = 128
VMEM_LIMIT = 56 * 1024 * 1024
NEG_BIG = -1e30

ROW_TILE = 512
RET_CHUNK = 256
RET_SEQS_PER_STEP = 4
ATTN_Q_TILE = 512
ATTN_KV_TILE = 512
ATTN_HEADS_PER_STEP = 8
SAMPLE_TOK_PAD = 16
X_SAMPLE_TOK_PAD = 8
RET_SAMPLE_GROUP = 16
X_SAMPLE_GROUP = 16
DECODE_CHUNK = 4096
DMA_LOOP_UNROLL = 4


def _resident(shape):
    nd = len(shape)
    return pl.BlockSpec(shape, lambda *_: (0,) * nd, pipeline_mode=pl.Buffered(1))


def _params(n_axes):
    return pltpu.CompilerParams(dimension_semantics=("arbitrary",) * n_axes, vmem_limit_bytes=VMEM_LIMIT)


def _rms(x, g=None):
    y = x * lax.rsqrt(jnp.mean(x * x, axis=-1, keepdims=True) + RMS_EPS)
    return y if g is None else y * g


def _sigmoid(x):
    return 1.0 / (1.0 + jnp.exp(-x))


def _dot(a, b):
    return jnp.dot(a, b, preferred_element_type=F32)


def _dot_nt(a, b):
    return lax.dot_general(a, b, (((1,), (1,)), ((), ())), preferred_element_type=F32)


def _dot_tn(a, b):
    return lax.dot_general(a, b, (((0,), (0,)), ((), ())), preferred_element_type=F32)


def _rope_half_vreg(z, cos_t, sin_lo, sin_hi, quarter):
    return (z * cos_t + pltpu.roll(z, LANES - quarter, 1) * sin_lo + pltpu.roll(z, quarter, 1) * sin_hi)


def _in_proj_kernel(x_ref, g_ref, wrq, wrk, wrv, wrg, wcq, wckv, wkpe, wxq, gq_ref, gkv_ref, wuqn, wuqp,
                    cosr, sinr, cosp, sinlo, sinhi, *rest,
                    ret_heads, ret_dk, heads, d_nope, d_rope, kv_lora, absorbed, q_scale):
    if absorbed:
        wa, (rq_o, rk_o, rv_o, rg_o, qcat_o, xq_o, ckv_o, kpe_o, a_o) = rest[0], rest[1:]
    else:
        wa, wb, (rq_o, rk_o, rv_o, rg_o, qcat_o, xq_o, ckv_o, kpe_o, a_o, b_o) = rest[0], rest[1], rest[2:]
    u = _rms(x_ref[...], g_ref[...]).astype(BF16)
    cr, sr = cosr[...], sinr[...]
    cp, slo, shi = cosp[...], sinlo[...], sinhi[...]
    group = d_nope + LANES
    k_scale = ret_dk ** -0.5

    cq = _dot_nt(u, wcq[...])
    ckv = _dot_nt(u, wckv[...])
    kpe_raw = _dot_nt(u, wkpe[...])
    zq = _dot_nt(u, wrq[...])
    zk = _dot_nt(u, wrk[...])

    cqn = _rms(cq, gq_ref[...]).astype(BF16)
    ckvn = _rms(ckv, gkv_ref[...])
    ckv_o[...] = ckvn
    kpe = _rope_half_vreg(kpe_raw, cp, slo, shi, d_rope // 2)
    if absorbed:
        kpe_o[...] = kpe[:, :d_rope]
    else:
        kpe_o[0] = kpe.T[:d_rope, :]

    qn = _dot(cqn, wuqn[...])
    qp = _dot(cqn, wuqp[...])
    if not absorbed:
        ckvb = ckvn.astype(BF16)
        kn = _dot(ckvb, wa[...])
        vn = _dot(ckvb, wb[...])

    for h in range(ret_heads):
        sl = slice(h * ret_dk, (h + 1) * ret_dk)
        q_h = zq[:, sl]
        k_h = zk[:, sl]
        rq_o[:, sl] = (q_h * cr + pltpu.roll(q_h, ret_dk // 2, 1) * sr).astype(BF16)
        rk_o[:, sl] = ((k_h * cr + pltpu.roll(k_h, ret_dk // 2, 1) * sr) * k_scale).astype(BF16)

    rv = _dot_nt(u, wrv[...])
    rg = _dot_nt(u, wrg[...])
    xq = _dot_nt(u, wxq[...])

    for h in range(heads):
        qn_h = qn[:, h * d_nope:(h + 1) * d_nope]
        qp_h = _rope_half_vreg(qp[:, h * LANES:(h + 1) * LANES], cp, slo, shi, d_rope // 2)
        if absorbed:
            a_o[:, h * kv_lora:(h + 1) * kv_lora] = _dot(qn_h.astype(BF16), wa[h]).astype(BF16)
        else:
            qn_h = qn_h * q_scale
            qp_h = qp_h * q_scale
        qcat_o[:, h * group:h * group + d_nope] = qn_h.astype(BF16)
        qcat_o[:, h * group + d_nope:(h + 1) * group] = qp_h.astype(BF16)
    if not absorbed:
        kpeb = kpe.astype(BF16)
        for h in range(heads):
            a_o[:, h * group:h * group + d_nope] = kn[:, h * d_nope:(h + 1) * d_nope].astype(BF16)
            a_o[:, h * group + d_nope:(h + 1) * group] = kpeb
        b_o[...] = vn.astype(BF16)

    rv_o[...] = rv.astype(BF16)
    rg_o[...] = rg.astype(BF16)
    xq_o[...] = xq.astype(BF16)


def _in_proj(x, w, tabs, dims, *, absorbed, tm):
    n, d_model = x.shape
    heads, d_nope, d_rope, kv_lora = dims["heads"], dims["d_nope"], dims["d_rope"], dims["kv_lora"]
    ret_heads, ret_dk, ret_dv = dims["ret_heads"], dims["ret_dk"], dims["ret_dv"]
    group = d_nope + LANES
    tab_rows = tabs["cosr"].shape[0]
    tab_tiles = tab_rows // tm
    row = lambda c: pl.BlockSpec((tm, c), lambda i: (i, 0))
    tab = lambda: pl.BlockSpec((tm, LANES), lambda i: (i % tab_tiles, 0))
    weights = [w["g_pre"], w["w_rq"], w["w_rk"], w["w_rv"], w["w_rg"], w["w_cq"], w["w_ckv"], w["w_kpe"],
               w["w_xq"], w["g_q"], w["g_kv"], w["w_uq_n"], w["w_uq_p"]]
    out_cols = [(ret_heads * ret_dk, BF16), (ret_heads * ret_dk, BF16), (ret_heads * ret_dv, BF16),
                (ret_heads * ret_dv, BF16), (heads * group, BF16), (w["w_xq"].shape[0], BF16), (kv_lora, F32),
                (d_rope, F32)]
    if absorbed:
        mode_weights = [w["w_uk_t"]]
        out_cols += [(heads * kv_lora, BF16)]
    else:
        mode_weights = [w["w_kn"], w["w_vn"]]
        out_cols += [(heads * group, BF16), (w["w_vn"].shape[1], BF16)]
    kern = functools.partial(_in_proj_kernel, ret_heads=ret_heads, ret_dk=ret_dk, heads=heads, d_nope=d_nope,
                             d_rope=d_rope, kv_lora=kv_lora, absorbed=absorbed,
                             q_scale=(d_nope + d_rope) ** -0.5 * math.log2(math.e))
    out_specs = [row(c) for c, _ in out_cols]
    out_shape = [jax.ShapeDtypeStruct((n, c), dt) for c, dt in out_cols]
    if not absorbed:
        kpe_slot = 7
        out_specs[kpe_slot] = pl.BlockSpec((1, d_rope, tm), lambda i: (i // tab_tiles, 0, i % tab_tiles))
        out_shape[kpe_slot] = jax.ShapeDtypeStruct((n // tab_rows, d_rope, tab_rows), F32)
    return pl.pallas_call(
        kern,
        grid=(n // tm,),
        in_specs=([row(d_model)] + [_resident(a.shape) for a in weights] + [tab() for _ in range(5)]
                  + [_resident(a.shape) for a in mode_weights]),
        out_specs=out_specs,
        out_shape=out_shape,
        compiler_params=_params(1),
        name="in_proj_sample" if absorbed else "in_proj_prompt",
    )(x, *weights, tabs["cosr"], tabs["sinr"], tabs["cosp"], tabs["sinlo"], tabs["sinhi"], *mode_weights)


def _ret_head(q, k, v, g, s, dec, qd, kd, g_l):
    inner = _dot_nt(q, k) * dec
    o = _dot(inner.astype(BF16), v) + _dot((q.astype(F32) * qd).astype(BF16), s.astype(BF16))
    s_new = s * g_l + _dot_tn((k.astype(F32) * kd).astype(BF16), v)
    gf = g.astype(F32)
    o = (gf * _sigmoid(gf)) * _rms(o)
    return o.astype(BF16), s_new


def _ret_prompt_kernel(rq, rk, rv, rg, dec, qd, kd, o_ref, s_out, s_scr, *, heads, dk, dv, g_l):
    c = pl.program_id(1)

    @pl.when(c == 0)
    def _():
        s_scr[...] = jnp.zeros(s_scr.shape, F32)

    for b in range(rq.shape[0]):
        for h in range(heads):
            o, s_new = _ret_head(rq[b, :, h * dk:(h + 1) * dk], rk[b, :, h * dk:(h + 1) * dk],
                                 rv[b, :, h * dv:(h + 1) * dv], rg[b, :, h * dv:(h + 1) * dv],
                                 s_scr[b, h], dec[h], qd[h], kd[h], g_l[h])
            o_ref[b, :, h * dv:(h + 1) * dv] = o
            s_scr[b, h] = s_new

    @pl.when(c == pl.num_programs(1) - 1)
    def _():
        s_out[...] = s_scr[...]


def _ret_consts(heads, length, dk, chunk_rows=None):
    rows = length if chunk_rows is None else chunk_rows
    lg = np.log1p(-np.exp2(-5.0 - np.arange(heads, dtype=np.float64)))
    i = np.arange(rows, dtype=np.float64)
    diff = i[:, None] - i[None, :]
    valid = (diff >= 0) & (i[:, None] < length) & (i[None, :] < length)
    dec = np.where(valid[None], np.exp(np.maximum(diff, 0.0)[None] * lg[:, None, None]), 0.0)
    qd = np.exp((i[None, :] + 1.0) * lg[:, None])
    kd = np.where(i[None, :] < length, np.exp((length - 1.0 - i[None, :]) * lg[:, None]), 0.0)
    qd = np.broadcast_to(qd[:, :, None], (heads, rows, dk))
    kd = np.broadcast_to(kd[:, :, None], (heads, rows, dk))
    g_l = tuple(float(v) for v in np.exp(length * lg))
    return (jnp.asarray(dec, F32), jnp.asarray(qd, F32), jnp.asarray(kd, F32), g_l)


def _ret_prompt(rq, rk, rv, rg, batch, seq, dims):
    heads, dk, dv = dims["ret_heads"], dims["ret_dk"], dims["ret_dv"]
    L = RET_CHUNK
    nc = seq // L
    dec, qd, kd, g_l = _ret_consts(heads, L, dk)
    nb = RET_SEQS_PER_STEP
    row = lambda c: pl.BlockSpec((nb, L, c), lambda b, i: (b, i, 0))
    per_seq = lambda a: a.reshape(batch, seq, a.shape[-1])
    kern = functools.partial(_ret_prompt_kernel, heads=heads, dk=dk, dv=dv, g_l=g_l)
    o, state = pl.pallas_call(
        kern,
        grid=(batch // nb, nc),
        in_specs=[row(heads * dk), row(heads * dk), row(heads * dv), row(heads * dv),
                  _resident(dec.shape), _resident(qd.shape), _resident(kd.shape)],
        out_specs=[row(heads * dv), pl.BlockSpec((nb, heads, dk, dv), lambda b, i: (b, 0, 0, 0))],
        out_shape=[jax.ShapeDtypeStruct((batch, seq, heads * dv), BF16),
                   jax.ShapeDtypeStruct((batch, heads, dk, dv), F32)],
        scratch_shapes=[pltpu.VMEM((nb, heads, dk, dv), F32)],
        compiler_params=_params(2),
        name="retention_prompt",
    )(per_seq(rq), per_seq(rk), per_seq(rv), per_seq(rg), dec, qd, kd)
    return o.reshape(batch * seq, heads * dv), state


def _ret_sample_kernel(r, s0, dec, qd, kd, o_ref, s_out, *, group, heads, dk, dv, g_l):
    nk, nv = heads * dk, heads * dv
    rq, rk = r.at[:, :, 0:nk], r.at[:, :, nk:2 * nk]
    rv, rg = r.at[:, :, 2 * nk:2 * nk + nv], r.at[:, :, 2 * nk + nv:2 * nk + 2 * nv]
    pairs = [(i, h) for i in range(group) for h in range(heads)]

    def state_side(i, h):
        q = rq[i, :, h * dk:(h + 1) * dk]
        k = rk[i, :, h * dk:(h + 1) * dk]
        v = rv[i, :, h * dv:(h + 1) * dv]
        s = s0[i, h]
        read = _dot((q.astype(F32) * qd[h]).astype(BF16), s.astype(BF16))
        s_out[i, h] = s * g_l[h] + _dot_tn((k.astype(F32) * kd[h]).astype(BF16), v)
        return read

    def output_side(i, h, read):
        q = rq[i, :, h * dk:(h + 1) * dk]
        k = rk[i, :, h * dk:(h + 1) * dk]
        v = rv[i, :, h * dv:(h + 1) * dv]
        o = _dot((_dot_nt(q, k) * dec[h]).astype(BF16), v) + read
        gf = rg[i, :, h * dv:(h + 1) * dv].astype(F32)
        o_ref[i, :, h * dv:(h + 1) * dv] = ((gf * _sigmoid(gf)) * _rms(o)).astype(BF16)

    read = state_side(*pairs[0])
    for n, (i, h) in enumerate(pairs):
        nxt = state_side(*pairs[n + 1]) if n + 1 < len(pairs) else None
        output_side(i, h, read)
        read = nxt


def _ret_sample(r, state, tokens, dims):
    heads, dk, dv = dims["ret_heads"], dims["ret_dk"], dims["ret_dv"]
    b, tp, _ = r.shape
    G = RET_SAMPLE_GROUP
    dec, qd, kd, g_l = _ret_consts(heads, tokens, dk, chunk_rows=tp)
    blk = lambda c: pl.BlockSpec((G, tp, c), lambda i: (i, 0, 0))
    st = pl.BlockSpec((G, heads, dk, dv), lambda i: (i, 0, 0, 0))
    kern = functools.partial(_ret_sample_kernel, group=G, heads=heads, dk=dk, dv=dv, g_l=g_l)
    return pl.pallas_call(
        kern,
        grid=(b // G,),
        in_specs=[blk(r.shape[2]), st, _resident(dec.shape), _resident(qd.shape), _resident(kd.shape)],
        out_specs=[blk(heads * dv), st],
        out_shape=[jax.ShapeDtypeStruct((b, tp, heads * dv), BF16),
                   jax.ShapeDtypeStruct((b, heads, dk, dv), F32)],
        compiler_params=_params(1),
        name="retention_sample",
    )(r, state, dec, qd, kd)


def _mla_prompt_kernel(q_ref, k_ref, v_ref, o_ref, m_scr, l_scr, acc_scr, *, tq, tk, hp, group, d_v):
    qi = pl.program_id(2)
    ri = lax.broadcasted_iota(jnp.int32, (tq, tk), 0)
    ci = lax.broadcasted_iota(jnp.int32, (tq, tk), 1)
    causal_bias = jnp.where(ci <= ri, 0.0, NEG_BIG)

    def block(j, carry, first):
        start = pl.multiple_of(j * tk, tk)

        def scores(h):
            return _dot_nt(q_ref[:, h * group:(h + 1) * group], k_ref[pl.ds(start, tk), h * group:(h + 1) * group])

        s_next = scores(0)
        for h in range(hp):
            s = s_next
            if h + 1 < hp:
                s_next = scores(h + 1)
            vals = v_ref[pl.ds(start, tk), h * d_v:(h + 1) * d_v]
            if first:
                s = s + causal_bias
                m_new = jnp.broadcast_to(jnp.max(s, axis=-1, keepdims=True), (tq, LANES))
                p = jnp.exp2(s - jnp.tile(m_new, (1, tk // LANES)))
                l_scr[h] = jnp.broadcast_to(jnp.sum(p, axis=-1, keepdims=True), (tq, LANES))
                acc_scr[h] = _dot(p.astype(BF16), vals)
            else:
                m_prev = m_scr[h]
                m_new = jnp.maximum(m_prev, jnp.max(s, axis=-1, keepdims=True))
                a = jnp.exp2(m_prev - m_new)
                p = jnp.exp2(s - jnp.tile(m_new, (1, tk // LANES)))
                l_scr[h] = a * l_scr[h] + jnp.sum(p, axis=-1, keepdims=True)
                acc_scr[h] = a * acc_scr[h] + _dot(p.astype(BF16), vals)
            m_scr[h] = m_new
        return carry

    block(qi, 0, True)
    lax.fori_loop(0, qi, functools.partial(block, first=False), 0)
    for h in range(hp):
        o_ref[:, h * d_v:(h + 1) * d_v] = (acc_scr[h] / l_scr[h]).astype(BF16)


def _mla_prompt(qcat, kcat, v, batch, seq, dims):
    heads, d_nope, d_v = dims["heads"], dims["d_nope"], dims["d_v"]
    group = d_nope + LANES
    t = ATTN_Q_TILE
    hp = ATTN_HEADS_PER_STEP
    nq = seq // t
    assert ATTN_KV_TILE == t and d_v == LANES
    kern = functools.partial(_mla_prompt_kernel, tq=t, tk=ATTN_KV_TILE, hp=hp, group=group, d_v=d_v)
    return pl.pallas_call(
        kern,
        grid=(batch, heads // hp, nq),
        in_specs=[pl.BlockSpec((t, hp * group), lambda b, h, i: (b * nq + i, h)),
                  pl.BlockSpec((seq, hp * group), lambda b, h, i: (b, h)),
                  pl.BlockSpec((seq, hp * d_v), lambda b, h, i: (b, h))],
        out_specs=pl.BlockSpec((t, hp * d_v), lambda b, h, i: (b * nq + i, h)),
        out_shape=jax.ShapeDtypeStruct((batch * seq, heads * d_v), BF16),
        scratch_shapes=[pltpu.VMEM((hp, t, LANES), F32), pltpu.VMEM((hp, t, LANES), F32),
                        pltpu.VMEM((hp, t, d_v), F32)],
        compiler_params=_params(3),
        name="mla_prompt",
    )(qcat, kcat, v)


def _mla_sample_kernel(pt_ref, ql_ref, qp_ref, cn_ref, kn_ref, ckv_hbm, kpe_hbm, o_ref,
                       ckv_buf, kpe_buf, kbf, s_scr, sem,
                       *, layer, n_pages, page, chunk, tokens, heads, scale):
    b = pl.program_id(0)
    past = n_pages * page
    d_rope = qp_ref.shape[2]

    def page_copy(seq, slot, p, array):
        pg = pt_ref[seq, p]
        if array == 0:
            off = pl.multiple_of(p * page, page)
            return pltpu.make_async_copy(ckv_hbm.at[layer, pg], ckv_buf.at[slot, pl.ds(off, page), :],
                                         sem.at[0, slot])
        off = pl.multiple_of(p * d_rope, d_rope)
        return pltpu.make_async_copy(kpe_hbm.at[layer, pg], kpe_buf.at[slot, pl.ds(off, d_rope), :],
                                     sem.at[1, slot])

    def for_each_page(seq, slot, fn, arrays=(0, 1), unroll=DMA_LOOP_UNROLL):
        def body(p, carry):
            for a in arrays:
                fn(page_copy(seq, slot, p, a))
            return carry
        lax.fori_loop(0, n_pages, body, 0, unroll=unroll)

    @pl.when(b == 0)
    def _():
        for_each_page(0, 0, lambda cp: cp.start())

    @pl.when(b + 1 < pl.num_programs(0))
    def _():
        for_each_page(b + 1, (b + 1) % 2, lambda cp: cp.start())

    slot = b % 2
    for_each_page(b, slot, lambda cp: cp.wait(), arrays=(0,), unroll=True)
    for_each_page(b, slot, lambda cp: cp.wait(), arrays=(1,), unroll=True)

    ql = ql_ref[0]
    qp = qp_ref[0]
    ckv_s = ckv_buf.at[slot]
    kpe_s = kpe_buf.at[slot]
    n_chunks = past // chunk

    cn = cn_ref[0].astype(BF16)
    kn = kn_ref[0].astype(BF16)
    s_new = (_dot_nt(ql, cn) + _dot_nt(qp, kn)) * scale
    t = lax.broadcasted_iota(jnp.int32, s_new.shape, 0) // heads
    col = lax.broadcasted_iota(jnp.int32, s_new.shape, 1)
    s_new = jnp.where((col <= t) & (col < tokens), s_new, NEG_BIG)
    m = jnp.max(s_new, axis=-1, keepdims=True)
    p_new = jnp.exp(s_new - m)
    l = jnp.sum(p_new, axis=-1, keepdims=True)
    acc = _dot(p_new.astype(BF16), cn)

    def scores(c):
        sl = slice(c * chunk, (c + 1) * chunk)
        kc = ckv_s[sl, :].astype(BF16)
        kbf[sl, :] = kc
        pe = jnp.concatenate([kpe_s[p * d_rope:(p + 1) * d_rope, :]
                              for p in range(c * chunk // page, (c + 1) * chunk // page)], axis=1)
        s_scr[:, sl] = (_dot_nt(ql, kc) + _dot(qp, pe.astype(BF16))) * scale

    def values(c, m, l, acc):
        sl = slice(c * chunk, (c + 1) * chunk)
        s = s_scr[:, sl]
        m_new = jnp.maximum(m, jnp.max(s, axis=-1, keepdims=True))
        a = jnp.exp(m - m_new)
        p = jnp.exp(s - m_new)
        l = a * l + jnp.sum(p, axis=-1, keepdims=True)
        return m_new, l, a * acc + _dot(p.astype(BF16), kbf[sl, :])

    scores(0)
    for c in range(n_chunks):
        if c + 1 < n_chunks:
            scores(c + 1)
        m, l, acc = values(c, m, l, acc)
    o_ref[0] = acc / l


def _mla_sample(ql, qp, ckv_new, kpe_new, cache_ckv, cache_kpe_t, layer, page_table, tokens, dims):
    heads, d_nope, d_rope, kv_lora = dims["heads"], dims["d_nope"], dims["d_rope"], dims["kv_lora"]
    b, rows, _ = ql.shape
    n_pages = page_table.shape[1]
    page = cache_ckv.shape[2]
    past = n_pages * page
    tp = ckv_new.shape[1]
    per_b = lambda r, c: pl.BlockSpec((1, r, c), lambda i, pt: (i, 0, 0))
    hbm = pl.BlockSpec(memory_space=pl.ANY)
    kern = functools.partial(_mla_sample_kernel, layer=layer, n_pages=n_pages, page=page, chunk=DECODE_CHUNK,
                             tokens=tokens, heads=heads, scale=(d_nope + d_rope) ** -0.5)
    grid_spec = pltpu.PrefetchScalarGridSpec(
        num_scalar_prefetch=1,
        grid=(b,),
        in_specs=[per_b(rows, kv_lora), per_b(rows, d_rope), per_b(tp, kv_lora), per_b(tp, d_rope), hbm, hbm],
        out_specs=per_b(rows, kv_lora),
        scratch_shapes=[pltpu.VMEM((2, past, kv_lora), F32), pltpu.VMEM((2, n_pages * d_rope, page), F32),
                        pltpu.VMEM((past, kv_lora), BF16), pltpu.VMEM((rows, past), F32),
                        pltpu.SemaphoreType.DMA((2, 2))],
    )
    return pl.pallas_call(
        kern,
        grid_spec=grid_spec,
        out_shape=jax.ShapeDtypeStruct((b, rows, kv_lora), F32),
        compiler_params=_params(1),
        name="mla_sample",
    )(page_table, ql, qp, ckv_new, kpe_new, cache_ckv, cache_kpe_t)


def _mem_kv_kernel(m_ref, g_ref, wk, wv, k_o, v_o):
    mn = _rms(m_ref[0], g_ref[...]).astype(BF16)
    k_o[0] = _dot(mn, wk[...])
    v_o[0] = _dot(mn, wv[...])


def _mem_kv(mem, g, wk, wv):
    b, m, d = mem.shape
    c = wk.shape[1]
    return pl.pallas_call(
        _mem_kv_kernel,
        grid=(b,),
        in_specs=[pl.BlockSpec((1, m, d), lambda i: (i, 0, 0)), _resident(g.shape), _resident(wk.shape),
                  _resident(wv.shape)],
        out_specs=[pl.BlockSpec((1, m, c), lambda i: (i, 0, 0))] * 2,
        out_shape=[jax.ShapeDtypeStruct((b, m, c), F32)] * 2,
        compiler_params=_params(1),
        name="mem_kv",
    )(mem, g, wk, wv)


def _x_attend_heads(q, mk, mv, heads, hd):
    r = q.shape[0]
    lane = lax.broadcasted_iota(jnp.int32, q.shape, 1)
    sels = [(lane >= h * hd) & (lane < (h + 1) * hd) for h in range(heads)]
    q_heads = jnp.concatenate([jnp.where(sel, q, jnp.zeros_like(q)) for sel in sels], axis=0)
    s = _dot_nt(q_heads, mk) * (hd ** -0.5)
    p = jnp.exp(s - jnp.max(s, axis=-1, keepdims=True))
    p = p / jnp.sum(p, axis=-1, keepdims=True)
    pv = _dot(p.astype(BF16), mv)
    out = jnp.zeros(q.shape, F32)
    for h, sel in enumerate(sels):
        out = out + jnp.where(sel, pv[h * r:(h + 1) * r], 0.0)
    return out


def _x_sample_kernel(q_ref, kt_ref, vt_ref, o_ref, *, group, heads, hd):
    tp = q_ref.shape[1]
    lane = lax.broadcasted_iota(jnp.int32, q_ref.shape[1:], 1)
    sels = [(lane >= h * hd) & (lane < (h + 1) * hd) for h in range(heads)]
    def scores(g):
        q_heads = jnp.concatenate([jnp.where(sel, q_ref[g], 0.0) for sel in sels], axis=0).astype(BF16)
        return _dot(q_heads, kt_ref[g].astype(BF16)) * (hd ** -0.5)

    s_next = scores(0)
    for g in range(group):
        q = q_ref[g]
        s = s_next
        if g + 1 < group:
            s_next = scores(g + 1)
        p = jnp.exp(s - jnp.max(s, axis=-1, keepdims=True))
        p = p / jnp.sum(p, axis=-1, keepdims=True)
        r = _dot_nt(p.astype(BF16), vt_ref[g].astype(BF16))
        out = jnp.zeros(q.shape, F32)
        for h, sel in enumerate(sels):
            out = out + jnp.where(sel, r[h * tp:(h + 1) * tp], 0.0)
        o_ref[g] = out


def _x_sample(xq, mkt, mvt, heads):
    b, tp, c = xq.shape
    m = mkt.shape[2]
    G = X_SAMPLE_GROUP
    kern = functools.partial(_x_sample_kernel, group=G, heads=heads, hd=c // heads)
    return pl.pallas_call(
        kern,
        grid=(b // G,),
        in_specs=[pl.BlockSpec((G, tp, c), lambda i: (i, 0, 0)),
                  pl.BlockSpec((G, c, m), lambda i: (i, 0, 0)),
                  pl.BlockSpec((G, c, m), lambda i: (i, 0, 0))],
        out_specs=pl.BlockSpec((G, tp, c), lambda i: (i, 0, 0)),
        out_shape=jax.ShapeDtypeStruct((b, tp, c), F32),
        compiler_params=_params(1),
        name="x_attend_sample",
    )(xq, mkt, mvt)


def _merge_kernel(x_ref, gpre, wgate, oret, wret, omla, wmla, wuv, ox, wx, wout, gpost, *rest,
                  heads, kv_lora, d_v, absorbed, x_heads):
    h_o = rest[-1]
    if x_heads:
        mk, mv = rest[:2]
        o_x = _x_attend_heads(ox[...], mk[0].astype(BF16), mv[0].astype(BF16), x_heads, ox.shape[1] // x_heads)
    else:
        o_x = ox[...]
    x = x_ref[...]
    d = x.shape[1]
    u = _rms(x, gpre[...]).astype(BF16)
    a_ret = _dot(oret[...], wret[...])
    if absorbed:
        a_mla = jnp.zeros(x.shape, F32)
        for h in range(heads):
            o_h = _dot(omla[:, h * kv_lora:(h + 1) * kv_lora].astype(BF16), wuv[h]).astype(BF16)
            a_mla = a_mla + _dot(o_h, wmla[h * d_v:(h + 1) * d_v, :])
    else:
        a_mla = _dot(omla[...], wmla[...])
    a_x = _dot(o_x.astype(BF16), wx[...])
    mixed = jnp.zeros(x.shape, F32)
    for i, a in enumerate((a_ret, a_mla, a_x)):
        mixed = mixed + _sigmoid(_dot_nt(u, wgate[i * d:(i + 1) * d, :])) * a
    h_o[...] = x + _rms(_dot(mixed.astype(BF16), wout[...]), gpost[...])


def _merge(x, oret, omla, ox, w, dims, *, absorbed, tm, mem=None):
    n, d = x.shape
    row = lambda a: pl.BlockSpec((tm, a.shape[1]), lambda i: (i, 0))
    res = lambda a: _resident(a.shape)
    args = [x, w["g_pre"], w["w_gates"], oret, w["w_ret_o"], omla, w["w_mla_o"], w["w_uv_h"], ox, w["w_x_o"],
            w["w_out"], w["g_post"]]
    specs = [row(x), res(w["g_pre"]), res(w["w_gates"]), row(oret), res(w["w_ret_o"]), row(omla),
             res(w["w_mla_o"]), res(w["w_uv_h"]), row(ox), res(w["w_x_o"]), res(w["w_out"]), res(w["g_post"])]
    x_heads = 0
    if mem is not None:
        mk, mv, x_heads, seq_rows = mem
        tiles_per_seq = seq_rows // tm
        per_seq = pl.BlockSpec((1,) + mk.shape[1:], lambda i: (i // tiles_per_seq, 0, 0))
        args += [mk, mv]
        specs += [per_seq, per_seq]
    kern = functools.partial(_merge_kernel, heads=dims["heads"], kv_lora=dims["kv_lora"], d_v=dims["d_v"],
                             absorbed=absorbed, x_heads=x_heads)
    return pl.pallas_call(
        kern,
        grid=(n // tm,),
        in_specs=specs,
        out_specs=pl.BlockSpec((tm, d), lambda i: (i, 0)),
        out_shape=jax.ShapeDtypeStruct((n, d), F32),
        compiler_params=_params(1),
        name="merge_sample" if absorbed else "merge_prompt",
    )(*args)


def _ffn_kernel(h_ref, gpre, wg, wu, wd, gpost, y_o):
    h = h_ref[...]
    f = _rms(h, gpre[...]).astype(BF16)
    gate = _dot(f, wg[...])
    act = (gate * _sigmoid(gate) * _dot(f, wu[...])).astype(BF16)
    y_o[...] = h + _rms(_dot(act, wd[...]), gpost[...])


def _ffn(h, w, *, tm):
    n, d = h.shape
    res = lambda a: _resident(a.shape)
    return pl.pallas_call(
        _ffn_kernel,
        grid=(n // tm,),
        in_specs=[pl.BlockSpec((tm, d), lambda i: (i, 0)), res(w["g_ffn_pre"]), res(w["w_ffn_gate"]),
                  res(w["w_ffn_up"]), res(w["w_ffn_down"]), res(w["g_ffn_post"])],
        out_specs=pl.BlockSpec((tm, d), lambda i: (i, 0)),
        out_shape=jax.ShapeDtypeStruct((n, d), F32),
        compiler_params=_params(1),
        name="ffn",
    )(h, w["g_ffn_pre"], w["w_ffn_gate"], w["w_ffn_up"], w["w_ffn_down"], w["g_ffn_post"])


def _rope_tables(pos, ret_dk, d_rope):
    posf = pos.astype(F32)[:, None]

    def angles(half):
        inv = ROPE_BASE ** (-jnp.arange(half, dtype=F32) / half)
        ang = posf * inv[None, :]
        return jnp.cos(ang), jnp.sin(ang)

    cr, sr = angles(ret_dk // 2)
    cp, sp = angles(d_rope // 2)
    z = jnp.zeros_like(cp)
    pad = jnp.zeros((pos.shape[0], LANES - d_rope), F32)
    return {
        "cosr": jnp.concatenate([cr, cr], axis=1),
        "sinr": jnp.concatenate([-sr, sr], axis=1),
        "cosp": jnp.concatenate([cp, cp, pad], axis=1),
        "sinlo": jnp.concatenate([-sp, z, pad], axis=1),
        "sinhi": jnp.concatenate([z, sp, pad], axis=1),
    }


def _layer_weights(l, dims, sizes, norm_mix_pre, norm_mix_post, norm_ffn_pre, norm_ffn_post, norm_mem, norm_q_lat,
                   norm_kv_lat, w_in, w_uq, w_uk, w_uv, w_mem_k, w_mem_v, w_ret_o, w_mla_o, w_x_o, w_out,
                   w_ffn_gate, w_ffn_up, w_ffn_down):
    heads, d_nope, d_rope, kv_lora, d_v = (dims[k] for k in ("heads", "d_nope", "d_rope", "kv_lora", "d_v"))
    bf = lambda a: a.astype(BF16)
    gain = lambda a: a[l].astype(F32)[None, :]
    offs = np.concatenate([[0], np.cumsum(sizes)])
    w_in_t = bf(jnp.swapaxes(w_in[l], 0, 1))
    seg = [w_in_t[offs[i]:offs[i + 1], :] for i in range(len(sizes))]
    q_lora = w_uq.shape[1]
    uq = w_uq[l].reshape(q_lora, heads, d_nope + d_rope)
    uq_p = jnp.pad(uq[:, :, d_nope:], ((0, 0), (0, 0), (0, LANES - d_rope)))
    return {
        "g_pre": gain(norm_mix_pre), "g_post": gain(norm_mix_post), "g_ffn_pre": gain(norm_ffn_pre),
        "g_ffn_post": gain(norm_ffn_post), "g_mem": gain(norm_mem), "g_q": gain(norm_q_lat),
        "g_kv": gain(norm_kv_lat),
        "w_rq": seg[0], "w_rk": seg[1], "w_rv": seg[2], "w_rg": seg[3], "w_cq": seg[4],
        "w_ckv": seg[5], "w_kpe": jnp.pad(seg[6], ((0, LANES - d_rope), (0, 0))), "w_xq": seg[7],
        "w_gates": seg[8],
        "w_uq_n": bf(uq[:, :, :d_nope].reshape(q_lora, heads * d_nope)),
        "w_uq_p": bf(uq_p.reshape(q_lora, heads * LANES)),
        "w_uk_t": bf(jnp.swapaxes(w_uk[l], 1, 2)),
        "w_kn": bf(jnp.swapaxes(w_uk[l], 0, 1).reshape(kv_lora, heads * d_nope)),
        "w_vn": bf(jnp.swapaxes(w_uv[l], 0, 1).reshape(kv_lora, heads * d_v)),
        "w_uv_h": bf(w_uv[l]),
        "w_mem_k": bf(w_mem_k[l]), "w_mem_v": bf(w_mem_v[l]),
        "w_ret_o": bf(w_ret_o[l]), "w_mla_o": bf(w_mla_o[l]), "w_x_o": bf(w_x_o[l]), "w_out": bf(w_out[l]),
        "w_ffn_gate": bf(w_ffn_gate[l]), "w_ffn_up": bf(w_ffn_up[l]), "w_ffn_down": bf(w_ffn_down[l]),
    }


def _pad_tokens(a, b, tokens, rows=SAMPLE_TOK_PAD):
    a = a.reshape(b, tokens, a.shape[-1])
    return jnp.pad(a, ((0, 0), (0, rows - tokens), (0, 0)))


def kernel(x_prompt, x_sample, mem_prompt, cache_ckv, cache_kpe, page_table, state_ret, cache_mem_k, cache_mem_v,
           norm_mix_pre, norm_mix_post, norm_ffn_pre, norm_ffn_post, norm_mem, norm_q_lat, norm_kv_lat, w_in,
           w_uq, w_uk, w_uv, w_mem_k, w_mem_v, w_ret_o, w_mla_o, w_x_o, w_out, w_ffn_gate, w_ffn_up, w_ffn_down):
    depth = w_in.shape[0]
    batch, seq, d_model = x_prompt.shape
    db, tokens, _ = x_sample.shape
    ret_heads, ret_dk, ret_dv = state_ret.shape[2:]
    heads, kv_lora, d_nope = w_uk.shape[1:]
    d_rope = cache_kpe.shape[-1]
    d_v = w_uv.shape[-1]
    n_mem, x_heads, x_hd = cache_mem_k.shape[2:]
    q_lora = w_uq.shape[1]
    past_len = page_table.shape[1] * cache_ckv.shape[2]
    dims = dict(heads=heads, d_nope=d_nope, d_rope=d_rope, kv_lora=kv_lora, d_v=d_v,
                ret_heads=ret_heads, ret_dk=ret_dk, ret_dv=ret_dv)
    assert ret_dk == LANES and d_nope == LANES and d_rope <= LANES // 2
    assert tokens <= min(SAMPLE_TOK_PAD, X_SAMPLE_TOK_PAD)
    sizes = (ret_heads * ret_dk, ret_heads * ret_dk, ret_heads * ret_dv, ret_heads * ret_dv, q_lora, kv_lora,
             d_rope, x_heads * x_hd, w_in.shape[2] - (2 * ret_heads * ret_dk + 2 * ret_heads * ret_dv + q_lora
                                                       + kv_lora + d_rope + x_heads * x_hd))

    cache_kpe_t = jnp.swapaxes(cache_kpe, 2, 3)

    tabs_p = _rope_tables(jnp.arange(seq), ret_dk, d_rope)
    n_s = db * tokens
    tabs_s = _rope_tables(past_len + (jnp.arange(n_s) % tokens), ret_dk, d_rope)

    y_p = x_prompt.reshape(batch * seq, d_model)
    y_s = x_sample.reshape(n_s, d_model)
    outs = [[] for _ in range(8)]
    for l in range(depth):
        w = _layer_weights(l, dims, sizes, norm_mix_pre, norm_mix_post, norm_ffn_pre, norm_ffn_post, norm_mem,
                           norm_q_lat, norm_kv_lat, w_in, w_uq, w_uk, w_uv, w_mem_k, w_mem_v, w_ret_o, w_mla_o,
                           w_x_o, w_out, w_ffn_gate, w_ffn_up, w_ffn_down)

        mk_p, mv_p = _mem_kv(mem_prompt, w["g_mem"], w["w_mem_k"], w["w_mem_v"])
        rq, rk, rv, rg, qcat, xq, ckv_p, kpe_p, kcat, vn = _in_proj(y_p, w, tabs_p, dims, absorbed=False,
                                                                    tm=ROW_TILE)
        o_ret, ret_p = _ret_prompt(rq, rk, rv, rg, batch, seq, dims)
        o_mla = _mla_prompt(qcat, kcat, vn, batch, seq, dims)
        h_p = _merge(y_p, o_ret, o_mla, xq, w, dims, absorbed=False, tm=ROW_TILE, mem=(mk_p, mv_p, x_heads, seq))
        y_p = _ffn(h_p, w, tm=ROW_TILE)

        rq, rk, rv, rg, qcat, xq, ckv_s, kpe_s, qlat = _in_proj(y_s, w, tabs_s, dims, absorbed=True, tm=ROW_TILE)
        pad = lambda a: _pad_tokens(a, db, tokens)
        o_ret, ret_s = _ret_sample(pad(jnp.concatenate([rq, rk, rv, rg], axis=1)), state_ret[l].astype(F32),
                                   tokens, dims)
        o_ret = o_ret[:, :tokens].reshape(n_s, ret_heads * ret_dv)
        group = d_nope + LANES
        q_pe = qcat.reshape(db, tokens * heads, group)[:, :, d_nope:d_nope + d_rope]
        o_lat = _mla_sample(qlat.reshape(db, tokens * heads, kv_lora), q_pe, pad(ckv_s), pad(kpe_s),
                            cache_ckv, cache_kpe_t, l, page_table, tokens, dims)
        o_lat = o_lat.reshape(n_s, heads * kv_lora)
        mem_t = lambda c: jnp.transpose(c[l], (0, 2, 3, 1)).reshape(db, x_heads * x_hd, n_mem)
        o_x = _x_sample(_pad_tokens(xq.astype(F32), db, tokens, rows=X_SAMPLE_TOK_PAD), mem_t(cache_mem_k),
                        mem_t(cache_mem_v), x_heads)
        o_x = o_x[:, :tokens].reshape(n_s, x_heads * x_hd)
        h_s = _merge(y_s, o_ret, o_lat, o_x, w, dims, absorbed=True, tm=ROW_TILE)
        y_s = _ffn(h_s, w, tm=ROW_TILE)

        for lst, val in zip(outs, (ckv_p.reshape(batch, seq, kv_lora), jnp.swapaxes(kpe_p, 1, 2),
                                   ckv_s.reshape(db, tokens, kv_lora), kpe_s.reshape(db, tokens, d_rope),
                                   ret_p.astype(x_prompt.dtype), ret_s.astype(state_ret.dtype),
                                   mk_p.reshape(batch, n_mem, x_heads, x_hd),
                                   mv_p.reshape(batch, n_mem, x_heads, x_hd))):
            lst.append(val)

    return (y_p.reshape(batch, seq, d_model), y_s.reshape(db, tokens, d_model)) + tuple(jnp.stack(o) for o in outs)
```
